```python
import math
import jax, jax.numpy as jnp
from jax import lax
import numpy as np

D_MODEL = 4096
BATCH = 1
SEQ = 8192
DEPTH = 2

GRID_W = 64
Q_BLOCK = 128
EPS = 1e-6

A_HEAD_DIM = 128
A_HEADS = 3 * D_MODEL // (8 * A_HEAD_DIM)
A_KV_HEADS = A_HEADS // 3
A_ROPE_THETA = 10000.0

B_QK_DIM = 64
B_V_DIM = 2 * B_QK_DIM
B_HEADS = D_MODEL // (4 * B_V_DIM)
B_ROPE_DIM = B_QK_DIM // 4
ROPE_THETA = 500000.0

C_V_DIM = 128
C_HEADS = 3 * D_MODEL // (8 * C_V_DIM)
C_Q_LORA = D_MODEL // 4
C_KV_LORA = D_MODEL // 8
C_NOPE = 128
C_ROPE = 64
C_QK_DIM = C_NOPE + C_ROPE
C_ROPE_THETA = 10000.0

MIX_WIDTH = A_HEADS * A_HEAD_DIM + B_HEADS * B_V_DIM + C_HEADS * C_V_DIM

IN_SPLITS = (
    A_HEADS * A_HEAD_DIM,
    A_KV_HEADS * A_HEAD_DIM,
    A_KV_HEADS * A_HEAD_DIM,
    B_HEADS * 2 * B_QK_DIM,
    B_HEADS * 2 * B_QK_DIM,
    B_HEADS * B_V_DIM,
    C_Q_LORA,
    C_KV_LORA,
    C_ROPE,
)
N_IN = sum(IN_SPLITS)

FFN_HIDDEN = -(-8 * D_MODEL // (3 * 256)) * 256

kernel_name = "hybrid_parallel_gqa_diff_mla_encoder"


def rmsnorm(x, g):
    xf = x.astype(jnp.float32)
    y = xf * lax.rsqrt(jnp.mean(xf * xf, axis=-1, keepdims=True) + EPS) * g.astype(jnp.float32)
    return y.astype(x.dtype)


def rope_table(pos, dim, theta):
    inv = theta ** (-jnp.arange(0, dim, 2, dtype=jnp.float32) / dim)
    ang = pos.astype(jnp.float32)[:, None] * inv[None, :]
    return jnp.cos(ang), jnp.sin(ang)


def apply_rope(x, tab):
    cos, sin = tab
    n = cos.shape[-1]
    c = cos[:, None, :]
    s = sin[:, None, :]
    x1 = x[..., :n].astype(jnp.float32)
    x2 = x[..., n:].astype(jnp.float32)
    return jnp.concatenate([x1 * c - x2 * s, x2 * c + x1 * s], axis=-1).astype(x.dtype)


def to_blocks(t):
    b, s = t.shape[:2]
    return jnp.moveaxis(t.reshape(b, s // Q_BLOCK, Q_BLOCK, *t.shape[2:]), 1, 0)


def from_blocks(t):
    t = jnp.moveaxis(t, 0, 1)
    return t.reshape(t.shape[0], t.shape[1] * t.shape[2], *t.shape[3:])


def gqa_attention(q, k, v, scale):
    b, s, h, d = q.shape
    hk = k.shape[2]
    qb = to_blocks(q.reshape(b, s, hk, h // hk, d))

    def one_block(qblk):
        sc = jnp.einsum('bqhgd,bkhd->bhgqk', qblk, k).astype(jnp.float32) * scale
        p = jax.nn.softmax(sc, axis=-1).astype(v.dtype)
        return jnp.einsum('bhgqk,bkhe->bqhge', p, v)

    o = from_blocks(lax.map(one_block, qb))
    return o.reshape(b, s, h, v.shape[-1])


def diff_attention(q, k, v, lam, scale):
    qb = to_blocks(q)

    def one_block(qblk):
        sc = jnp.einsum('bqhmd,bkhmd->bhmqk', qblk, k).astype(jnp.float32) * scale
        p = jax.nn.softmax(sc, axis=-1)
        a = (p[:, :, 0] - lam * p[:, :, 1]).astype(v.dtype)
        return jnp.einsum('bhqk,bkhe->bqhe', a, v)

    return from_blocks(lax.map(one_block, qb))


def mixer_a(aq, ak, av, q_norm, k_norm, row_tab, col_tab):
    b, s, _ = aq.shape
    q = rmsnorm(aq.reshape(b, s, A_HEADS, A_HEAD_DIM), q_norm)
    k = rmsnorm(ak.reshape(b, s, A_KV_HEADS, A_HEAD_DIM), k_norm)
    v = av.reshape(b, s, A_KV_HEADS, A_HEAD_DIM)
    half = A_HEAD_DIM // 2
    q = jnp.concatenate([apply_rope(q[..., :half], row_tab), apply_rope(q[..., half:], col_tab)], axis=-1)
    k = jnp.concatenate([apply_rope(k[..., :half], row_tab), apply_rope(k[..., half:], col_tab)], axis=-1)
    o = gqa_attention(q, k, v, 1.0 / math.sqrt(A_HEAD_DIM))
    return o.reshape(b, s, A_HEADS * A_HEAD_DIM)


def mixer_b(bq, bk, bv, q_norm, k_norm, lam_p, subln, lambda_init, tab):
    b, s, _ = bq.shape

    def prep(t, g):
        t = rmsnorm(t.reshape(b, s, B_HEADS * 2, B_QK_DIM), g)
        t = jnp.concatenate([apply_rope(t[..., :B_ROPE_DIM], tab), t[..., B_ROPE_DIM:]], axis=-1)
        return t.reshape(b, s, B_HEADS, 2, B_QK_DIM)

    q = prep(bq, q_norm)
    k = prep(bk, k_norm)
    v = bv.reshape(b, s, B_HEADS, B_V_DIM)
    lp = lam_p.astype(jnp.float32)
    lam = jnp.exp(jnp.sum(lp[0] * lp[1])) - jnp.exp(jnp.sum(lp[2] * lp[3])) + lambda_init
    o = diff_attention(q, k, v, lam, 1.0 / math.sqrt(B_QK_DIM))
    o = rmsnorm(o, subln) * (1.0 - lambda_init)
    return o.reshape(b, s, B_HEADS * B_V_DIM)


def mixer_c(cql, ckvl, ckr, q_a_norm, kv_a_norm, w_q_up, w_kv_up, q_norm, k_norm, tab):
    b, s, _ = cql.shape
    q = (rmsnorm(cql, q_a_norm) @ w_q_up).reshape(b, s, C_HEADS, C_QK_DIM)
    kv = (rmsnorm(ckvl, kv_a_norm) @ w_kv_up).reshape(b, s, C_HEADS, C_NOPE + C_V_DIM)
    k_nope, v = kv[..., :C_NOPE], kv[..., C_NOPE:]
    k_rope = jnp.broadcast_to(ckr[:, :, None, :], (b, s, C_HEADS, C_ROPE))
    k = jnp.concatenate([k_nope, k_rope], axis=-1)
    q = rmsnorm(q, q_norm)
    k = rmsnorm(k, k_norm)
    q = jnp.concatenate([q[..., :C_NOPE], apply_rope(q[..., C_NOPE:], tab)], axis=-1)
    k = jnp.concatenate([k[..., :C_NOPE], apply_rope(k[..., C_NOPE:], tab)], axis=-1)
    o = gqa_attention(q, k, v, 1.0 / math.sqrt(C_QK_DIM))
    return o.reshape(b, s, C_HEADS * C_V_DIM)


def swiglu(x, w_gate, w_up, w_down):
    return (jax.nn.silu(x @ w_gate) * (x @ w_up)) @ w_down


def setup_inputs(seed: int = 0) -> dict:
    key = jax.random.key(seed)
    ks = jax.random.split(key, 24)
    L, D, F = DEPTH, D_MODEL, FFN_HIDDEN

    def nrm(k, shape, scale):
        return jax.random.normal(k, shape, jnp.float32) * scale

    def gain(k, shape):
        return 1.0 + 0.02 * jax.random.normal(k, shape, jnp.float32)

    return {
        "x": nrm(ks[0], (BATCH, SEQ, D), 1.0),
        "attn_norm": gain(ks[1], (L, D)),
        "w_in": nrm(ks[2], (L, D, N_IN), D ** -0.5),
        "a_q_norm": gain(ks[3], (L, A_HEAD_DIM)),
        "a_k_norm": gain(ks[4], (L, A_HEAD_DIM)),
        "b_q_norm": gain(ks[5], (L, B_QK_DIM)),
        "b_k_norm": gain(ks[6], (L, B_QK_DIM)),
        "b_lambda": nrm(ks[7], (L, 4, B_QK_DIM), 0.1),
        "b_subln": gain(ks[8], (L, B_V_DIM)),
        "c_q_a_norm": gain(ks[9], (L, C_Q_LORA)),
        "c_kv_a_norm": gain(ks[10], (L, C_KV_LORA)),
        "c_w_q_up": nrm(ks[11], (L, C_Q_LORA, C_HEADS * C_QK_DIM), C_Q_LORA ** -0.5),
        "c_w_kv_up": nrm(ks[12], (L, C_KV_LORA, C_HEADS * (C_NOPE + C_V_DIM)), C_KV_LORA ** -0.5),
        "c_q_norm": gain(ks[13], (L, C_QK_DIM)),
        "c_k_norm": gain(ks[14], (L, C_QK_DIM)),
        "w_out": nrm(ks[15], (L, MIX_WIDTH, D), MIX_WIDTH ** -0.5),
        "ffn_norm": gain(ks[16], (L, D)),
        "w_gate": nrm(ks[17], (L, D, F), D ** -0.5),
        "w_up": nrm(ks[18], (L, D, F), D ** -0.5),
        "w_down": nrm(ks[19], (L, F, D), F ** -0.5),
    }


def reference(x, attn_norm, w_in, a_q_norm, a_k_norm, b_q_norm, b_k_norm, b_lambda, b_subln,
              c_q_a_norm, c_kv_a_norm, c_w_q_up, c_w_kv_up, c_q_norm, c_k_norm, w_out,
              ffn_norm, w_gate, w_up, w_down):
    b, s, _ = x.shape
    rows = s // GRID_W
    t = jnp.arange(s, dtype=jnp.int32)
    row_idx = jnp.repeat(jnp.arange(rows, dtype=jnp.int32), GRID_W)
    col_idx = jnp.tile(jnp.arange(GRID_W, dtype=jnp.int32), rows)
    row_tab = rope_table(row_idx, A_HEAD_DIM // 2, A_ROPE_THETA)
    col_tab = rope_table(col_idx, A_HEAD_DIM // 2, A_ROPE_THETA)
    b_tab = rope_table(t, B_ROPE_DIM, ROPE_THETA)
    c_tab = rope_table(t, C_ROPE, C_ROPE_THETA)
    split_points = [int(p) for p in np.cumsum(IN_SPLITS)[:-1]]

    h = x
    for l in range(DEPTH):
        lambda_init = 0.8 - 0.6 * math.exp(-0.3 * l)
        xn = rmsnorm(h, attn_norm[l])
        proj = xn @ w_in[l]
        aq, ak, av, bq, bk, bv, cql, ckvl, ckr = jnp.split(proj, split_points, axis=-1)
        ya = mixer_a(aq, ak, av, a_q_norm[l], a_k_norm[l], row_tab, col_tab)
        yb = mixer_b(bq, bk, bv, b_q_norm[l], b_k_norm[l], b_lambda[l], b_subln[l], lambda_init, b_tab)
        yc = mixer_c(cql, ckvl, ckr, c_q_a_norm[l], c_kv_a_norm[l], c_w_q_up[l], c_w_kv_up[l],
                     c_q_norm[l], c_k_norm[l], c_tab)
        h = h + jnp.concatenate([ya, yb, yc], axis=-1) @ w_out[l]
        h = h + swiglu(rmsnorm(h, ffn_norm[l]), w_gate[l], w_up[l], w_down[l])
    return h
```

```python
import functools
import math

import jax
import jax.numpy as jnp
import numpy as np
from jax import lax
from jax.experimental import pallas as pl
from jax.experimental.pallas import tpu as pltpu

D_MODEL = 4096
SEQ = 8192
DEPTH = 2
GRID_W = 64
EPS = 1e-6

A_HEAD_DIM = 128
A_HEADS = 12
A_KV_HEADS = 4
A_ROPE_THETA = 10000.0

B_QK_DIM = 64
B_V_DIM = 128
B_HEADS = 8
B_ROPE_DIM = 16
B_ROPE_THETA = 500000.0

C_V_DIM = 128
C_HEADS = 12
C_Q_LORA = 1024
C_KV_LORA = 512
C_NOPE = 128
C_ROPE = 64
C_QK_DIM = C_NOPE + C_ROPE
C_QK_PAD = 256
C_ROPE_THETA = 10000.0

N_IN = 7232
FFN_HIDDEN = 11008

LANES = 128
VMEM_LIMIT_BYTES = 56 * 1024 * 1024

N_IN_PAD = 7680
FFN_PAD = 11264

LOG2E = math.log2(math.e)

_A_Q0, _A_K0, _A_V0 = 0, 12, 16
_B_Q0, _B_K0, _B_V0 = 20, 28, 36
_C_KR = 56


def _cparams(sem):
    return pltpu.CompilerParams(dimension_semantics=sem, vmem_limit_bytes=VMEM_LIMIT_BYTES)


def _rmsnorm_body(x_ref, g_ref, o_ref):
    x = x_ref[...]
    ms = jnp.mean(x * x, axis=-1, keepdims=True)
    o_ref[...] = (x * lax.rsqrt(ms + EPS) * g_ref[...]).astype(o_ref.dtype)


def _rmsnorm(x, g, tm=256):
    s, d = x.shape
    return pl.pallas_call(
        _rmsnorm_body,
        grid=(s // tm,),
        in_specs=[pl.BlockSpec((tm, d), lambda i: (i, 0)),
                  pl.BlockSpec((1, d), lambda i: (0, 0))],
        out_specs=pl.BlockSpec((tm, d), lambda i: (i, 0)),
        out_shape=jax.ShapeDtypeStruct((s, d), jnp.bfloat16),
        compiler_params=_cparams(("parallel",)),
        name="rmsnorm",
    )(x, g.reshape(1, d))


def _matmul_body(*refs, n_w, has_res, nk):
    x_ref = refs[0]
    w_refs = refs[1:1 + n_w]
    pos = 1 + n_w
    res_ref = refs[pos] if has_res else None
    pos += int(has_res)
    o_ref = refs[pos]
    acc_refs = refs[pos + 1:]

    def epilogue(accs):
        if n_w == 2:
            g, u = accs
            y = g * jax.nn.sigmoid(g) * u
        else:
            y = accs[0]
        if has_res:
            y = y + res_ref[...]
        o_ref[...] = y.astype(o_ref.dtype)

    x = x_ref[...]
    prods = [jnp.dot(x, w[...], preferred_element_type=jnp.float32) for w in w_refs]
    if nk == 1:
        epilogue(prods)
        return

    k = pl.program_id(2)

    @pl.when(k == 0)
    def _():
        for a, p in zip(acc_refs, prods):
            a[...] = p

    @pl.when(k > 0)
    def _():
        for a, p in zip(acc_refs, prods):
            a[...] += p

    @pl.when(k == nk - 1)
    def _():
        epilogue([a[...] for a in acc_refs])


def _matmul(x, ws, res=None, *, out_dtype, tm, tn, tk, name):
    m, kdim = x.shape
    n = ws[0].shape[1]
    nk = kdim // tk
    assert m % tm == 0 and n % tn == 0 and kdim % tk == 0
    n_w = len(ws)
    in_specs = [pl.BlockSpec((tm, tk), lambda i, j, k: (i, k))]
    in_specs += [pl.BlockSpec((tk, tn), lambda i, j, k: (k, j)) for _ in ws]
    args = [x, *ws]
    if res is not None:
        in_specs.append(pl.BlockSpec((tm, tn), lambda i, j, k: (i, j)))
        args.append(res)
    scratch = [pltpu.VMEM((tm, tn), jnp.float32) for _ in ws] if nk > 1 else []
    return pl.pallas_call(
        functools.partial(_matmul_body, n_w=n_w, has_res=res is not None, nk=nk),
        grid=(m // tm, n // tn, nk),
        in_specs=in_specs,
        out_specs=pl.BlockSpec((tm, tn), lambda i, j, k: (i, j)),
        out_shape=jax.ShapeDtypeStruct((m, n), out_dtype),
        scratch_shapes=scratch,
        compiler_params=_cparams(("parallel", "parallel", "arbitrary")),
        name=name,
    )(*args)


def _prep_body(*refs, nx, norm, n_norm, rope_blocks, shifts, scale, transpose):
    x_refs = refs[:nx]
    pos = nx
    g_ref = None
    if norm is not None:
        g_ref = refs[pos]
        pos += 1
    tabs = None
    if rope_blocks:
        tabs = refs[pos:pos + 3]
        pos += 3
    o_ref = refs[pos]

    xs = [r[...] for r in x_refs]
    x = xs[0] if nx == 1 else jnp.concatenate(xs, axis=-1)
    width = x.shape[-1]

    if norm == "full":
        ss = jnp.sum(x * x, axis=-1, keepdims=True)
        x = x * lax.rsqrt(ss * (1.0 / n_norm) + EPS) * g_ref[...]
    elif norm == "half":
        lane = lax.broadcasted_iota(jnp.int32, x.shape, 1)
        lo = lane < (LANES // 2)
        x2 = x * x
        ss_lo = jnp.sum(jnp.where(lo, x2, 0.0), axis=-1, keepdims=True)
        ss_hi = jnp.sum(jnp.where(lo, 0.0, x2), axis=-1, keepdims=True)
        ss = jnp.where(lo, ss_lo, ss_hi)
        x = x * lax.rsqrt(ss * (1.0 / n_norm) + EPS) * g_ref[...]

    if rope_blocks:
        c_ref, s1_ref, s2_ref = tabs
        pieces = []
        for b in range(width // LANES):
            xb = x[:, b * LANES:(b + 1) * LANES]
            if b in rope_blocks:
                sl = slice(b * LANES, (b + 1) * LANES)
                xb = (xb * c_ref[:, sl]
                      + pltpu.roll(xb, shifts[0], 1) * s1_ref[:, sl]
                      + pltpu.roll(xb, shifts[1], 1) * s2_ref[:, sl])
            pieces.append(xb)
        x = pieces[0] if len(pieces) == 1 else jnp.concatenate(pieces, axis=-1)

    if scale != 1.0:
        x = x * scale
    if transpose:
        x = x.T
    o_ref[...] = x.astype(o_ref.dtype).reshape(o_ref.shape)


def _prep(xs, *, n_heads, out_kind, tm, gain=None, norm=None, n_norm=None, tabs=None,
          rope_blocks=(), shifts=(0, 0), scale=1.0, name="prep"):
    s = xs[0][0].shape[0]
    width = sum(bw for _, bw, _ in xs)
    in_specs, args = [], []
    for arr, bw, fn in xs:
        in_specs.append(pl.BlockSpec((tm, bw), functools.partial(lambda i, h, fn: (i, fn(h)), fn=fn)))
        args.append(arr)
    if norm is not None:
        in_specs.append(pl.BlockSpec((1, width), lambda i, h: (0, 0)))
        args.append(gain.reshape(1, width))
    if rope_blocks:
        for t in tabs:
            in_specs.append(pl.BlockSpec((tm, width), lambda i, h: (i, 0)))
            args.append(t)
    if out_kind == "rows":
        out_shape = (n_heads, s, width)
        out_spec = pl.BlockSpec((1, tm, width), lambda i, h: (h, i, 0))
    elif out_kind == "cols":
        out_shape = (n_heads, width, s)
        out_spec = pl.BlockSpec((1, width, tm), lambda i, h: (h, 0, i))
    else:
        out_shape = (n_heads, s // tm, width, tm)
        out_spec = pl.BlockSpec((1, 1, width, tm), lambda i, h: (h, i, 0, 0))
    return pl.pallas_call(
        functools.partial(_prep_body, nx=len(xs), norm=norm, n_norm=n_norm,
                          rope_blocks=tuple(rope_blocks), shifts=shifts, scale=scale,
                          transpose=out_kind != "rows"),
        grid=(s // tm, n_heads),
        in_specs=in_specs,
        out_specs=out_spec,
        out_shape=jax.ShapeDtypeStruct(out_shape, jnp.bfloat16),
        compiler_params=_cparams(("parallel", "arbitrary")),
        name=name,
    )(*args)


def _attn_body(*refs, nchunk, tk, diff, lambda_init):
    if diff:
        qT_ref, k_ref, vT_ref, lam_ref, sub_ref, o_ref, m_ref, l_ref, acc_ref = refs
    else:
        qT_ref, k_ref, vT_ref, o_ref, m_ref, l_ref, acc_ref = refs

    qT = qT_ref[0]
    tq = qT.shape[1]
    if diff:
        row = lax.broadcasted_iota(jnp.int32, qT.shape, 0)
        first = row < B_QK_DIM
        zero = jnp.zeros_like(qT)
        qT = jnp.concatenate([jnp.where(first, qT, zero), jnp.where(first, zero, qT)], axis=1)

    m_ref[...] = jnp.full(m_ref.shape, -jnp.inf, jnp.float32)
    l_ref[...] = jnp.zeros(l_ref.shape, jnp.float32)
    acc_ref[...] = jnp.zeros(acc_ref.shape, jnp.float32)

    def body(j, carry):
        start = pl.multiple_of(j * tk, tk)
        kk = k_ref[0, pl.ds(start, tk), :]
        s = jnp.dot(kk, qT, preferred_element_type=jnp.float32)
        m_old = m_ref[...]
        m_new = jnp.maximum(m_old, jnp.max(s, axis=0, keepdims=True))
        alpha = jnp.exp2(m_old - m_new)
        p = jnp.exp2(s - m_new)
        l_ref[...] = alpha * l_ref[...] + jnp.sum(p, axis=0, keepdims=True)
        pv = jnp.dot(vT_ref[0, j], p.astype(jnp.bfloat16),
                     preferred_element_type=jnp.float32)
        acc_ref[...] = acc_ref[...] * alpha + pv
        m_ref[...] = m_new
        return carry

    lax.fori_loop(0, nchunk, body, 0)

    o = acc_ref[...] / l_ref[...]
    if diff:
        lp = lam_ref[...]
        lam = (jnp.exp(jnp.sum(lp[0:1] * lp[1:2], axis=-1, keepdims=True))
               - jnp.exp(jnp.sum(lp[2:3] * lp[3:4], axis=-1, keepdims=True))
               + lambda_init)
        o = o[:, :tq] - lam * o[:, tq:]
        ms = jnp.mean(o * o, axis=0, keepdims=True)
        o = o * lax.rsqrt(ms + EPS) * sub_ref[...] * (1.0 - lambda_init)
    o_ref[...] = o.T.astype(o_ref.dtype)


def _attention(qT, k, vT, *, tq, diff=False, lam_p=None, subln=None, lambda_init=0.0, name):
    n_heads, dq, s = qT.shape
    n_kv = k.shape[0]
    group = n_heads // n_kv
    _, nchunk, dv, tk = vT.shape
    nq = 2 * tq if diff else tq
    in_specs = [pl.BlockSpec((1, dq, tq), lambda h, i: (h, 0, i)),
                pl.BlockSpec((1, s, dq), lambda h, i: (h // group, 0, 0)),
                pl.BlockSpec((1, nchunk, dv, tk), lambda h, i: (h // group, 0, 0, 0))]
    args = [qT, k, vT]
    if diff:
        in_specs += [pl.BlockSpec(lam_p.shape, lambda h, i: (0, 0)),
                     pl.BlockSpec((dv, 1), lambda h, i: (0, 0))]
        args += [lam_p, subln.reshape(dv, 1)]
    return pl.pallas_call(
        functools.partial(_attn_body, nchunk=nchunk, tk=tk, diff=diff, lambda_init=lambda_init),
        grid=(n_heads, s // tq),
        in_specs=in_specs,
        out_specs=pl.BlockSpec((tq, dv), lambda h, i: (i, h)),
        out_shape=jax.ShapeDtypeStruct((s, n_heads * dv), jnp.bfloat16),
        scratch_shapes=[pltpu.VMEM((1, nq), jnp.float32),
                        pltpu.VMEM((1, nq), jnp.float32),
                        pltpu.VMEM((dv, nq), jnp.float32)],
        compiler_params=_cparams(("parallel", "arbitrary")),
        name=name,
    )(*args)


def _rope_cos_sin(pos, dim, theta):
    inv = theta ** (-jnp.arange(0, dim, 2, dtype=jnp.float32) / dim)
    ang = pos.astype(jnp.float32)[:, None] * inv[None, :]
    return jnp.cos(ang), jnp.sin(ang)


def _rotate_half_tables(cos, sin):
    zero = jnp.zeros_like(sin)
    return (jnp.concatenate([cos, cos], axis=-1),
            jnp.concatenate([-sin, zero], axis=-1),
            jnp.concatenate([zero, sin], axis=-1))


def _rope_tables(s):
    t = jnp.arange(s, dtype=jnp.int32)
    row = t // GRID_W
    col = t % GRID_W
    half = A_HEAD_DIM // 2
    a_row = _rotate_half_tables(*_rope_cos_sin(row, half, A_ROPE_THETA))
    a_col = _rotate_half_tables(*_rope_cos_sin(col, half, A_ROPE_THETA))
    a_tabs = tuple(jnp.concatenate([r, c], axis=-1) for r, c in zip(a_row, a_col))

    b_rot = _rotate_half_tables(*_rope_cos_sin(t, B_ROPE_DIM, B_ROPE_THETA))
    rest = B_QK_DIM - B_ROPE_DIM
    fill = (jnp.ones((s, rest), jnp.float32), jnp.zeros((s, rest), jnp.float32),
            jnp.zeros((s, rest), jnp.float32))
    b_tabs = tuple(jnp.tile(jnp.concatenate([r, f], axis=-1), (1, 2)) for r, f in zip(b_rot, fill))

    c_rot = _rotate_half_tables(*_rope_cos_sin(t, C_ROPE, C_ROPE_THETA))
    ones = jnp.ones((s, C_NOPE), jnp.float32)
    zeros = jnp.zeros((s, C_NOPE), jnp.float32)
    pad = jnp.zeros((s, C_QK_PAD - C_QK_DIM), jnp.float32)
    c_tabs = (jnp.concatenate([ones, c_rot[0], pad], axis=-1),
              jnp.concatenate([zeros, c_rot[1], pad], axis=-1),
              jnp.concatenate([zeros, c_rot[2], pad], axis=-1))
    return a_tabs, b_tabs, c_tabs


def _pad_cols(w, n):
    return jnp.pad(w, ((0, 0), (0, n - w.shape[1])))


def _pad_vec(g, n):
    return jnp.pad(g, (0, n - g.shape[0]))


def kernel(x, attn_norm, w_in, a_q_norm, a_k_norm, b_q_norm, b_k_norm, b_lambda, b_subln,
           c_q_a_norm, c_kv_a_norm, c_w_q_up, c_w_kv_up, c_q_norm, c_k_norm, w_out,
           ffn_norm, w_gate, w_up, w_down):
    bsz, s, d = x.shape
    assert (bsz, s, d) == (1, SEQ, D_MODEL)
    bf = jnp.bfloat16
    a_tabs, b_tabs, c_tabs = _rope_tables(s)
    tp = 512
    h = x.reshape(s, d)

    for l in range(DEPTH):
        lambda_init = 0.8 - 0.6 * math.exp(-0.3 * l)
        w_in_l = _pad_cols(w_in[l].astype(bf), N_IN_PAD)
        wq_up = c_w_q_up[l].astype(bf).reshape(C_Q_LORA, C_HEADS, C_QK_DIM)
        wq_up = jnp.pad(wq_up, ((0, 0), (0, 0), (0, C_QK_PAD - C_QK_DIM))).reshape(C_Q_LORA, C_HEADS * C_QK_PAD)
        wkv_up = c_w_kv_up[l].astype(bf)
        w_out_l = w_out[l].astype(bf)
        w_gate_l = _pad_cols(w_gate[l].astype(bf), FFN_PAD)
        w_up_l = _pad_cols(w_up[l].astype(bf), FFN_PAD)
        w_down_l = jnp.pad(w_down[l].astype(bf), ((0, FFN_PAD - FFN_HIDDEN), (0, 0)))

        xn = _rmsnorm(h, attn_norm[l])
        proj = _matmul(xn, [w_in_l], out_dtype=jnp.float32, tm=1024, tn=512, tk=d, name="in_proj")

        a_scale = LOG2E / math.sqrt(A_HEAD_DIM)
        a_q = _prep([(proj, LANES, lambda hh: _A_Q0 + hh)], n_heads=A_HEADS, out_kind="cols", tm=tp,
                    gain=a_q_norm[l], norm="full", n_norm=A_HEAD_DIM, tabs=a_tabs, rope_blocks=(0,),
                    shifts=(96, 32), scale=a_scale, name="prep_a_q")
        a_k = _prep([(proj, LANES, lambda hh: _A_K0 + hh)], n_heads=A_KV_HEADS, out_kind="rows", tm=tp,
                    gain=a_k_norm[l], norm="full", n_norm=A_HEAD_DIM, tabs=a_tabs, rope_blocks=(0,),
                    shifts=(96, 32), name="prep_a_k")
        a_v = _prep([(proj, LANES, lambda hh: _A_V0 + hh)], n_heads=A_KV_HEADS, out_kind="chunks", tm=tp,
                    name="prep_a_v")
        ya = _attention(a_q, a_k, a_v, tq=512, name="attn_a")

        b_scale = LOG2E / math.sqrt(B_QK_DIM)
        b_q = _prep([(proj, LANES, lambda hh: _B_Q0 + hh)], n_heads=B_HEADS, out_kind="cols", tm=tp,
                    gain=jnp.tile(b_q_norm[l], 2), norm="half", n_norm=B_QK_DIM, tabs=b_tabs,
                    rope_blocks=(0,), shifts=(120, 8), scale=b_scale, name="prep_b_q")
        b_k = _prep([(proj, LANES, lambda hh: _B_K0 + hh)], n_heads=B_HEADS, out_kind="rows", tm=tp,
                    gain=jnp.tile(b_k_norm[l], 2), norm="half", n_norm=B_QK_DIM, tabs=b_tabs,
                    rope_blocks=(0,), shifts=(120, 8), name="prep_b_k")
        b_v = _prep([(proj, LANES, lambda hh: _B_V0 + hh)], n_heads=B_HEADS, out_kind="chunks", tm=tp,
                    name="prep_b_v")
        yb = _attention(b_q, b_k, b_v, tq=256, diff=True, lam_p=b_lambda[l], subln=b_subln[l],
                        lambda_init=lambda_init, name="attn_b")

        cq_lat = _prep([(proj, 512, lambda hh: 11), (proj, 512, lambda hh: 12)], n_heads=1, out_kind="rows",
                       tm=tp, gain=c_q_a_norm[l], norm="full", n_norm=C_Q_LORA, name="norm_c_q")[0]
        ckv_lat = _prep([(proj, 512, lambda hh: 13)], n_heads=1, out_kind="rows", tm=tp,
                        gain=c_kv_a_norm[l], norm="full", n_norm=C_KV_LORA, name="norm_c_kv")[0]
        c_q_raw = _matmul(cq_lat, [wq_up], out_dtype=jnp.float32, tm=1024, tn=512, tk=C_Q_LORA, name="c_q_up")
        c_kv_raw = _matmul(ckv_lat, [wkv_up], out_dtype=jnp.float32, tm=1024, tn=512, tk=C_KV_LORA, name="c_kv_up")
        c_scale = LOG2E / math.sqrt(C_QK_DIM)
        c_q = _prep([(c_q_raw, C_QK_PAD, lambda hh: hh)], n_heads=C_HEADS, out_kind="cols", tm=tp,
                    gain=_pad_vec(c_q_norm[l], C_QK_PAD), norm="full", n_norm=C_QK_DIM, tabs=c_tabs,
                    rope_blocks=(1,), shifts=(96, 32), scale=c_scale, name="prep_c_q")
        c_k = _prep([(c_kv_raw, LANES, lambda hh: 2 * hh), (proj, LANES, lambda hh: _C_KR)], n_heads=C_HEADS,
                    out_kind="rows", tm=tp, gain=_pad_vec(c_k_norm[l], C_QK_PAD), norm="full",
                    n_norm=C_QK_DIM, tabs=c_tabs, rope_blocks=(1,), shifts=(96, 32), name="prep_c_k")
        c_v = _prep([(c_kv_raw, LANES, lambda hh: 2 * hh + 1)], n_heads=C_HEADS, out_kind="chunks", tm=tp,
                    name="prep_c_v")
        yc = _attention(c_q, c_k, c_v, tq=512, name="attn_c")

        mix = jnp.concatenate([ya, yb, yc], axis=-1)
        h = _matmul(mix, [w_out_l], res=h, out_dtype=jnp.float32, tm=1024, tn=512, tk=d, name="out_proj")

        hn = _rmsnorm(h, ffn_norm[l])
        act = _matmul(hn, [w_gate_l, w_up_l], out_dtype=bf, tm=1024, tn=512, tk=d, name="ffn_gate_up")
        h = _matmul(act, [w_down_l], res=h, out_dtype=jnp.float32, tm=1024, tn=512, tk=FFN_PAD // 4,
                    name="ffn_down")

    return h.reshape(bsz, s, d)
```

```python
import functools
import math

import jax
import jax.numpy as jnp
import numpy as np
from jax import lax
from jax.experimental import pallas as pl
from jax.experimental.pallas import tpu as pltpu

D_MODEL = 4096
SEQ = 8192
DEPTH = 2
GRID_W = 64
EPS = 1e-6

A_HEAD_DIM = 128
A_HEADS = 12
A_KV_HEADS = 4
A_ROPE_THETA = 10000.0

B_QK_DIM = 64
B_V_DIM = 128
B_HEADS = 8
B_ROPE_DIM = 16
B_ROPE_THETA = 500000.0

C_V_DIM = 128
C_HEADS = 12
C_Q_LORA = 1024
C_KV_LORA = 512
C_NOPE = 128
C_ROPE = 64
C_QK_DIM = C_NOPE + C_ROPE
C_QK_PAD = 256
C_ROPE_THETA = 10000.0

N_IN = 7232
FFN_HIDDEN = 11008

LANES = 128
SUBLANES = 8
ATTN_ROW_BLOCK = 64
ATTN_CHUNKS_PER_TRIP = 8
ONES_ROWS = 16
VMEM_LIMIT_BYTES = 56 * 1024 * 1024

N_IN_PAD = 7680
FFN_PAD = 11264

LOG2E = math.log2(math.e)

_A_Q0, _A_K0, _A_V0 = 0, 12, 16
_B_Q0, _B_K0, _B_V0 = 20, 28, 36
_C_KR = 56


def _cparams(sem):
    return pltpu.CompilerParams(dimension_semantics=sem, vmem_limit_bytes=VMEM_LIMIT_BYTES)


def _rmsnorm_body(x_ref, g_ref, o_ref):
    x = x_ref[...]
    ms = jnp.mean(x * x, axis=-1, keepdims=True)
    o_ref[...] = (x * lax.rsqrt(ms + EPS) * g_ref[...]).astype(o_ref.dtype)


def _rmsnorm(x, g, tm=256):
    s, d = x.shape
    return pl.pallas_call(
        _rmsnorm_body,
        grid=(s // tm,),
        in_specs=[pl.BlockSpec((tm, d), lambda i: (i, 0)),
                  pl.BlockSpec((1, d), lambda i: (0, 0))],
        out_specs=pl.BlockSpec((tm, d), lambda i: (i, 0)),
        out_shape=jax.ShapeDtypeStruct((s, d), jnp.bfloat16),
        compiler_params=_cparams(("parallel",)),
        name="rmsnorm",
    )(x, g.reshape(1, d))


def _matmul_body(*refs, n_w, has_res, nk):
    x_ref = refs[0]
    w_refs = refs[1:1 + n_w]
    pos = 1 + n_w
    res_ref = refs[pos] if has_res else None
    pos += int(has_res)
    o_ref = refs[pos]
    acc_refs = refs[pos + 1:]

    def epilogue(accs):
        if n_w == 2:
            g, u = accs
            y = g * jax.nn.sigmoid(g) * u
        else:
            y = accs[0]
        if has_res:
            y = y + res_ref[...]
        o_ref[...] = y.astype(o_ref.dtype)

    x = x_ref[...]
    prods = [jnp.dot(x, w[...], preferred_element_type=jnp.float32) for w in w_refs]
    if nk == 1:
        epilogue(prods)
        return

    k = pl.program_id(2)

    @pl.when(k == 0)
    def _():
        for a, p in zip(acc_refs, prods):
            a[...] = p

    @pl.when(k > 0)
    def _():
        for a, p in zip(acc_refs, prods):
            a[...] += p

    @pl.when(k == nk - 1)
    def _():
        epilogue([a[...] for a in acc_refs])


def _matmul(x, ws, res=None, *, out_dtype, tm, tn, tk, name):
    m, kdim = x.shape
    n = ws[0].shape[1]
    nk = kdim // tk
    assert m % tm == 0 and n % tn == 0 and kdim % tk == 0
    n_w = len(ws)
    in_specs = [pl.BlockSpec((tm, tk), lambda i, j, k: (i, k))]
    in_specs += [pl.BlockSpec((tk, tn), lambda i, j, k: (k, j)) for _ in ws]
    args = [x, *ws]
    if res is not None:
        in_specs.append(pl.BlockSpec((tm, tn), lambda i, j, k: (i, j)))
        args.append(res)
    scratch = [pltpu.VMEM((tm, tn), jnp.float32) for _ in ws] if nk > 1 else []
    return pl.pallas_call(
        functools.partial(_matmul_body, n_w=n_w, has_res=res is not None, nk=nk),
        grid=(m // tm, n // tn, nk),
        in_specs=in_specs,
        out_specs=pl.BlockSpec((tm, tn), lambda i, j, k: (i, j)),
        out_shape=jax.ShapeDtypeStruct((m, n), out_dtype),
        scratch_shapes=scratch,
        compiler_params=_cparams(("parallel", "parallel", "arbitrary")),
        name=name,
    )(*args)


def _prep_body(*refs, nx, norm, n_norm, rope_blocks, shifts, scale, transpose, ones_rows):
    x_refs = refs[:nx]
    pos = nx
    g_ref = None
    if norm is not None:
        g_ref = refs[pos]
        pos += 1
    tabs = None
    if rope_blocks:
        tabs = refs[pos:pos + 3]
        pos += 3
    o_ref = refs[pos]

    xs = [r[...] for r in x_refs]
    x = xs[0] if nx == 1 else jnp.concatenate(xs, axis=-1)
    width = x.shape[-1]

    if norm == "full":
        ss = jnp.sum(x * x, axis=-1, keepdims=True)
        x = x * lax.rsqrt(ss * (1.0 / n_norm) + EPS) * g_ref[...]
    elif norm == "half":
        lane = lax.broadcasted_iota(jnp.int32, x.shape, 1)
        lo = lane < (LANES // 2)
        x2 = x * x
        ss_lo = jnp.sum(jnp.where(lo, x2, 0.0), axis=-1, keepdims=True)
        ss_hi = jnp.sum(jnp.where(lo, 0.0, x2), axis=-1, keepdims=True)
        ss = jnp.where(lo, ss_lo, ss_hi)
        x = x * lax.rsqrt(ss * (1.0 / n_norm) + EPS) * g_ref[...]

    if rope_blocks:
        c_ref, s1_ref, s2_ref = tabs
        pieces = []
        for b in range(width // LANES):
            xb = x[:, b * LANES:(b + 1) * LANES]
            if b in rope_blocks:
                sl = slice(b * LANES, (b + 1) * LANES)
                xb = (xb * c_ref[:, sl]
                      + pltpu.roll(xb, shifts[0], 1) * s1_ref[:, sl]
                      + pltpu.roll(xb, shifts[1], 1) * s2_ref[:, sl])
            pieces.append(xb)
        x = pieces[0] if len(pieces) == 1 else jnp.concatenate(pieces, axis=-1)

    if scale != 1.0:
        x = x * scale
    if transpose:
        x = x.T
    if ones_rows:
        x = jnp.concatenate([x, jnp.ones((ones_rows, x.shape[1]), x.dtype)], axis=0)
    o_ref[...] = x.astype(o_ref.dtype).reshape(o_ref.shape)


def _prep(xs, *, n_heads, out_kind, tm, gain=None, norm=None, n_norm=None, tabs=None,
          rope_blocks=(), shifts=(0, 0), scale=1.0, name="prep"):
    s = xs[0][0].shape[0]
    width = sum(bw for _, bw, _ in xs)
    in_specs, args = [], []
    for arr, bw, fn in xs:
        in_specs.append(pl.BlockSpec((tm, bw), functools.partial(lambda i, h, fn: (i, fn(h)), fn=fn)))
        args.append(arr)
    if norm is not None:
        in_specs.append(pl.BlockSpec((1, width), lambda i, h: (0, 0)))
        args.append(gain.reshape(1, width))
    if rope_blocks:
        for t in tabs:
            in_specs.append(pl.BlockSpec((tm, width), lambda i, h: (i, 0)))
            args.append(t)
    if out_kind == "rows":
        out_shape = (n_heads, s, width)
        out_spec = pl.BlockSpec((1, tm, width), lambda i, h: (h, i, 0))
    elif out_kind == "cols":
        out_shape = (n_heads, width, s)
        out_spec = pl.BlockSpec((1, width, tm), lambda i, h: (h, 0, i))
    else:
        out_shape = (n_heads, s // tm, width + ONES_ROWS, tm)
        out_spec = pl.BlockSpec((1, 1, width + ONES_ROWS, tm), lambda i, h: (h, i, 0, 0))
    return pl.pallas_call(
        functools.partial(_prep_body, nx=len(xs), norm=norm, n_norm=n_norm,
                          rope_blocks=tuple(rope_blocks), shifts=shifts, scale=scale,
                          transpose=out_kind != "rows",
                          ones_rows=ONES_ROWS if out_kind == "chunks" else 0),
        grid=(s // tm, n_heads),
        in_specs=in_specs,
        out_specs=out_spec,
        out_shape=jax.ShapeDtypeStruct(out_shape, jnp.bfloat16),
        compiler_params=_cparams(("parallel", "arbitrary")),
        name=name,
    )(*args)


def _attn_body(*refs, nchunk, tk, diff, lambda_init):
    if diff:
        qT_ref, k_ref, vT_ref, lam_ref, sub_ref, o_ref, m_ref, acc_ref, s_ref, p_ref = refs
    else:
        qT_ref, k_ref, vT_ref, o_ref, m_ref, acc_ref, s_ref, p_ref = refs

    qT = qT_ref[0]
    tq = qT.shape[1]
    if diff:
        row = lax.broadcasted_iota(jnp.int32, qT.shape, 0)
        first = row < B_QK_DIM
        zero = jnp.zeros_like(qT)
        qT = jnp.concatenate([jnp.where(first, qT, zero), jnp.where(first, zero, qT)], axis=1)

    m_ref[...] = jnp.full(m_ref.shape, -jnp.inf, jnp.float32)
    acc_ref[...] = jnp.zeros(acc_ref.shape, jnp.float32)

    def scores(j):
        start = pl.multiple_of(j * tk, tk)
        return jnp.dot(k_ref[0, pl.ds(start, tk), :], qT,
                       preferred_element_type=jnp.float32)

    nq = qT.shape[1]
    rb = ATTN_ROW_BLOCK
    groups = rb // SUBLANES

    def softmax(slot):
        def block(r):
            return s_ref[slot, r * rb:(r + 1) * rb, :].reshape(groups, SUBLANES, nq)

        mx = jnp.max(block(0), axis=0)
        for r in range(1, tk // rb):
            mx = jnp.maximum(mx, jnp.max(block(r), axis=0))
        m_old = m_ref[...]
        m_new = jnp.maximum(m_old, jnp.max(mx, axis=0, keepdims=True))
        alpha = jnp.exp2(m_old - m_new)
        m_rows = jnp.broadcast_to(m_new, (SUBLANES, nq))[None]
        for r in range(tk // rb):
            p = jnp.exp2(block(r) - m_rows).reshape(rb, nq)
            p_ref[slot, r * rb:(r + 1) * rb, :] = p.astype(jnp.bfloat16)
        m_ref[...] = m_new
        return alpha

    def pv_update(j, slot, alpha):
        pv = jnp.dot(vT_ref[0, j], p_ref[slot], preferred_element_type=jnp.float32)
        acc_ref[...] = acc_ref[...] * alpha + pv

    s_ref[0] = scores(0)

    def trip(t, carry):
        for u in range(ATTN_CHUNKS_PER_TRIP):
            j = t * ATTN_CHUNKS_PER_TRIP + u
            s_ref[(u + 1) % 2] = scores(jnp.minimum(j + 1, nchunk - 1))
            alpha = softmax(u % 2)
            pv_update(j, u % 2, alpha)
        return carry

    lax.fori_loop(0, nchunk // ATTN_CHUNKS_PER_TRIP, trip, 0)

    dv = o_ref.shape[1]
    o = acc_ref[:dv, :] / acc_ref[dv:dv + 1, :]
    if diff:
        lp = lam_ref[...]
        lam = (jnp.exp(jnp.sum(lp[0:1] * lp[1:2], axis=-1, keepdims=True))
               - jnp.exp(jnp.sum(lp[2:3] * lp[3:4], axis=-1, keepdims=True))
               + lambda_init)
        o = o[:, :tq] - lam * o[:, tq:]
        ms = jnp.mean(o * o, axis=0, keepdims=True)
        o = o * lax.rsqrt(ms + EPS) * sub_ref[...] * (1.0 - lambda_init)
    o_ref[...] = o.T.astype(o_ref.dtype)


def _attention(qT, k, vT, *, tq, diff=False, lam_p=None, subln=None, lambda_init=0.0, name):
    n_heads, dq, s = qT.shape
    n_kv = k.shape[0]
    group = n_heads // n_kv
    _, nchunk, dv_ext, tk = vT.shape
    dv = dv_ext - ONES_ROWS
    nq = 2 * tq if diff else tq
    in_specs = [pl.BlockSpec((1, dq, tq), lambda h, i: (h, 0, i)),
                pl.BlockSpec((1, s, dq), lambda h, i: (h // group, 0, 0)),
                pl.BlockSpec((1, nchunk, dv_ext, tk), lambda h, i: (h // group, 0, 0, 0))]
    args = [qT, k, vT]
    if diff:
        in_specs += [pl.BlockSpec(lam_p.shape, lambda h, i: (0, 0)),
                     pl.BlockSpec((dv, 1), lambda h, i: (0, 0))]
        args += [lam_p, subln.reshape(dv, 1)]
    return pl.pallas_call(
        functools.partial(_attn_body, nchunk=nchunk, tk=tk, diff=diff, lambda_init=lambda_init),
        grid=(n_heads, s // tq),
        in_specs=in_specs,
        out_specs=pl.BlockSpec((tq, dv), lambda h, i: (i, h)),
        out_shape=jax.ShapeDtypeStruct((s, n_heads * dv), jnp.bfloat16),
        scratch_shapes=[pltpu.VMEM((1, nq), jnp.float32),
                        pltpu.VMEM((dv_ext, nq), jnp.float32),
                        pltpu.VMEM((2, tk, nq), jnp.float32),
                        pltpu.VMEM((2, tk, nq), jnp.bfloat16)],
        compiler_params=_cparams(("parallel", "arbitrary")),
        name=name,
    )(*args)


def _rope_cos_sin(pos, dim, theta):
    inv = theta ** (-jnp.arange(0, dim, 2, dtype=jnp.float32) / dim)
    ang = pos.astype(jnp.float32)[:, None] * inv[None, :]
    return jnp.cos(ang), jnp.sin(ang)


def _rotate_half_tables(cos, sin):
    zero = jnp.zeros_like(sin)
    return (jnp.concatenate([cos, cos], axis=-1),
            jnp.concatenate([-sin, zero], axis=-1),
            jnp.concatenate([zero, sin], axis=-1))


def _rope_tables(s):
    t = jnp.arange(s, dtype=jnp.int32)
    row = t // GRID_W
    col = t % GRID_W
    half = A_HEAD_DIM // 2
    a_row = _rotate_half_tables(*_rope_cos_sin(row, half, A_ROPE_THETA))
    a_col = _rotate_half_tables(*_rope_cos_sin(col, half, A_ROPE_THETA))
    a_tabs = tuple(jnp.concatenate([r, c], axis=-1) for r, c in zip(a_row, a_col))

    b_rot = _rotate_half_tables(*_rope_cos_sin(t, B_ROPE_DIM, B_ROPE_THETA))
    rest = B_QK_DIM - B_ROPE_DIM
    fill = (jnp.ones((s, rest), jnp.float32), jnp.zeros((s, rest), jnp.float32),
            jnp.zeros((s, rest), jnp.float32))
    b_tabs = tuple(jnp.tile(jnp.concatenate([r, f], axis=-1), (1, 2)) for r, f in zip(b_rot, fill))

    c_rot = _rotate_half_tables(*_rope_cos_sin(t, C_ROPE, C_ROPE_THETA))
    ones = jnp.ones((s, C_NOPE), jnp.float32)
    zeros = jnp.zeros((s, C_NOPE), jnp.float32)
    pad = jnp.zeros((s, C_QK_PAD - C_QK_DIM), jnp.float32)
    c_tabs = (jnp.concatenate([ones, c_rot[0], pad], axis=-1),
              jnp.concatenate([zeros, c_rot[1], pad], axis=-1),
              jnp.concatenate([zeros, c_rot[2], pad], axis=-1))
    return a_tabs, b_tabs, c_tabs


def _pad_cols(w, n):
    return jnp.pad(w, ((0, 0), (0, n - w.shape[1])))


def _pad_vec(g, n):
    return jnp.pad(g, (0, n - g.shape[0]))


def kernel(x, attn_norm, w_in, a_q_norm, a_k_norm, b_q_norm, b_k_norm, b_lambda, b_subln,
           c_q_a_norm, c_kv_a_norm, c_w_q_up, c_w_kv_up, c_q_norm, c_k_norm, w_out,
           ffn_norm, w_gate, w_up, w_down):
    bsz, s, d = x.shape
    assert (bsz, s, d) == (1, SEQ, D_MODEL)
    bf = jnp.bfloat16
    a_tabs, b_tabs, c_tabs = _rope_tables(s)
    tp = 512
    h = x.reshape(s, d)

    for l in range(DEPTH):
        lambda_init = 0.8 - 0.6 * math.exp(-0.3 * l)
        w_in_l = _pad_cols(w_in[l].astype(bf), N_IN_PAD)
        wq_up = c_w_q_up[l].astype(bf).reshape(C_Q_LORA, C_HEADS, C_QK_DIM)
        wq_up = jnp.pad(wq_up, ((0, 0), (0, 0), (0, C_QK_PAD - C_QK_DIM))).reshape(C_Q_LORA, C_HEADS * C_QK_PAD)
        wkv_up = c_w_kv_up[l].astype(bf)
        w_out_l = w_out[l].astype(bf)
        w_gate_l = _pad_cols(w_gate[l].astype(bf), FFN_PAD)
        w_up_l = _pad_cols(w_up[l].astype(bf), FFN_PAD)
        w_down_l = jnp.pad(w_down[l].astype(bf), ((0, FFN_PAD - FFN_HIDDEN), (0, 0)))

        xn = _rmsnorm(h, attn_norm[l])
        proj = _matmul(xn, [w_in_l], out_dtype=jnp.float32, tm=1024, tn=512, tk=d, name="in_proj")

        a_scale = LOG2E / math.sqrt(A_HEAD_DIM)
        a_q = _prep([(proj, LANES, lambda hh: _A_Q0 + hh)], n_heads=A_HEADS, out_kind="cols", tm=tp,
                    gain=a_q_norm[l], norm="full", n_norm=A_HEAD_DIM, tabs=a_tabs, rope_blocks=(0,),
                    shifts=(96, 32), scale=a_scale, name="prep_a_q")
        a_k = _prep([(proj, LANES, lambda hh: _A_K0 + hh)], n_heads=A_KV_HEADS, out_kind="rows", tm=tp,
                    gain=a_k_norm[l], norm="full", n_norm=A_HEAD_DIM, tabs=a_tabs, rope_blocks=(0,),
                    shifts=(96, 32), name="prep_a_k")
        a_v = _prep([(proj, LANES, lambda hh: _A_V0 + hh)], n_heads=A_KV_HEADS, out_kind="chunks", tm=tp,
                    name="prep_a_v")
        ya = _attention(a_q, a_k, a_v, tq=512, name="attn_a")

        b_scale = LOG2E / math.sqrt(B_QK_DIM)
        b_q = _prep([(proj, LANES, lambda hh: _B_Q0 + hh)], n_heads=B_HEADS, out_kind="cols", tm=tp,
                    gain=jnp.tile(b_q_norm[l], 2), norm="half", n_norm=B_QK_DIM, tabs=b_tabs,
                    rope_blocks=(0,), shifts=(120, 8), scale=b_scale, name="prep_b_q")
        b_k = _prep([(proj, LANES, lambda hh: _B_K0 + hh)], n_heads=B_HEADS, out_kind="rows", tm=tp,
                    gain=jnp.tile(b_k_norm[l], 2), norm="half", n_norm=B_QK_DIM, tabs=b_tabs,
                    rope_blocks=(0,), shifts=(120, 8), name="prep_b_k")
        b_v = _prep([(proj, LANES, lambda hh: _B_V0 + hh)], n_heads=B_HEADS, out_kind="chunks", tm=tp,
                    name="prep_b_v")
        yb = _attention(b_q, b_k, b_v, tq=256, diff=True, lam_p=b_lambda[l], subln=b_subln[l],
                        lambda_init=lambda_init, name="attn_b")

        cq_lat = _prep([(proj, 512, lambda hh: 11), (proj, 512, lambda hh: 12)], n_heads=1, out_kind="rows",
                       tm=tp, gain=c_q_a_norm[l], norm="full", n_norm=C_Q_LORA, name="norm_c_q")[0]
        ckv_lat = _prep([(proj, 512, lambda hh: 13)], n_heads=1, out_kind="rows", tm=tp,
                        gain=c_kv_a_norm[l], norm="full", n_norm=C_KV_LORA, name="norm_c_kv")[0]
        c_q_raw = _matmul(cq_lat, [wq_up], out_dtype=jnp.float32, tm=1024, tn=512, tk=C_Q_LORA, name="c_q_up")
        c_kv_raw = _matmul(ckv_lat, [wkv_up], out_dtype=jnp.float32, tm=1024, tn=512, tk=C_KV_LORA, name="c_kv_up")
        c_scale = LOG2E / math.sqrt(C_QK_DIM)
        c_q = _prep([(c_q_raw, C_QK_PAD, lambda hh: hh)], n_heads=C_HEADS, out_kind="cols", tm=tp,
                    gain=_pad_vec(c_q_norm[l], C_QK_PAD), norm="full", n_norm=C_QK_DIM, tabs=c_tabs,
                    rope_blocks=(1,), shifts=(96, 32), scale=c_scale, name="prep_c_q")
        c_k = _prep([(c_kv_raw, LANES, lambda hh: 2 * hh), (proj, LANES, lambda hh: _C_KR)], n_heads=C_HEADS,
                    out_kind="rows", tm=tp, gain=_pad_vec(c_k_norm[l], C_QK_PAD), norm="full",
                    n_norm=C_QK_DIM, tabs=c_tabs, rope_blocks=(1,), shifts=(96, 32), name="prep_c_k")
        c_v = _prep([(c_kv_raw, LANES, lambda hh: 2 * hh + 1)], n_heads=C_HEADS, out_kind="chunks", tm=tp,
                    name="prep_c_v")
        yc = _attention(c_q, c_k, c_v, tq=512, name="attn_c")

        mix = jnp.concatenate([ya, yb, yc], axis=-1)
        h = _matmul(mix, [w_out_l], res=h, out_dtype=jnp.float32, tm=1024, tn=512, tk=d, name="out_proj")

        hn = _rmsnorm(h, ffn_norm[l])
        act = _matmul(hn, [w_gate_l, w_up_l], out_dtype=bf, tm=1024, tn=512, tk=d, name="ffn_gate_up")
        h = _matmul(act, [w_down_l], res=h, out_dtype=jnp.float32, tm=1024, tn=512, tk=FFN_PAD // 4,
                    name="ffn_down")

    return h.reshape(bsz, s, d)
```

```python
import functools
import math

import jax
import jax.numpy as jnp
import numpy as np
from jax import lax
from jax.experimental import pallas as pl
from jax.experimental.pallas import tpu as pltpu

D_MODEL = 4096
SEQ = 8192
DEPTH = 2
GRID_W = 64
EPS = 1e-6

A_HEAD_DIM = 128
A_HEADS = 12
A_KV_HEADS = 4
A_ROPE_THETA = 10000.0

B_QK_DIM = 64
B_V_DIM = 128
B_HEADS = 8
B_ROPE_DIM = 16
B_ROPE_THETA = 500000.0

C_V_DIM = 128
C_HEADS = 12
C_Q_LORA = 1024
C_KV_LORA = 512
C_NOPE = 128
C_ROPE = 64
C_QK_DIM = C_NOPE + C_ROPE
C_QK_PAD = 256
C_ROPE_THETA = 10000.0

N_IN = 7232
FFN_HIDDEN = 11008

LANES = 128
SUBLANES = 8
ATTN_ROW_BLOCK = 64
ATTN_CHUNKS_PER_TRIP = 8
ONES_ROWS = 16
VMEM_LIMIT_BYTES = 56 * 1024 * 1024

N_IN_PAD = 7680

LOG2E = math.log2(math.e)

PROJ_BLOCK = 512
_C_KR_BLOCK = 56


def _head_pieces(n_heads, lanes_per_head=LANES, lane0=0, blocks=1):
    out = []
    for h in range(n_heads):
        head = []
        for b in range(blocks):
            col = lane0 + h * lanes_per_head + b * LANES
            head.append((col // PROJ_BLOCK, col % PROJ_BLOCK))
        out.append(head)
    return out


def _cparams(sem):
    return pltpu.CompilerParams(dimension_semantics=sem, vmem_limit_bytes=VMEM_LIMIT_BYTES)


def _rmsnorm_body(x_ref, g_ref, o_ref):
    x = x_ref[...]
    ms = jnp.mean(x * x, axis=-1, keepdims=True)
    o_ref[...] = (x * lax.rsqrt(ms + EPS) * g_ref[...]).astype(o_ref.dtype)


def _rmsnorm(x, g, tm=256):
    s, d = x.shape
    return pl.pallas_call(
        _rmsnorm_body,
        grid=(s // tm,),
        in_specs=[pl.BlockSpec((tm, d), lambda i: (i, 0)),
                  pl.BlockSpec((1, d), lambda i: (0, 0))],
        out_specs=pl.BlockSpec((tm, d), lambda i: (i, 0)),
        out_shape=jax.ShapeDtypeStruct((s, d), jnp.bfloat16),
        compiler_params=_cparams(("parallel",)),
        name="rmsnorm",
    )(x, g.reshape(1, d))


def _matmul_body(*refs, n_w, has_res, nk):
    x_ref = refs[0]
    w_refs = refs[1:1 + n_w]
    pos = 1 + n_w
    res_ref = refs[pos] if has_res else None
    pos += int(has_res)
    o_ref = refs[pos]
    acc_refs = refs[pos + 1:]

    def epilogue(accs):
        if n_w == 2:
            g, u = accs
            y = g * jax.nn.sigmoid(g) * u
        else:
            y = accs[0]
        if has_res:
            y = y + res_ref[...]
        o_ref[...] = y.astype(o_ref.dtype)

    x = x_ref[...]
    tiles = [w[...] if w.dtype == x.dtype else w[...].astype(x.dtype) for w in w_refs]
    prods = [jnp.dot(x, w, preferred_element_type=jnp.float32) for w in tiles]
    if nk == 1:
        epilogue(prods)
        return

    k = pl.program_id(2)

    @pl.when(k == 0)
    def _():
        for a, p in zip(acc_refs, prods):
            a[...] = p

    @pl.when(k > 0)
    def _():
        for a, p in zip(acc_refs, prods):
            a[...] += p

    @pl.when(k == nk - 1)
    def _():
        epilogue([a[...] for a in acc_refs])


def _matmul(x, ws, res=None, *, out_dtype, tm, tn, tk, name, layer=None):
    m, kdim = x.shape
    n = ws[0].shape[-1]
    nk = kdim // tk
    assert m % tm == 0 and n % tn == 0 and kdim % tk == 0
    n_w = len(ws)
    in_specs = [pl.BlockSpec((tm, tk), lambda i, j, k: (i, k))]
    if layer is None:
        in_specs += [pl.BlockSpec((tk, tn), lambda i, j, k: (k, j)) for _ in ws]
    else:
        in_specs += [pl.BlockSpec((None, tk, tn), lambda i, j, k: (layer, k, j)) for _ in ws]
    args = [x, *ws]
    if res is not None:
        in_specs.append(pl.BlockSpec((tm, tn), lambda i, j, k: (i, j)))
        args.append(res)
    scratch = [pltpu.VMEM((tm, tn), jnp.float32) for _ in ws] if nk > 1 else []
    return pl.pallas_call(
        functools.partial(_matmul_body, n_w=n_w, has_res=res is not None, nk=nk),
        grid=(m // tm, n // tn, nk),
        in_specs=in_specs,
        out_specs=pl.BlockSpec((tm, tn), lambda i, j, k: (i, j)),
        out_shape=jax.ShapeDtypeStruct((m, n), out_dtype),
        scratch_shapes=scratch,
        compiler_params=_cparams(("parallel", "parallel", "arbitrary")),
        name=name,
    )(*args)


def _prep_head(x, g_ref, tabs, *, norm, n_norm, rope_blocks, shifts, scale, transpose, ones_rows):
    width = x.shape[-1]
    if norm == "full":
        ss = jnp.sum(x * x, axis=-1, keepdims=True)
        x = x * lax.rsqrt(ss * (1.0 / n_norm) + EPS) * g_ref[...]
    elif norm == "half":
        lane = lax.broadcasted_iota(jnp.int32, x.shape, 1)
        lo = lane < (LANES // 2)
        x2 = x * x
        ss_lo = jnp.sum(jnp.where(lo, x2, 0.0), axis=-1, keepdims=True)
        ss_hi = jnp.sum(jnp.where(lo, 0.0, x2), axis=-1, keepdims=True)
        ss = jnp.where(lo, ss_lo, ss_hi)
        x = x * lax.rsqrt(ss * (1.0 / n_norm) + EPS) * g_ref[...]

    if rope_blocks:
        c_ref, s1_ref, s2_ref = tabs
        blocks = []
        for b in range(width // LANES):
            xb = x[:, b * LANES:(b + 1) * LANES]
            if b in rope_blocks:
                sl = slice(b * LANES, (b + 1) * LANES)
                xb = (xb * c_ref[:, sl]
                      + pltpu.roll(xb, shifts[0], 1) * s1_ref[:, sl]
                      + pltpu.roll(xb, shifts[1], 1) * s2_ref[:, sl])
            blocks.append(xb)
        x = blocks[0] if len(blocks) == 1 else jnp.concatenate(blocks, axis=-1)

    if scale != 1.0:
        x = x * scale
    if transpose:
        x = x.T
    if ones_rows:
        x = jnp.concatenate([x, jnp.ones((ones_rows, x.shape[1]), x.dtype)], axis=0)
    return x


def _prep_body(*refs, nx, pieces, has_gain, has_tabs, **head_kw):
    x_refs = refs[:nx]
    pos = nx
    g_ref = None
    if has_gain:
        g_ref = refs[pos]
        pos += 1
    tabs = None
    if has_tabs:
        tabs = refs[pos:pos + 3]
        pos += 3
    o_ref = refs[pos]
    for h, head_pieces in enumerate(pieces):
        cols = [x_refs[src][:, off:off + LANES] for src, off in head_pieces]
        x = cols[0] if len(cols) == 1 else jnp.concatenate(cols, axis=-1)
        y = _prep_head(x, g_ref, tabs, **head_kw)
        o_ref[h] = y.astype(o_ref.dtype).reshape(o_ref.shape[1:])


def _prep(srcs, pieces, *, out_kind, tm, gain=None, norm=None, n_norm=None, tabs=None,
          rope_blocks=(), shifts=(0, 0), scale=1.0, name="prep"):
    s = srcs[0][0].shape[0]
    n_heads = len(pieces)
    width = LANES * len(pieces[0])
    in_specs, args = [], []
    for arr, bw, idx in srcs:
        in_specs.append(pl.BlockSpec((tm, bw), functools.partial(lambda i, idx: (i, idx), idx=idx)))
        args.append(arr)
    if norm is not None:
        in_specs.append(pl.BlockSpec((1, width), lambda i: (0, 0)))
        args.append(gain.reshape(1, width))
    if rope_blocks:
        for t in tabs:
            in_specs.append(pl.BlockSpec((tm, width), lambda i: (i, 0)))
            args.append(t)
    if out_kind == "rows":
        out_shape = (n_heads, s, width)
        out_spec = pl.BlockSpec((n_heads, tm, width), lambda i: (0, i, 0))
    elif out_kind == "cols":
        out_shape = (n_heads, width, s)
        out_spec = pl.BlockSpec((n_heads, width, tm), lambda i: (0, 0, i))
    else:
        out_shape = (n_heads, s // tm, width + ONES_ROWS, tm)
        out_spec = pl.BlockSpec((n_heads, 1, width + ONES_ROWS, tm), lambda i: (0, i, 0, 0))
    return pl.pallas_call(
        functools.partial(_prep_body, nx=len(srcs), pieces=tuple(tuple(p) for p in pieces),
                          has_gain=norm is not None, has_tabs=bool(rope_blocks),
                          norm=norm, n_norm=n_norm, rope_blocks=tuple(rope_blocks), shifts=shifts,
                          scale=scale, transpose=out_kind != "rows",
                          ones_rows=ONES_ROWS if out_kind == "chunks" else 0),
        grid=(s // tm,),
        in_specs=in_specs,
        out_specs=out_spec,
        out_shape=jax.ShapeDtypeStruct(out_shape, jnp.bfloat16),
        compiler_params=_cparams(("parallel",)),
        name=name,
    )(*args)


def _attn_body(*refs, nchunk, tk, diff, lambda_init):
    if diff:
        qT_ref, k_ref, vT_ref, lam_ref, sub_ref, o_ref, m_ref, acc_ref, s_ref, p_ref = refs
    else:
        qT_ref, k_ref, vT_ref, o_ref, m_ref, acc_ref, s_ref, p_ref = refs

    qT = qT_ref[0]
    tq = qT.shape[1]
    if diff:
        row = lax.broadcasted_iota(jnp.int32, qT.shape, 0)
        first = row < B_QK_DIM
        zero = jnp.zeros_like(qT)
        qT = jnp.concatenate([jnp.where(first, qT, zero), jnp.where(first, zero, qT)], axis=1)

    m_ref[...] = jnp.full(m_ref.shape, -jnp.inf, jnp.float32)
    acc_ref[...] = jnp.zeros(acc_ref.shape, jnp.float32)

    def scores(j):
        start = pl.multiple_of(j * tk, tk)
        return jnp.dot(k_ref[0, pl.ds(start, tk), :], qT,
                       preferred_element_type=jnp.float32)

    nq = qT.shape[1]
    rb = ATTN_ROW_BLOCK
    groups = rb // SUBLANES

    def softmax(slot):
        def block(r):
            return s_ref[slot, r * rb:(r + 1) * rb, :].reshape(groups, SUBLANES, nq)

        mx = jnp.max(block(0), axis=0)
        for r in range(1, tk // rb):
            mx = jnp.maximum(mx, jnp.max(block(r), axis=0))
        m_old = m_ref[...]
        m_new = jnp.maximum(m_old, jnp.max(mx, axis=0, keepdims=True))
        alpha = jnp.exp2(m_old - m_new)
        m_rows = jnp.broadcast_to(m_new, (SUBLANES, nq))[None]
        for r in range(tk // rb):
            p = jnp.exp2(block(r) - m_rows).reshape(rb, nq)
            p_ref[slot, r * rb:(r + 1) * rb, :] = p.astype(jnp.bfloat16)
        m_ref[...] = m_new
        return alpha

    def pv_update(j, slot, alpha):
        pv = jnp.dot(vT_ref[0, j], p_ref[slot], preferred_element_type=jnp.float32)
        acc_ref[...] = acc_ref[...] * alpha + pv

    s_ref[0] = scores(0)

    def trip(t, carry):
        for u in range(ATTN_CHUNKS_PER_TRIP):
            j = t * ATTN_CHUNKS_PER_TRIP + u
            s_ref[(u + 1) % 2] = scores(jnp.minimum(j + 1, nchunk - 1))
            alpha = softmax(u % 2)
            pv_update(j, u % 2, alpha)
        return carry

    lax.fori_loop(0, nchunk // ATTN_CHUNKS_PER_TRIP, trip, 0)

    dv = o_ref.shape[1]
    o = acc_ref[:dv, :] / acc_ref[dv:dv + 1, :]
    if diff:
        lp = lam_ref[...]
        lam = (jnp.exp(jnp.sum(lp[0:1] * lp[1:2], axis=-1, keepdims=True))
               - jnp.exp(jnp.sum(lp[2:3] * lp[3:4], axis=-1, keepdims=True))
               + lambda_init)
        o = o[:, :tq] - lam * o[:, tq:]
        ms = jnp.mean(o * o, axis=0, keepdims=True)
        o = o * lax.rsqrt(ms + EPS) * sub_ref[...] * (1.0 - lambda_init)
    o_ref[...] = o.T.astype(o_ref.dtype)


def _attention(qT, k, vT, *, tq, diff=False, lam_p=None, subln=None, lambda_init=0.0, name):
    n_heads, dq, s = qT.shape
    n_kv = k.shape[0]
    group = n_heads // n_kv
    _, nchunk, dv_ext, tk = vT.shape
    dv = dv_ext - ONES_ROWS
    nq = 2 * tq if diff else tq
    in_specs = [pl.BlockSpec((1, dq, tq), lambda h, i: (h, 0, i)),
                pl.BlockSpec((1, s, dq), lambda h, i: (h // group, 0, 0)),
                pl.BlockSpec((1, nchunk, dv_ext, tk), lambda h, i: (h // group, 0, 0, 0))]
    args = [qT, k, vT]
    if diff:
        in_specs += [pl.BlockSpec(lam_p.shape, lambda h, i: (0, 0)),
                     pl.BlockSpec((dv, 1), lambda h, i: (0, 0))]
        args += [lam_p, subln.reshape(dv, 1)]
    return pl.pallas_call(
        functools.partial(_attn_body, nchunk=nchunk, tk=tk, diff=diff, lambda_init=lambda_init),
        grid=(n_heads, s // tq),
        in_specs=in_specs,
        out_specs=pl.BlockSpec((tq, dv), lambda h, i: (i, h)),
        out_shape=jax.ShapeDtypeStruct((s, n_heads * dv), jnp.bfloat16),
        scratch_shapes=[pltpu.VMEM((1, nq), jnp.float32),
                        pltpu.VMEM((dv_ext, nq), jnp.float32),
                        pltpu.VMEM((2, tk, nq), jnp.float32),
                        pltpu.VMEM((2, tk, nq), jnp.bfloat16)],
        compiler_params=_cparams(("parallel", "arbitrary")),
        name=name,
    )(*args)


def _rope_cos_sin(pos, dim, theta):
    inv = theta ** (-jnp.arange(0, dim, 2, dtype=jnp.float32) / dim)
    ang = pos.astype(jnp.float32)[:, None] * inv[None, :]
    return jnp.cos(ang), jnp.sin(ang)


def _rotate_half_tables(cos, sin):
    zero = jnp.zeros_like(sin)
    return (jnp.concatenate([cos, cos], axis=-1),
            jnp.concatenate([-sin, zero], axis=-1),
            jnp.concatenate([zero, sin], axis=-1))


def _rope_tables(s):
    t = jnp.arange(s, dtype=jnp.int32)
    row = t // GRID_W
    col = t % GRID_W
    half = A_HEAD_DIM // 2
    a_row = _rotate_half_tables(*_rope_cos_sin(row, half, A_ROPE_THETA))
    a_col = _rotate_half_tables(*_rope_cos_sin(col, half, A_ROPE_THETA))
    a_tabs = tuple(jnp.concatenate([r, c], axis=-1) for r, c in zip(a_row, a_col))

    b_rot = _rotate_half_tables(*_rope_cos_sin(t, B_ROPE_DIM, B_ROPE_THETA))
    rest = B_QK_DIM - B_ROPE_DIM
    fill = (jnp.ones((s, rest), jnp.float32), jnp.zeros((s, rest), jnp.float32),
            jnp.zeros((s, rest), jnp.float32))
    b_tabs = tuple(jnp.tile(jnp.concatenate([r, f], axis=-1), (1, 2)) for r, f in zip(b_rot, fill))

    c_rot = _rotate_half_tables(*_rope_cos_sin(t, C_ROPE, C_ROPE_THETA))
    ones = jnp.ones((s, C_NOPE), jnp.float32)
    zeros = jnp.zeros((s, C_NOPE), jnp.float32)
    pad = jnp.zeros((s, C_QK_PAD - C_QK_DIM), jnp.float32)
    c_tabs = (jnp.concatenate([ones, c_rot[0], pad], axis=-1),
              jnp.concatenate([zeros, c_rot[1], pad], axis=-1),
              jnp.concatenate([zeros, c_rot[2], pad], axis=-1))
    return a_tabs, b_tabs, c_tabs


def _pad_cols(w, n):
    return jnp.pad(w, ((0, 0), (0, n - w.shape[1])))


def _pad_vec(g, n):
    return jnp.pad(g, (0, n - g.shape[0]))


def kernel(x, attn_norm, w_in, a_q_norm, a_k_norm, b_q_norm, b_k_norm, b_lambda, b_subln,
           c_q_a_norm, c_kv_a_norm, c_w_q_up, c_w_kv_up, c_q_norm, c_k_norm, w_out,
           ffn_norm, w_gate, w_up, w_down):
    bsz, s, d = x.shape
    assert (bsz, s, d) == (1, SEQ, D_MODEL)
    bf = jnp.bfloat16
    a_tabs, b_tabs, c_tabs = _rope_tables(s)
    tp = 512
    h = x.reshape(s, d)

    for l in range(DEPTH):
        lambda_init = 0.8 - 0.6 * math.exp(-0.3 * l)
        w_in_l = _pad_cols(w_in[l].astype(bf), N_IN_PAD)
        wq_up = c_w_q_up[l].astype(bf).reshape(C_Q_LORA, C_HEADS, C_QK_DIM)
        wq_up = jnp.pad(wq_up, ((0, 0), (0, 0), (0, C_QK_PAD - C_QK_DIM))).reshape(C_Q_LORA, C_HEADS * C_QK_PAD)
        wkv_up = c_w_kv_up[l].astype(bf)

        xn = _rmsnorm(h, attn_norm[l])
        proj = _matmul(xn, [w_in_l], out_dtype=jnp.float32, tm=1024, tn=512, tk=d, name="in_proj")

        def proj_blocks(first, count):
            return [(proj, PROJ_BLOCK, first + b) for b in range(count)]

        a_scale = LOG2E / math.sqrt(A_HEAD_DIM)
        a_q = _prep(proj_blocks(0, 3), _head_pieces(A_HEADS), out_kind="cols", tm=tp,
                    gain=a_q_norm[l], norm="full", n_norm=A_HEAD_DIM, tabs=a_tabs, rope_blocks=(0,),
                    shifts=(96, 32), scale=a_scale, name="prep_a_q")
        a_k = _prep(proj_blocks(3, 1), _head_pieces(A_KV_HEADS), out_kind="rows", tm=tp,
                    gain=a_k_norm[l], norm="full", n_norm=A_HEAD_DIM, tabs=a_tabs, rope_blocks=(0,),
                    shifts=(96, 32), name="prep_a_k")
        a_v = _prep(proj_blocks(4, 1), _head_pieces(A_KV_HEADS), out_kind="chunks", tm=tp, name="prep_a_v")
        ya = _attention(a_q, a_k, a_v, tq=1024, name="attn_a")

        b_scale = LOG2E / math.sqrt(B_QK_DIM)
        b_q = _prep(proj_blocks(5, 2), _head_pieces(B_HEADS), out_kind="cols", tm=tp,
                    gain=jnp.tile(b_q_norm[l], 2), norm="half", n_norm=B_QK_DIM, tabs=b_tabs,
                    rope_blocks=(0,), shifts=(120, 8), scale=b_scale, name="prep_b_q")
        b_k = _prep(proj_blocks(7, 2), _head_pieces(B_HEADS), out_kind="rows", tm=tp,
                    gain=jnp.tile(b_k_norm[l], 2), norm="half", n_norm=B_QK_DIM, tabs=b_tabs,
                    rope_blocks=(0,), shifts=(120, 8), name="prep_b_k")
        b_v = _prep(proj_blocks(9, 2), _head_pieces(B_HEADS), out_kind="chunks", tm=tp, name="prep_b_v")
        yb = _attention(b_q, b_k, b_v, tq=512, diff=True, lam_p=b_lambda[l], subln=b_subln[l],
                        lambda_init=lambda_init, name="attn_b")

        cq_lat = _prep(proj_blocks(11, 2), _head_pieces(1, blocks=C_Q_LORA // LANES), out_kind="rows", tm=tp,
                       gain=c_q_a_norm[l], norm="full", n_norm=C_Q_LORA, name="norm_c_q")[0]
        ckv_lat = _prep(proj_blocks(13, 1), _head_pieces(1, blocks=C_KV_LORA // LANES), out_kind="rows", tm=tp,
                        gain=c_kv_a_norm[l], norm="full", n_norm=C_KV_LORA, name="norm_c_kv")[0]
        c_q_raw = _matmul(cq_lat, [wq_up], out_dtype=jnp.float32, tm=1024, tn=512, tk=C_Q_LORA, name="c_q_up")
        c_kv_raw = _matmul(ckv_lat, [wkv_up], out_dtype=jnp.float32, tm=1024, tn=512, tk=C_KV_LORA, name="c_kv_up")
        c_scale = LOG2E / math.sqrt(C_QK_DIM)
        n_up = C_HEADS * C_QK_PAD // PROJ_BLOCK
        c_q = _prep([(c_q_raw, PROJ_BLOCK, b) for b in range(n_up)],
                    _head_pieces(C_HEADS, lanes_per_head=C_QK_PAD, blocks=2), out_kind="cols", tm=tp,
                    gain=_pad_vec(c_q_norm[l], C_QK_PAD), norm="full", n_norm=C_QK_DIM, tabs=c_tabs,
                    rope_blocks=(1,), shifts=(96, 32), scale=c_scale, name="prep_c_q")
        kv_srcs = [(c_kv_raw, PROJ_BLOCK, b) for b in range(n_up)]
        c_k_pieces = [[nope, (n_up, 0)] for (nope,) in _head_pieces(C_HEADS, lanes_per_head=2 * LANES)]
        c_k = _prep(kv_srcs + [(proj, LANES, _C_KR_BLOCK)], c_k_pieces, out_kind="rows", tm=tp,
                    gain=_pad_vec(c_k_norm[l], C_QK_PAD), norm="full", n_norm=C_QK_DIM, tabs=c_tabs,
                    rope_blocks=(1,), shifts=(96, 32), name="prep_c_k")
        c_v = _prep(kv_srcs, _head_pieces(C_HEADS, lanes_per_head=2 * LANES, lane0=LANES), out_kind="chunks",
                    tm=tp, name="prep_c_v")
        yc = _attention(c_q, c_k, c_v, tq=1024, name="attn_c")

        mix = jnp.concatenate([ya, yb, yc], axis=-1)
        h = _matmul(mix, [w_out], res=h, out_dtype=jnp.float32, tm=1024, tn=512, tk=d, layer=l,
                    name="out_proj")

        hn = _rmsnorm(h, ffn_norm[l])
        act = _matmul(hn, [w_gate, w_up], out_dtype=bf, tm=1024, tn=256, tk=d, layer=l, name="ffn_gate_up")
        h = _matmul(act, [w_down], res=h, out_dtype=jnp.float32, tm=1024, tn=256, tk=FFN_HIDDEN // 2,
                    layer=l, name="ffn_down")

    return h.reshape(bsz, s, d)
```

```python
import functools
import math

import jax
import jax.numpy as jnp
import numpy as np
from jax import lax
from jax.experimental import pallas as pl
from jax.experimental.pallas import tpu as pltpu

D_MODEL = 4096
SEQ = 8192
DEPTH = 2
GRID_W = 64
EPS = 1e-6

A_HEAD_DIM = 128
A_HEADS = 12
A_KV_HEADS = 4
A_ROPE_THETA = 10000.0

B_QK_DIM = 64
B_V_DIM = 128
B_HEADS = 8
B_ROPE_DIM = 16
B_ROPE_THETA = 500000.0

C_V_DIM = 128
C_HEADS = 12
C_Q_LORA = 1024
C_KV_LORA = 512
C_NOPE = 128
C_ROPE = 64
C_QK_DIM = C_NOPE + C_ROPE
C_QK_PAD = 256
C_ROPE_THETA = 10000.0

N_IN = 7232
FFN_HIDDEN = 11008

LANES = 128
SUBLANES = 8
ATTN_ROW_BLOCK = 64
ATTN_CHUNKS_PER_TRIP = 8
ONES_ROWS = 16
VMEM_LIMIT_BYTES = 56 * 1024 * 1024

LOG2E = math.log2(math.e)

PROJ_BLOCK = 512
N_IN_MAIN = 14 * PROJ_BLOCK


def _head_pieces(n_heads, lanes_per_head=LANES, lane0=0, blocks=1):
    out = []
    for h in range(n_heads):
        head = []
        for b in range(blocks):
            col = lane0 + h * lanes_per_head + b * LANES
            head.append((col // PROJ_BLOCK, col % PROJ_BLOCK))
        out.append(head)
    return out


def _cparams(sem):
    return pltpu.CompilerParams(dimension_semantics=sem, vmem_limit_bytes=VMEM_LIMIT_BYTES)


def _rmsnorm_body(x_ref, g_ref, o_ref):
    x = x_ref[...]
    ms = jnp.mean(x * x, axis=-1, keepdims=True)
    o_ref[...] = (x * lax.rsqrt(ms + EPS) * g_ref[...]).astype(o_ref.dtype)


def _rmsnorm(x, g, tm=256):
    s, d = x.shape
    return pl.pallas_call(
        _rmsnorm_body,
        grid=(s // tm,),
        in_specs=[pl.BlockSpec((tm, d), lambda i: (i, 0)),
                  pl.BlockSpec((1, d), lambda i: (0, 0))],
        out_specs=pl.BlockSpec((tm, d), lambda i: (i, 0)),
        out_shape=jax.ShapeDtypeStruct((s, d), jnp.bfloat16),
        compiler_params=_cparams(("parallel",)),
        name="rmsnorm",
    )(x, g.reshape(1, d))


def _matmul_body(*refs, n_w, has_res, nk):
    x_ref = refs[0]
    w_refs = refs[1:1 + n_w]
    pos = 1 + n_w
    res_ref = refs[pos] if has_res else None
    pos += int(has_res)
    o_ref = refs[pos]
    acc_refs = refs[pos + 1:]

    def epilogue(accs):
        if n_w == 2:
            g, u = accs
            y = g * jax.nn.sigmoid(g) * u
        else:
            y = accs[0]
        if has_res:
            y = y + res_ref[...]
        o_ref[...] = y.astype(o_ref.dtype)

    x = x_ref[...]
    tiles = [w[...] if w.dtype == x.dtype else w[...].astype(x.dtype) for w in w_refs]
    prods = [jnp.dot(x, w, preferred_element_type=jnp.float32) for w in tiles]
    if nk == 1:
        epilogue(prods)
        return

    k = pl.program_id(2)

    @pl.when(k == 0)
    def _():
        for a, p in zip(acc_refs, prods):
            a[...] = p

    @pl.when(k > 0)
    def _():
        for a, p in zip(acc_refs, prods):
            a[...] += p

    @pl.when(k == nk - 1)
    def _():
        epilogue([a[...] for a in acc_refs])


def _matmul(x, ws, res=None, *, out_dtype, tm, tn, tk, name, layer=None, n_cols=None):
    m, kdim = x.shape
    n = ws[0].shape[-1] if n_cols is None else n_cols
    nk = kdim // tk
    assert m % tm == 0 and n % tn == 0 and kdim % tk == 0
    n_w = len(ws)
    in_specs = [pl.BlockSpec((tm, tk), lambda i, j, k: (i, k))]
    if layer is None:
        in_specs += [pl.BlockSpec((tk, tn), lambda i, j, k: (k, j)) for _ in ws]
    else:
        in_specs += [pl.BlockSpec((None, tk, tn), lambda i, j, k: (layer, k, j)) for _ in ws]
    args = [x, *ws]
    if res is not None:
        in_specs.append(pl.BlockSpec((tm, tn), lambda i, j, k: (i, j)))
        args.append(res)
    scratch = [pltpu.VMEM((tm, tn), jnp.float32) for _ in ws] if nk > 1 else []
    return pl.pallas_call(
        functools.partial(_matmul_body, n_w=n_w, has_res=res is not None, nk=nk),
        grid=(m // tm, n // tn, nk),
        in_specs=in_specs,
        out_specs=pl.BlockSpec((tm, tn), lambda i, j, k: (i, j)),
        out_shape=jax.ShapeDtypeStruct((m, n), out_dtype),
        scratch_shapes=scratch,
        compiler_params=_cparams(("parallel", "parallel", "arbitrary")),
        name=name,
    )(*args)


def _matmul_kouter_body(x_ref, w_ref, res_ref, o_ref, acc_ref, *, nk):
    k = pl.program_id(2)
    j = pl.program_id(3)
    prod = jnp.dot(x_ref[...], w_ref[...].astype(x_ref.dtype), preferred_element_type=jnp.float32)

    @pl.when(k == 0)
    def _():
        acc_ref[j] = prod

    @pl.when(jnp.logical_and(k > 0, k < nk - 1))
    def _():
        acc_ref[j] += prod

    @pl.when(k == nk - 1)
    def _():
        o_ref[...] = (acc_ref[j] + prod + res_ref[...]).astype(o_ref.dtype)


def _matmul_kouter(x, w, res, *, layer, tm, tn, tk, n_split, name):
    m, kdim = x.shape
    n = w.shape[-1]
    nk = kdim // tk
    nj = n // (tn * n_split)
    assert m % tm == 0 and n % (tn * n_split) == 0 and kdim % tk == 0 and nk >= 2

    def out_index(hf, i, k, j):
        return i, hf * nj + jnp.where(k == nk - 1, j, 0)

    return pl.pallas_call(
        functools.partial(_matmul_kouter_body, nk=nk),
        grid=(n_split, m // tm, nk, nj),
        in_specs=[pl.BlockSpec((tm, tk), lambda hf, i, k, j: (i, k)),
                  pl.BlockSpec((None, tk, tn), lambda hf, i, k, j: (layer, k, hf * nj + j)),
                  pl.BlockSpec((tm, tn), out_index)],
        out_specs=pl.BlockSpec((tm, tn), out_index),
        out_shape=jax.ShapeDtypeStruct((m, n), res.dtype),
        scratch_shapes=[pltpu.VMEM((nj, tm, tn), jnp.float32)],
        compiler_params=_cparams(("parallel", "parallel", "arbitrary", "arbitrary")),
        name=name,
    )(x, w, res)


def _prep_head(x, g_ref, tabs, *, norm, n_norm, rope_blocks, shifts, scale, transpose, ones_rows):
    width = x.shape[-1]
    if norm == "full":
        ss = jnp.sum(x * x, axis=-1, keepdims=True)
        x = x * lax.rsqrt(ss * (1.0 / n_norm) + EPS) * g_ref[...]
    elif norm == "half":
        lane = lax.broadcasted_iota(jnp.int32, x.shape, 1)
        lo = lane < (LANES // 2)
        x2 = x * x
        ss_lo = jnp.sum(jnp.where(lo, x2, 0.0), axis=-1, keepdims=True)
        ss_hi = jnp.sum(jnp.where(lo, 0.0, x2), axis=-1, keepdims=True)
        ss = jnp.where(lo, ss_lo, ss_hi)
        x = x * lax.rsqrt(ss * (1.0 / n_norm) + EPS) * g_ref[...]

    if rope_blocks:
        c_ref, s1_ref, s2_ref = tabs
        blocks = []
        for b in range(width // LANES):
            xb = x[:, b * LANES:(b + 1) * LANES]
            if b in rope_blocks:
                sl = slice(b * LANES, (b + 1) * LANES)
                xb = (xb * c_ref[:, sl]
                      + pltpu.roll(xb, shifts[0], 1) * s1_ref[:, sl]
                      + pltpu.roll(xb, shifts[1], 1) * s2_ref[:, sl])
            blocks.append(xb)
        x = blocks[0] if len(blocks) == 1 else jnp.concatenate(blocks, axis=-1)

    if scale != 1.0:
        x = x * scale
    if transpose:
        x = x.T
    if ones_rows:
        x = jnp.concatenate([x, jnp.ones((ones_rows, x.shape[1]), x.dtype)], axis=0)
    return x


def _prep_body(*refs, nx, pieces, has_gain, has_tabs, **head_kw):
    x_refs = refs[:nx]
    pos = nx
    g_ref = None
    if has_gain:
        g_ref = refs[pos]
        pos += 1
    tabs = None
    if has_tabs:
        tabs = refs[pos:pos + 3]
        pos += 3
    o_ref = refs[pos]
    for h, head_pieces in enumerate(pieces):
        cols = [x_refs[src][:, off:off + LANES] for src, off in head_pieces]
        x = cols[0] if len(cols) == 1 else jnp.concatenate(cols, axis=-1)
        y = _prep_head(x, g_ref, tabs, **head_kw)
        o_ref[h] = y.astype(o_ref.dtype).reshape(o_ref.shape[1:])


def _prep(srcs, pieces, *, out_kind, tm, gain=None, norm=None, n_norm=None, tabs=None,
          rope_blocks=(), shifts=(0, 0), scale=1.0, name="prep"):
    s = srcs[0][0].shape[0]
    n_heads = len(pieces)
    width = LANES * len(pieces[0])
    in_specs, args = [], []
    for arr, bw, idx in srcs:
        in_specs.append(pl.BlockSpec((tm, bw), functools.partial(lambda i, idx: (i, idx), idx=idx)))
        args.append(arr)
    if norm is not None:
        in_specs.append(pl.BlockSpec((1, width), lambda i: (0, 0)))
        args.append(gain.reshape(1, width))
    if rope_blocks:
        for t in tabs:
            in_specs.append(pl.BlockSpec((tm, width), lambda i: (i, 0)))
            args.append(t)
    if out_kind == "rows":
        out_shape = (n_heads, s, width)
        out_spec = pl.BlockSpec((n_heads, tm, width), lambda i: (0, i, 0))
    elif out_kind == "cols":
        out_shape = (n_heads, width, s)
        out_spec = pl.BlockSpec((n_heads, width, tm), lambda i: (0, 0, i))
    else:
        out_shape = (n_heads, s // tm, width + ONES_ROWS, tm)
        out_spec = pl.BlockSpec((n_heads, 1, width + ONES_ROWS, tm), lambda i: (0, i, 0, 0))
    return pl.pallas_call(
        functools.partial(_prep_body, nx=len(srcs), pieces=tuple(tuple(p) for p in pieces),
                          has_gain=norm is not None, has_tabs=bool(rope_blocks),
                          norm=norm, n_norm=n_norm, rope_blocks=tuple(rope_blocks), shifts=shifts,
                          scale=scale, transpose=out_kind != "rows",
                          ones_rows=ONES_ROWS if out_kind == "chunks" else 0),
        grid=(s // tm,),
        in_specs=in_specs,
        out_specs=out_spec,
        out_shape=jax.ShapeDtypeStruct(out_shape, jnp.bfloat16),
        compiler_params=_cparams(("parallel",)),
        name=name,
    )(*args)


def _attn_body(*refs, nchunk, tk, diff, lambda_init):
    if diff:
        qT_ref, k_ref, vT_ref, lam_ref, sub_ref, o_ref, m_ref, acc_ref, s_ref, p_ref = refs
    else:
        qT_ref, k_ref, vT_ref, o_ref, m_ref, acc_ref, s_ref, p_ref = refs

    qT = qT_ref[0]
    tq = qT.shape[1]
    if diff:
        row = lax.broadcasted_iota(jnp.int32, qT.shape, 0)
        first = row < B_QK_DIM
        zero = jnp.zeros_like(qT)
        qT = jnp.concatenate([jnp.where(first, qT, zero), jnp.where(first, zero, qT)], axis=1)

    m_ref[...] = jnp.full(m_ref.shape, -jnp.inf, jnp.float32)
    acc_ref[...] = jnp.zeros(acc_ref.shape, jnp.float32)

    def scores(j):
        start = pl.multiple_of(j * tk, tk)
        return jnp.dot(k_ref[0, pl.ds(start, tk), :], qT,
                       preferred_element_type=jnp.float32)

    nq = qT.shape[1]
    rb = ATTN_ROW_BLOCK
    groups = rb // SUBLANES

    def softmax(slot):
        def block(r):
            return s_ref[slot, r * rb:(r + 1) * rb, :].reshape(groups, SUBLANES, nq)

        mx = jnp.max(block(0), axis=0)
        for r in range(1, tk // rb):
            mx = jnp.maximum(mx, jnp.max(block(r), axis=0))
        m_old = m_ref[...]
        m_new = jnp.maximum(m_old, jnp.max(mx, axis=0, keepdims=True))
        alpha = jnp.exp2(m_old - m_new)
        m_rows = jnp.broadcast_to(m_new, (SUBLANES, nq))[None]
        for r in range(tk // rb):
            p = jnp.exp2(block(r) - m_rows).reshape(rb, nq)
            p_ref[slot, r * rb:(r + 1) * rb, :] = p.astype(jnp.bfloat16)
        m_ref[...] = m_new
        return alpha

    def pv_update(j, slot, alpha):
        pv = jnp.dot(vT_ref[0, j], p_ref[slot], preferred_element_type=jnp.float32)
        acc_ref[...] = acc_ref[...] * alpha + pv

    s_ref[0] = scores(0)

    def trip(t, carry):
        for u in range(ATTN_CHUNKS_PER_TRIP):
            j = t * ATTN_CHUNKS_PER_TRIP + u
            s_ref[(u + 1) % 2] = scores(jnp.minimum(j + 1, nchunk - 1))
            alpha = softmax(u % 2)
            pv_update(j, u % 2, alpha)
        return carry

    lax.fori_loop(0, nchunk // ATTN_CHUNKS_PER_TRIP, trip, 0)

    dv = o_ref.shape[1]
    o = acc_ref[:dv, :] / acc_ref[dv:dv + 1, :]
    if diff:
        lp = lam_ref[...]
        lam = (jnp.exp(jnp.sum(lp[0:1] * lp[1:2], axis=-1, keepdims=True))
               - jnp.exp(jnp.sum(lp[2:3] * lp[3:4], axis=-1, keepdims=True))
               + lambda_init)
        o = o[:, :tq] - lam * o[:, tq:]
        ms = jnp.mean(o * o, axis=0, keepdims=True)
        o = o * lax.rsqrt(ms + EPS) * sub_ref[...] * (1.0 - lambda_init)
    o_ref[...] = o.T.astype(o_ref.dtype)


def _attention(qT, k, vT, *, tq, diff=False, lam_p=None, subln=None, lambda_init=0.0, name):
    n_heads, dq, s = qT.shape
    n_kv = k.shape[0]
    group = n_heads // n_kv
    _, nchunk, dv_ext, tk = vT.shape
    dv = dv_ext - ONES_ROWS
    nq = 2 * tq if diff else tq
    in_specs = [pl.BlockSpec((1, dq, tq), lambda h, i: (h, 0, i)),
                pl.BlockSpec((1, s, dq), lambda h, i: (h // group, 0, 0)),
                pl.BlockSpec((1, nchunk, dv_ext, tk), lambda h, i: (h // group, 0, 0, 0))]
    args = [qT, k, vT]
    if diff:
        in_specs += [pl.BlockSpec(lam_p.shape, lambda h, i: (0, 0)),
                     pl.BlockSpec((dv, 1), lambda h, i: (0, 0))]
        args += [lam_p, subln.reshape(dv, 1)]
    return pl.pallas_call(
        functools.partial(_attn_body, nchunk=nchunk, tk=tk, diff=diff, lambda_init=lambda_init),
        grid=(n_heads, s // tq),
        in_specs=in_specs,
        out_specs=pl.BlockSpec((tq, dv), lambda h, i: (i, h)),
        out_shape=jax.ShapeDtypeStruct((s, n_heads * dv), jnp.bfloat16),
        scratch_shapes=[pltpu.VMEM((1, nq), jnp.float32),
                        pltpu.VMEM((dv_ext, nq), jnp.float32),
                        pltpu.VMEM((2, tk, nq), jnp.float32),
                        pltpu.VMEM((2, tk, nq), jnp.bfloat16)],
        compiler_params=_cparams(("parallel", "arbitrary")),
        name=name,
    )(*args)


def _rope_cos_sin(pos, dim, theta):
    inv = theta ** (-jnp.arange(0, dim, 2, dtype=jnp.float32) / dim)
    ang = pos.astype(jnp.float32)[:, None] * inv[None, :]
    return jnp.cos(ang), jnp.sin(ang)


def _rotate_half_tables(cos, sin):
    zero = jnp.zeros_like(sin)
    return (jnp.concatenate([cos, cos], axis=-1),
            jnp.concatenate([-sin, zero], axis=-1),
            jnp.concatenate([zero, sin], axis=-1))


def _rope_tables(s):
    t = jnp.arange(s, dtype=jnp.int32)
    row = t // GRID_W
    col = t % GRID_W
    half = A_HEAD_DIM // 2
    a_row = _rotate_half_tables(*_rope_cos_sin(row, half, A_ROPE_THETA))
    a_col = _rotate_half_tables(*_rope_cos_sin(col, half, A_ROPE_THETA))
    a_tabs = tuple(jnp.concatenate([r, c], axis=-1) for r, c in zip(a_row, a_col))

    b_rot = _rotate_half_tables(*_rope_cos_sin(t, B_ROPE_DIM, B_ROPE_THETA))
    rest = B_QK_DIM - B_ROPE_DIM
    fill = (jnp.ones((s, rest), jnp.float32), jnp.zeros((s, rest), jnp.float32),
            jnp.zeros((s, rest), jnp.float32))
    b_tabs = tuple(jnp.tile(jnp.concatenate([r, f], axis=-1), (1, 2)) for r, f in zip(b_rot, fill))

    c_rot = _rotate_half_tables(*_rope_cos_sin(t, C_ROPE, C_ROPE_THETA))
    ones = jnp.ones((s, C_NOPE), jnp.float32)
    zeros = jnp.zeros((s, C_NOPE), jnp.float32)
    pad = jnp.zeros((s, C_QK_PAD - C_QK_DIM), jnp.float32)
    c_tabs = (jnp.concatenate([ones, c_rot[0], pad], axis=-1),
              jnp.concatenate([zeros, c_rot[1], pad], axis=-1),
              jnp.concatenate([zeros, c_rot[2], pad], axis=-1))
    return a_tabs, b_tabs, c_tabs


def _pad_cols(w, n):
    return jnp.pad(w, ((0, 0), (0, n - w.shape[1])))


def _pad_vec(g, n):
    return jnp.pad(g, (0, n - g.shape[0]))


def kernel(x, attn_norm, w_in, a_q_norm, a_k_norm, b_q_norm, b_k_norm, b_lambda, b_subln,
           c_q_a_norm, c_kv_a_norm, c_w_q_up, c_w_kv_up, c_q_norm, c_k_norm, w_out,
           ffn_norm, w_gate, w_up, w_down):
    bsz, s, d = x.shape
    assert (bsz, s, d) == (1, SEQ, D_MODEL)
    bf = jnp.bfloat16
    a_tabs, b_tabs, c_tabs = _rope_tables(s)
    tp = 512
    h = x.reshape(s, d)

    for l in range(DEPTH):
        lambda_init = 0.8 - 0.6 * math.exp(-0.3 * l)
        w_kr = _pad_cols(w_in[l, :, N_IN_MAIN:].astype(bf), LANES)
        wq_up = c_w_q_up[l].astype(bf).reshape(C_Q_LORA, C_HEADS, C_QK_DIM)
        wq_up = jnp.pad(wq_up, ((0, 0), (0, 0), (0, C_QK_PAD - C_QK_DIM))).reshape(C_Q_LORA, C_HEADS * C_QK_PAD)
        wkv_up = c_w_kv_up[l].astype(bf)

        xn = _rmsnorm(h, attn_norm[l])
        proj = _matmul(xn, [w_in], out_dtype=jnp.float32, tm=1024, tn=PROJ_BLOCK, tk=d, layer=l,
                       n_cols=N_IN_MAIN, name="in_proj")
        c_kr = _matmul(xn, [w_kr], out_dtype=jnp.float32, tm=1024, tn=LANES, tk=d, name="in_proj_kr")

        def proj_blocks(first, count):
            return [(proj, PROJ_BLOCK, first + b) for b in range(count)]

        a_scale = LOG2E / math.sqrt(A_HEAD_DIM)
        a_q = _prep(proj_blocks(0, 3), _head_pieces(A_HEADS), out_kind="cols", tm=tp,
                    gain=a_q_norm[l], norm="full", n_norm=A_HEAD_DIM, tabs=a_tabs, rope_blocks=(0,),
                    shifts=(96, 32), scale=a_scale, name="prep_a_q")
        a_k = _prep(proj_blocks(3, 1), _head_pieces(A_KV_HEADS), out_kind="rows", tm=tp,
                    gain=a_k_norm[l], norm="full", n_norm=A_HEAD_DIM, tabs=a_tabs, rope_blocks=(0,),
                    shifts=(96, 32), name="prep_a_k")
        a_v = _prep(proj_blocks(4, 1), _head_pieces(A_KV_HEADS), out_kind="chunks", tm=tp, name="prep_a_v")
        ya = _attention(a_q, a_k, a_v, tq=1024, name="attn_a")

        b_scale = LOG2E / math.sqrt(B_QK_DIM)
        b_q = _prep(proj_blocks(5, 2), _head_pieces(B_HEADS), out_kind="cols", tm=tp,
                    gain=jnp.tile(b_q_norm[l], 2), norm="half", n_norm=B_QK_DIM, tabs=b_tabs,
                    rope_blocks=(0,), shifts=(120, 8), scale=b_scale, name="prep_b_q")
        b_k = _prep(proj_blocks(7, 2), _head_pieces(B_HEADS), out_kind="rows", tm=tp,
                    gain=jnp.tile(b_k_norm[l], 2), norm="half", n_norm=B_QK_DIM, tabs=b_tabs,
                    rope_blocks=(0,), shifts=(120, 8), name="prep_b_k")
        b_v = _prep(proj_blocks(9, 2), _head_pieces(B_HEADS), out_kind="chunks", tm=tp, name="prep_b_v")
        yb = _attention(b_q, b_k, b_v, tq=512, diff=True, lam_p=b_lambda[l], subln=b_subln[l],
                        lambda_init=lambda_init, name="attn_b")

        cq_lat = _prep(proj_blocks(11, 2), _head_pieces(1, blocks=C_Q_LORA // LANES), out_kind="rows", tm=tp,
                       gain=c_q_a_norm[l], norm="full", n_norm=C_Q_LORA, name="norm_c_q")[0]
        ckv_lat = _prep(proj_blocks(13, 1), _head_pieces(1, blocks=C_KV_LORA // LANES), out_kind="rows", tm=tp,
                        gain=c_kv_a_norm[l], norm="full", n_norm=C_KV_LORA, name="norm_c_kv")[0]
        c_q_raw = _matmul(cq_lat, [wq_up], out_dtype=jnp.float32, tm=1024, tn=512, tk=C_Q_LORA, name="c_q_up")
        c_kv_raw = _matmul(ckv_lat, [wkv_up], out_dtype=jnp.float32, tm=1024, tn=512, tk=C_KV_LORA, name="c_kv_up")
        c_scale = LOG2E / math.sqrt(C_QK_DIM)
        n_up = C_HEADS * C_QK_PAD // PROJ_BLOCK
        c_q = _prep([(c_q_raw, PROJ_BLOCK, b) for b in range(n_up)],
                    _head_pieces(C_HEADS, lanes_per_head=C_QK_PAD, blocks=2), out_kind="cols", tm=tp,
                    gain=_pad_vec(c_q_norm[l], C_QK_PAD), norm="full", n_norm=C_QK_DIM, tabs=c_tabs,
                    rope_blocks=(1,), shifts=(96, 32), scale=c_scale, name="prep_c_q")
        kv_srcs = [(c_kv_raw, PROJ_BLOCK, b) for b in range(n_up)]
        c_k_pieces = [[nope, (n_up, 0)] for (nope,) in _head_pieces(C_HEADS, lanes_per_head=2 * LANES)]
        c_k = _prep(kv_srcs + [(c_kr, LANES, 0)], c_k_pieces, out_kind="rows", tm=tp,
                    gain=_pad_vec(c_k_norm[l], C_QK_PAD), norm="full", n_norm=C_QK_DIM, tabs=c_tabs,
                    rope_blocks=(1,), shifts=(96, 32), name="prep_c_k")
        c_v = _prep(kv_srcs, _head_pieces(C_HEADS, lanes_per_head=2 * LANES, lane0=LANES), out_kind="chunks",
                    tm=tp, name="prep_c_v")
        yc = _attention(c_q, c_k, c_v, tq=1024, name="attn_c")

        mix = jnp.concatenate([ya, yb, yc], axis=-1)
        h = _matmul(mix, [w_out], res=h, out_dtype=jnp.float32, tm=1024, tn=512, tk=d, layer=l,
                    name="out_proj")

        hn = _rmsnorm(h, ffn_norm[l])
        act = _matmul(hn, [w_gate, w_up], out_dtype=bf, tm=1024, tn=256, tk=d, layer=l, name="ffn_gate_up")
        h = _matmul_kouter(act, w_down, h, layer=l, tm=1024, tn=256, tk=FFN_HIDDEN // 2, n_split=2,
                           name="ffn_down")

    return h.reshape(bsz, s, d)
```

```python
import functools
import math

import jax
import jax.numpy as jnp
import numpy as np
from jax import lax
from jax.experimental import pallas as pl
from jax.experimental.pallas import tpu as pltpu

D_MODEL = 4096
SEQ = 8192
DEPTH = 2
GRID_W = 64
EPS = 1e-6

A_HEAD_DIM = 128
A_HEADS = 12
A_KV_HEADS = 4
A_ROPE_THETA = 10000.0

B_QK_DIM = 64
B_V_DIM = 128
B_HEADS = 8
B_ROPE_DIM = 16
B_ROPE_THETA = 500000.0

C_V_DIM = 128
C_HEADS = 12
C_Q_LORA = 1024
C_KV_LORA = 512
C_NOPE = 128
C_ROPE = 64
C_QK_DIM = C_NOPE + C_ROPE
C_QK_PAD = 256
C_ROPE_THETA = 10000.0

N_IN = 7232
FFN_HIDDEN = 11008

LANES = 128
SUBLANES = 8
ATTN_ROW_BLOCK = 64
ATTN_CHUNKS_PER_TRIP = 8
ONES_ROWS = 16
VMEM_LIMIT_BYTES = 56 * 1024 * 1024

LOG2E = math.log2(math.e)

PROJ_BLOCK = 512
N_IN_MAIN = 14 * PROJ_BLOCK


def _head_pieces(n_heads, lanes_per_head=LANES, lane0=0, blocks=1):
    out = []
    for h in range(n_heads):
        head = []
        for b in range(blocks):
            col = lane0 + h * lanes_per_head + b * LANES
            head.append((col // PROJ_BLOCK, col % PROJ_BLOCK))
        out.append(head)
    return out


def _cparams(sem):
    return pltpu.CompilerParams(dimension_semantics=sem, vmem_limit_bytes=VMEM_LIMIT_BYTES)


def _rmsnorm_body(x_ref, g_ref, o_ref):
    x = x_ref[...]
    ms = jnp.mean(x * x, axis=-1, keepdims=True)
    o_ref[...] = (x * lax.rsqrt(ms + EPS) * g_ref[...]).astype(o_ref.dtype)


def _rmsnorm(x, g, tm=256):
    s, d = x.shape
    return pl.pallas_call(
        _rmsnorm_body,
        grid=(s // tm,),
        in_specs=[pl.BlockSpec((tm, d), lambda i: (i, 0)),
                  pl.BlockSpec((1, d), lambda i: (0, 0))],
        out_specs=pl.BlockSpec((tm, d), lambda i: (i, 0)),
        out_shape=jax.ShapeDtypeStruct((s, d), jnp.bfloat16),
        compiler_params=_cparams(("parallel",)),
        name="rmsnorm",
    )(x, g.reshape(1, d))


def _matmul_body(*refs, n_x, n_w, has_res, w_transposed):
    x_refs = refs[:n_x]
    w_refs = refs[n_x:n_x + n_w]
    pos = n_x + n_w
    res_ref = refs[pos] if has_res else None
    o_ref = refs[pos + int(has_res)]

    x = x_refs[0][...] if n_x == 1 else jnp.concatenate([r[...] for r in x_refs], axis=1)
    tiles = [w[...] if w.dtype == x.dtype else w[...].astype(x.dtype) for w in w_refs]
    contract = (((1,), (1,)), ((), ())) if w_transposed else (((1,), (0,)), ((), ()))
    prods = [lax.dot_general(x, w, contract, preferred_element_type=jnp.float32) for w in tiles]
    if n_w == 2:
        g, u = prods
        y = g * jax.nn.sigmoid(g) * u
    else:
        y = prods[0]
    if has_res:
        y = y + res_ref[...]
    o_ref[...] = y.astype(o_ref.dtype)


def _matmul(xs, ws, res=None, *, out_dtype, tm, tn, name, layer=None, w_transposed=False, n_cols=None):
    m = xs[0].shape[0]
    kdim = sum(x.shape[1] for x in xs)
    n = n_cols if n_cols is not None else ws[0].shape[-2 if w_transposed else -1]
    assert m % tm == 0 and n % tn == 0
    in_specs = [pl.BlockSpec((tm, x.shape[1]), lambda i, j: (i, 0)) for x in xs]
    w_block = (tn, kdim) if w_transposed else (kdim, tn)
    lead = () if layer is None else (None,)

    def w_index(i, j):
        pos = (j, 0) if w_transposed else (0, j)
        return pos if layer is None else (layer,) + pos

    in_specs += [pl.BlockSpec(lead + w_block, w_index) for _ in ws]
    args = [*xs, *ws]
    if res is not None:
        in_specs.append(pl.BlockSpec((tm, tn), lambda i, j: (i, j)))
        args.append(res)
    return pl.pallas_call(
        functools.partial(_matmul_body, n_x=len(xs), n_w=len(ws), has_res=res is not None,
                          w_transposed=w_transposed),
        grid=(m // tm, n // tn),
        in_specs=in_specs,
        out_specs=pl.BlockSpec((tm, tn), lambda i, j: (i, j)),
        out_shape=jax.ShapeDtypeStruct((m, n), out_dtype),
        compiler_params=_cparams(("parallel", "parallel")),
        name=name,
    )(*args)


def _tail_proj_body(x_ref, w_ref, o_ref):
    w = w_ref[...].astype(x_ref.dtype)
    w = jnp.concatenate([w, jnp.zeros_like(w)], axis=0)
    o_ref[...] = lax.dot_general(x_ref[...], w, (((1,), (1,)), ((), ())),
                                 preferred_element_type=jnp.float32)


def _tail_proj(x, w_t, *, layer, row0, tm, name):
    m, kdim = x.shape
    rows = w_t.shape[1] - row0
    assert rows * 2 == LANES and row0 % rows == 0
    return pl.pallas_call(
        _tail_proj_body,
        grid=(m // tm,),
        in_specs=[pl.BlockSpec((tm, kdim), lambda i: (i, 0)),
                  pl.BlockSpec((None, rows, kdim), lambda i: (layer, row0 // rows, 0))],
        out_specs=pl.BlockSpec((tm, LANES), lambda i: (i, 0)),
        out_shape=jax.ShapeDtypeStruct((m, LANES), jnp.float32),
        compiler_params=_cparams(("parallel",)),
        name=name,
    )(x, w_t)


def _matmul_kouter_body(x_ref, w_ref, res_ref, o_ref, acc_ref, *, nk):
    k = pl.program_id(2)
    j = pl.program_id(3)
    prod = jnp.dot(x_ref[...], w_ref[...].astype(x_ref.dtype), preferred_element_type=jnp.float32)

    @pl.when(k == 0)
    def _():
        acc_ref[j] = prod

    @pl.when(jnp.logical_and(k > 0, k < nk - 1))
    def _():
        acc_ref[j] += prod

    @pl.when(k == nk - 1)
    def _():
        o_ref[...] = (acc_ref[j] + prod + res_ref[...]).astype(o_ref.dtype)


def _matmul_kouter(x, w, res, *, layer, tm, tn, tk, n_split, name):
    m, kdim = x.shape
    n = w.shape[-1]
    nk = kdim // tk
    nj = n // (tn * n_split)
    assert m % tm == 0 and n % (tn * n_split) == 0 and kdim % tk == 0 and nk >= 2

    def out_index(hf, i, k, j):
        return i, hf * nj + jnp.where(k == nk - 1, j, 0)

    return pl.pallas_call(
        functools.partial(_matmul_kouter_body, nk=nk),
        grid=(n_split, m // tm, nk, nj),
        in_specs=[pl.BlockSpec((tm, tk), lambda hf, i, k, j: (i, k)),
                  pl.BlockSpec((None, tk, tn), lambda hf, i, k, j: (layer, k, hf * nj + j)),
                  pl.BlockSpec((tm, tn), out_index)],
        out_specs=pl.BlockSpec((tm, tn), out_index),
        out_shape=jax.ShapeDtypeStruct((m, n), res.dtype),
        scratch_shapes=[pltpu.VMEM((nj, tm, tn), jnp.float32)],
        compiler_params=_cparams(("parallel", "parallel", "arbitrary", "arbitrary")),
        name=name,
    )(x, w, res)


def _prep_head(x, g_ref, tabs, *, norm, n_norm, rope_blocks, shifts, scale, transpose, ones_rows):
    width = x.shape[-1]
    if norm == "full":
        ss = jnp.sum(x * x, axis=-1, keepdims=True)
        x = x * lax.rsqrt(ss * (1.0 / n_norm) + EPS) * g_ref[...]
    elif norm == "half":
        lane = lax.broadcasted_iota(jnp.int32, x.shape, 1)
        lo = lane < (LANES // 2)
        x2 = x * x
        ss_lo = jnp.sum(jnp.where(lo, x2, 0.0), axis=-1, keepdims=True)
        ss_hi = jnp.sum(jnp.where(lo, 0.0, x2), axis=-1, keepdims=True)
        ss = jnp.where(lo, ss_lo, ss_hi)
        x = x * lax.rsqrt(ss * (1.0 / n_norm) + EPS) * g_ref[...]

    if rope_blocks:
        c_ref, s1_ref, s2_ref = tabs
        blocks = []
        for b in range(width // LANES):
            xb = x[:, b * LANES:(b + 1) * LANES]
            if b in rope_blocks:
                sl = slice(b * LANES, (b + 1) * LANES)
                xb = (xb * c_ref[:, sl]
                      + pltpu.roll(xb, shifts[0], 1) * s1_ref[:, sl]
                      + pltpu.roll(xb, shifts[1], 1) * s2_ref[:, sl])
            blocks.append(xb)
        x = blocks[0] if len(blocks) == 1 else jnp.concatenate(blocks, axis=-1)

    if scale != 1.0:
        x = x * scale
    if transpose:
        x = x.T
    if ones_rows:
        x = jnp.concatenate([x, jnp.ones((ones_rows, x.shape[1]), x.dtype)], axis=0)
    return x


def _prep_body(*refs, nx, pieces, has_gain, has_tabs, **head_kw):
    x_refs = refs[:nx]
    pos = nx
    g_ref = None
    if has_gain:
        g_ref = refs[pos]
        pos += 1
    tabs = None
    if has_tabs:
        tabs = refs[pos:pos + 3]
        pos += 3
    o_ref = refs[pos]
    for h, head_pieces in enumerate(pieces):
        cols = [x_refs[src][:, off:off + LANES] for src, off in head_pieces]
        x = cols[0] if len(cols) == 1 else jnp.concatenate(cols, axis=-1)
        y = _prep_head(x, g_ref, tabs, **head_kw)
        o_ref[h] = y.astype(o_ref.dtype).reshape(o_ref.shape[1:])


def _prep(srcs, pieces, *, out_kind, tm, gain=None, norm=None, n_norm=None, tabs=None,
          rope_blocks=(), shifts=(0, 0), scale=1.0, name="prep"):
    s = srcs[0][0].shape[0]
    n_heads = len(pieces)
    width = LANES * len(pieces[0])
    in_specs, args = [], []
    for arr, bw, idx in srcs:
        in_specs.append(pl.BlockSpec((tm, bw), functools.partial(lambda i, idx: (i, idx), idx=idx)))
        args.append(arr)
    if norm is not None:
        in_specs.append(pl.BlockSpec((1, width), lambda i: (0, 0)))
        args.append(gain.reshape(1, width))
    if rope_blocks:
        for t in tabs:
            in_specs.append(pl.BlockSpec((tm, width), lambda i: (i, 0)))
            args.append(t)
    if out_kind == "rows":
        out_shape = (n_heads, s, width)
        out_spec = pl.BlockSpec((n_heads, tm, width), lambda i: (0, i, 0))
    elif out_kind == "cols":
        out_shape = (n_heads, width, s)
        out_spec = pl.BlockSpec((n_heads, width, tm), lambda i: (0, 0, i))
    else:
        out_shape = (n_heads, s // tm, width + ONES_ROWS, tm)
        out_spec = pl.BlockSpec((n_heads, 1, width + ONES_ROWS, tm), lambda i: (0, i, 0, 0))
    return pl.pallas_call(
        functools.partial(_prep_body, nx=len(srcs), pieces=tuple(tuple(p) for p in pieces),
                          has_gain=norm is not None, has_tabs=bool(rope_blocks),
                          norm=norm, n_norm=n_norm, rope_blocks=tuple(rope_blocks), shifts=shifts,
                          scale=scale, transpose=out_kind != "rows",
                          ones_rows=ONES_ROWS if out_kind == "chunks" else 0),
        grid=(s // tm,),
        in_specs=in_specs,
        out_specs=out_spec,
        out_shape=jax.ShapeDtypeStruct(out_shape, jnp.bfloat16),
        compiler_params=_cparams(("parallel",)),
        name=name,
    )(*args)


def _attn_body(*refs, nchunk, tk, diff, lambda_init):
    if diff:
        qT_ref, k_ref, vT_ref, lam_ref, sub_ref, o_ref, m_ref, acc_ref, s_ref, p_ref = refs
    else:
        qT_ref, k_ref, vT_ref, o_ref, m_ref, acc_ref, s_ref, p_ref = refs

    qT = qT_ref[0]
    tq = qT.shape[1]
    if diff:
        row = lax.broadcasted_iota(jnp.int32, qT.shape, 0)
        first = row < B_QK_DIM
        zero = jnp.zeros_like(qT)
        qT = jnp.concatenate([jnp.where(first, qT, zero), jnp.where(first, zero, qT)], axis=1)

    m_ref[...] = jnp.full(m_ref.shape, -jnp.inf, jnp.float32)
    acc_ref[...] = jnp.zeros(acc_ref.shape, jnp.float32)

    def scores(j):
        start = pl.multiple_of(j * tk, tk)
        return jnp.dot(k_ref[0, pl.ds(start, tk), :], qT,
                       preferred_element_type=jnp.float32)

    nq = qT.shape[1]
    rb = ATTN_ROW_BLOCK
    groups = rb // SUBLANES

    def softmax(slot):
        def block(r):
            return s_ref[slot, r * rb:(r + 1) * rb, :].reshape(groups, SUBLANES, nq)

        mx = jnp.max(block(0), axis=0)
        for r in range(1, tk // rb):
            mx = jnp.maximum(mx, jnp.max(block(r), axis=0))
        m_old = m_ref[...]
        m_new = jnp.maximum(m_old, jnp.max(mx, axis=0, keepdims=True))
        alpha = jnp.exp2(m_old - m_new)
        m_rows = jnp.broadcast_to(m_new, (SUBLANES, nq))[None]
        for r in range(tk // rb):
            p = jnp.exp2(block(r) - m_rows).reshape(rb, nq)
            p_ref[slot, r * rb:(r + 1) * rb, :] = p.astype(jnp.bfloat16)
        m_ref[...] = m_new
        return alpha

    def pv_update(j, slot, alpha):
        pv = jnp.dot(vT_ref[0, j], p_ref[slot], preferred_element_type=jnp.float32)
        acc_ref[...] = acc_ref[...] * alpha + pv

    s_ref[0] = scores(0)

    def trip(t, carry):
        for u in range(ATTN_CHUNKS_PER_TRIP):
            j = t * ATTN_CHUNKS_PER_TRIP + u
            s_ref[(u + 1) % 2] = scores(jnp.minimum(j + 1, nchunk - 1))
            alpha = softmax(u % 2)
            pv_update(j, u % 2, alpha)
        return carry

    lax.fori_loop(0, nchunk // ATTN_CHUNKS_PER_TRIP, trip, 0)

    dv = o_ref.shape[1]
    o = acc_ref[:dv, :] / acc_ref[dv:dv + 1, :]
    if diff:
        lp = lam_ref[...]
        lam = (jnp.exp(jnp.sum(lp[0:1] * lp[1:2], axis=-1, keepdims=True))
               - jnp.exp(jnp.sum(lp[2:3] * lp[3:4], axis=-1, keepdims=True))
               + lambda_init)
        o = o[:, :tq] - lam * o[:, tq:]
        ms = jnp.mean(o * o, axis=0, keepdims=True)
        o = o * lax.rsqrt(ms + EPS) * sub_ref[...] * (1.0 - lambda_init)
    o_ref[...] = o.T.astype(o_ref.dtype)


def _attention(qT, k, vT, *, tq, diff=False, lam_p=None, subln=None, lambda_init=0.0, name):
    n_heads, dq, s = qT.shape
    n_kv = k.shape[0]
    group = n_heads // n_kv
    _, nchunk, dv_ext, tk = vT.shape
    dv = dv_ext - ONES_ROWS
    nq = 2 * tq if diff else tq
    in_specs = [pl.BlockSpec((1, dq, tq), lambda h, i: (h, 0, i)),
                pl.BlockSpec((1, s, dq), lambda h, i: (h // group, 0, 0)),
                pl.BlockSpec((1, nchunk, dv_ext, tk), lambda h, i: (h // group, 0, 0, 0))]
    args = [qT, k, vT]
    if diff:
        in_specs += [pl.BlockSpec(lam_p.shape, lambda h, i: (0, 0)),
                     pl.BlockSpec((dv, 1), lambda h, i: (0, 0))]
        args += [lam_p, subln.reshape(dv, 1)]
    return pl.pallas_call(
        functools.partial(_attn_body, nchunk=nchunk, tk=tk, diff=diff, lambda_init=lambda_init),
        grid=(n_heads, s // tq),
        in_specs=in_specs,
        out_specs=pl.BlockSpec((tq, dv), lambda h, i: (i, h)),
        out_shape=jax.ShapeDtypeStruct((s, n_heads * dv), jnp.bfloat16),
        scratch_shapes=[pltpu.VMEM((1, nq), jnp.float32),
                        pltpu.VMEM((dv_ext, nq), jnp.float32),
                        pltpu.VMEM((2, tk, nq), jnp.float32),
                        pltpu.VMEM((2, tk, nq), jnp.bfloat16)],
        compiler_params=_cparams(("parallel", "arbitrary")),
        name=name,
    )(*args)


def _rope_cos_sin(pos, dim, theta):
    inv = theta ** (-jnp.arange(0, dim, 2, dtype=jnp.float32) / dim)
    ang = pos.astype(jnp.float32)[:, None] * inv[None, :]
    return jnp.cos(ang), jnp.sin(ang)


def _rotate_half_tables(cos, sin):
    zero = jnp.zeros_like(sin)
    return (jnp.concatenate([cos, cos], axis=-1),
            jnp.concatenate([-sin, zero], axis=-1),
            jnp.concatenate([zero, sin], axis=-1))


def _rope_tables(s):
    t = jnp.arange(s, dtype=jnp.int32)
    row = t // GRID_W
    col = t % GRID_W
    half = A_HEAD_DIM // 2
    a_row = _rotate_half_tables(*_rope_cos_sin(row, half, A_ROPE_THETA))
    a_col = _rotate_half_tables(*_rope_cos_sin(col, half, A_ROPE_THETA))
    a_tabs = tuple(jnp.concatenate([r, c], axis=-1) for r, c in zip(a_row, a_col))

    b_rot = _rotate_half_tables(*_rope_cos_sin(t, B_ROPE_DIM, B_ROPE_THETA))
    rest = B_QK_DIM - B_ROPE_DIM
    fill = (jnp.ones((s, rest), jnp.float32), jnp.zeros((s, rest), jnp.float32),
            jnp.zeros((s, rest), jnp.float32))
    b_tabs = tuple(jnp.tile(jnp.concatenate([r, f], axis=-1), (1, 2)) for r, f in zip(b_rot, fill))

    c_rot = _rotate_half_tables(*_rope_cos_sin(t, C_ROPE, C_ROPE_THETA))
    ones = jnp.ones((s, C_NOPE), jnp.float32)
    zeros = jnp.zeros((s, C_NOPE), jnp.float32)
    pad = jnp.zeros((s, C_QK_PAD - C_QK_DIM), jnp.float32)
    c_tabs = (jnp.concatenate([ones, c_rot[0], pad], axis=-1),
              jnp.concatenate([zeros, c_rot[1], pad], axis=-1),
              jnp.concatenate([zeros, c_rot[2], pad], axis=-1))
    return a_tabs, b_tabs, c_tabs


def _pad_vec(g, n):
    return jnp.pad(g, (0, n - g.shape[0]))


def kernel(x, attn_norm, w_in, a_q_norm, a_k_norm, b_q_norm, b_k_norm, b_lambda, b_subln,
           c_q_a_norm, c_kv_a_norm, c_w_q_up, c_w_kv_up, c_q_norm, c_k_norm, w_out,
           ffn_norm, w_gate, w_up, w_down):
    bsz, s, d = x.shape
    assert (bsz, s, d) == (1, SEQ, D_MODEL)
    bf = jnp.bfloat16
    a_tabs, b_tabs, c_tabs = _rope_tables(s)
    tp = 512
    h = x.reshape(s, d)
    w_in_t = jnp.swapaxes(w_in, 1, 2)

    for l in range(DEPTH):
        lambda_init = 0.8 - 0.6 * math.exp(-0.3 * l)
        wq_up = c_w_q_up[l].astype(bf).reshape(C_Q_LORA, C_HEADS, C_QK_DIM)
        wq_up = jnp.pad(wq_up, ((0, 0), (0, 0), (0, C_QK_PAD - C_QK_DIM))).reshape(C_Q_LORA, C_HEADS * C_QK_PAD)
        wkv_up = c_w_kv_up[l].astype(bf)

        xn = _rmsnorm(h, attn_norm[l])
        proj = _matmul([xn], [w_in_t], out_dtype=jnp.float32, tm=1024, tn=PROJ_BLOCK, layer=l,
                       w_transposed=True, n_cols=N_IN_MAIN, name="in_proj")
        c_kr = _tail_proj(xn, w_in_t, layer=l, row0=N_IN_MAIN, tm=1024, name="in_proj_kr")

        def proj_blocks(first, count):
            return [(proj, PROJ_BLOCK, first + b) for b in range(count)]

        a_scale = LOG2E / math.sqrt(A_HEAD_DIM)
        a_q = _prep(proj_blocks(0, 3), _head_pieces(A_HEADS), out_kind="cols", tm=tp,
                    gain=a_q_norm[l], norm="full", n_norm=A_HEAD_DIM, tabs=a_tabs, rope_blocks=(0,),
                    shifts=(96, 32), scale=a_scale, name="prep_a_q")
        a_k = _prep(proj_blocks(3, 1), _head_pieces(A_KV_HEADS), out_kind="rows", tm=tp,
                    gain=a_k_norm[l], norm="full", n_norm=A_HEAD_DIM, tabs=a_tabs, rope_blocks=(0,),
                    shifts=(96, 32), name="prep_a_k")
        a_v = _prep(proj_blocks(4, 1), _head_pieces(A_KV_HEADS), out_kind="chunks", tm=tp, name="prep_a_v")
        ya = _attention(a_q, a_k, a_v, tq=1024, name="attn_a")

        b_scale = LOG2E / math.sqrt(B_QK_DIM)
        b_q = _prep(proj_blocks(5, 2), _head_pieces(B_HEADS), out_kind="cols", tm=tp,
                    gain=jnp.tile(b_q_norm[l], 2), norm="half", n_norm=B_QK_DIM, tabs=b_tabs,
                    rope_blocks=(0,), shifts=(120, 8), scale=b_scale, name="prep_b_q")
        b_k = _prep(proj_blocks(7, 2), _head_pieces(B_HEADS), out_kind="rows", tm=tp,
                    gain=jnp.tile(b_k_norm[l], 2), norm="half", n_norm=B_QK_DIM, tabs=b_tabs,
                    rope_blocks=(0,), shifts=(120, 8), name="prep_b_k")
        b_v = _prep(proj_blocks(9, 2), _head_pieces(B_HEADS), out_kind="chunks", tm=tp, name="prep_b_v")
        yb = _attention(b_q, b_k, b_v, tq=512, diff=True, lam_p=b_lambda[l], subln=b_subln[l],
                        lambda_init=lambda_init, name="attn_b")

        cq_lat = _prep(proj_blocks(11, 2), _head_pieces(1, blocks=C_Q_LORA // LANES), out_kind="rows", tm=tp,
                       gain=c_q_a_norm[l], norm="full", n_norm=C_Q_LORA, name="norm_c_q")[0]
        ckv_lat = _prep(proj_blocks(13, 1), _head_pieces(1, blocks=C_KV_LORA // LANES), out_kind="rows", tm=tp,
                        gain=c_kv_a_norm[l], norm="full", n_norm=C_KV_LORA, name="norm_c_kv")[0]
        c_q_raw = _matmul([cq_lat], [wq_up], out_dtype=jnp.float32, tm=1024, tn=512, name="c_q_up")
        c_kv_raw = _matmul([ckv_lat], [wkv_up], out_dtype=jnp.float32, tm=1024, tn=512, name="c_kv_up")
        c_scale = LOG2E / math.sqrt(C_QK_DIM)
        n_up = C_HEADS * C_QK_PAD // PROJ_BLOCK
        c_q = _prep([(c_q_raw, PROJ_BLOCK, b) for b in range(n_up)],
                    _head_pieces(C_HEADS, lanes_per_head=C_QK_PAD, blocks=2), out_kind="cols", tm=tp,
                    gain=_pad_vec(c_q_norm[l], C_QK_PAD), norm="full", n_norm=C_QK_DIM, tabs=c_tabs,
                    rope_blocks=(1,), shifts=(96, 32), scale=c_scale, name="prep_c_q")
        kv_srcs = [(c_kv_raw, PROJ_BLOCK, b) for b in range(n_up)]
        c_k_pieces = [[nope, (n_up, 0)] for (nope,) in _head_pieces(C_HEADS, lanes_per_head=2 * LANES)]
        c_k = _prep(kv_srcs + [(c_kr, LANES, 0)], c_k_pieces, out_kind="rows", tm=tp,
                    gain=_pad_vec(c_k_norm[l], C_QK_PAD), norm="full", n_norm=C_QK_DIM, tabs=c_tabs,
                    rope_blocks=(1,), shifts=(96, 32), name="prep_c_k")
        c_v = _prep(kv_srcs, _head_pieces(C_HEADS, lanes_per_head=2 * LANES, lane0=LANES), out_kind="chunks",
                    tm=tp, name="prep_c_v")
        yc = _attention(c_q, c_k, c_v, tq=1024, name="attn_c")

        h = _matmul([ya, yb, yc], [w_out], res=h, out_dtype=jnp.float32, tm=1024, tn=512, layer=l,
                    name="out_proj")

        hn = _rmsnorm(h, ffn_norm[l])
        act = _matmul([hn], [w_gate, w_up], out_dtype=bf, tm=1024, tn=256, layer=l, name="ffn_gate_up")
        h = _matmul_kouter(act, w_down, h, layer=l, tm=1024, tn=256, tk=FFN_HIDDEN // 2, n_split=2,
                           name="ffn_down")

    return h.reshape(bsz, s, d)
```

```python
import functools
import math

import jax
import jax.numpy as jnp
import numpy as np
from jax import lax
from jax.experimental import pallas as pl
from jax.experimental.pallas import tpu as pltpu

D_MODEL = 4096
SEQ = 8192
DEPTH = 2
GRID_W = 64
EPS = 1e-6

A_HEAD_DIM = 128
A_HEADS = 12
A_KV_HEADS = 4
A_ROPE_THETA = 10000.0

B_QK_DIM = 64
B_V_DIM = 128
B_HEADS = 8
B_ROPE_DIM = 16
B_ROPE_THETA = 500000.0

C_V_DIM = 128
C_HEADS = 12
C_Q_LORA = 1024
C_KV_LORA = 512
C_NOPE = 128
C_ROPE = 64
C_QK_DIM = C_NOPE + C_ROPE
C_QK_PAD = 256
C_ROPE_THETA = 10000.0

N_IN = 7232
FFN_HIDDEN = 11008

LANES = 128
SUBLANES = 8
ATTN_ROW_BLOCK = 64
ATTN_KEY_CHUNK = 512
ATTN_CHUNKS_PER_TRIP = 8
ONES_ROWS = 16
VMEM_LIMIT_BYTES = 56 * 1024 * 1024

LOG2E = math.log2(math.e)

PROJ_BLOCK = 512
N_IN_MAIN = 14 * PROJ_BLOCK


def _head_pieces(n_heads, lanes_per_head=LANES, lane0=0, blocks=1):
    out = []
    for h in range(n_heads):
        head = []
        for b in range(blocks):
            col = lane0 + h * lanes_per_head + b * LANES
            head.append((col // PROJ_BLOCK, col % PROJ_BLOCK))
        out.append(head)
    return out


def _cparams(sem):
    return pltpu.CompilerParams(dimension_semantics=sem, vmem_limit_bytes=VMEM_LIMIT_BYTES)


def _rmsnorm_body(x_ref, g_ref, o_ref):
    x = x_ref[...]
    ms = jnp.mean(x * x, axis=-1, keepdims=True)
    o_ref[...] = (x * lax.rsqrt(ms + EPS) * g_ref[...]).astype(o_ref.dtype)


def _rmsnorm(x, g, tm=256):
    s, d = x.shape
    return pl.pallas_call(
        _rmsnorm_body,
        grid=(s // tm,),
        in_specs=[pl.BlockSpec((tm, d), lambda i: (i, 0)),
                  pl.BlockSpec((1, d), lambda i: (0, 0))],
        out_specs=pl.BlockSpec((tm, d), lambda i: (i, 0)),
        out_shape=jax.ShapeDtypeStruct((s, d), jnp.bfloat16),
        compiler_params=_cparams(("parallel",)),
        name="rmsnorm",
    )(x, g.reshape(1, d))


def _matmul_body(*refs, n_x, n_w, has_res, w_transposed):
    x_refs = refs[:n_x]
    w_refs = refs[n_x:n_x + n_w]
    pos = n_x + n_w
    res_ref = refs[pos] if has_res else None
    o_ref = refs[pos + int(has_res)]

    x = x_refs[0][...] if n_x == 1 else jnp.concatenate([r[...] for r in x_refs], axis=1)
    tiles = [w[...] if w.dtype == x.dtype else w[...].astype(x.dtype) for w in w_refs]
    contract = (((1,), (1,)), ((), ())) if w_transposed else (((1,), (0,)), ((), ()))
    prods = [lax.dot_general(x, w, contract, preferred_element_type=jnp.float32) for w in tiles]
    if n_w == 2:
        g, u = prods
        y = g * jax.nn.sigmoid(g) * u
    else:
        y = prods[0]
    if has_res:
        y = y + res_ref[...]
    o_ref[...] = y.astype(o_ref.dtype)


def _matmul(xs, ws, res=None, *, out_dtype, tm, tn, name, layer=None, w_transposed=False, n_cols=None):
    m = xs[0].shape[0]
    kdim = sum(x.shape[1] for x in xs)
    n = n_cols if n_cols is not None else ws[0].shape[-2 if w_transposed else -1]
    assert m % tm == 0 and n % tn == 0
    in_specs = [pl.BlockSpec((tm, x.shape[1]), lambda i, j: (i, 0)) for x in xs]
    w_block = (tn, kdim) if w_transposed else (kdim, tn)
    lead = () if layer is None else (None,)

    def w_index(i, j):
        pos = (j, 0) if w_transposed else (0, j)
        return pos if layer is None else (layer,) + pos

    in_specs += [pl.BlockSpec(lead + w_block, w_index) for _ in ws]
    args = [*xs, *ws]
    if res is not None:
        in_specs.append(pl.BlockSpec((tm, tn), lambda i, j: (i, j)))
        args.append(res)
    return pl.pallas_call(
        functools.partial(_matmul_body, n_x=len(xs), n_w=len(ws), has_res=res is not None,
                          w_transposed=w_transposed),
        grid=(m // tm, n // tn),
        in_specs=in_specs,
        out_specs=pl.BlockSpec((tm, tn), lambda i, j: (i, j)),
        out_shape=jax.ShapeDtypeStruct((m, n), out_dtype),
        compiler_params=_cparams(("parallel", "parallel")),
        name=name,
    )(*args)


def _tail_proj_body(x_ref, w_ref, o_ref):
    w = w_ref[...].astype(x_ref.dtype)
    w = jnp.concatenate([w, jnp.zeros_like(w)], axis=0)
    o_ref[...] = lax.dot_general(x_ref[...], w, (((1,), (1,)), ((), ())),
                                 preferred_element_type=jnp.float32)


def _tail_proj(x, w_t, *, layer, row0, tm, name):
    m, kdim = x.shape
    rows = w_t.shape[1] - row0
    assert rows * 2 == LANES and row0 % rows == 0
    return pl.pallas_call(
        _tail_proj_body,
        grid=(m // tm,),
        in_specs=[pl.BlockSpec((tm, kdim), lambda i: (i, 0)),
                  pl.BlockSpec((None, rows, kdim), lambda i: (layer, row0 // rows, 0))],
        out_specs=pl.BlockSpec((tm, LANES), lambda i: (i, 0)),
        out_shape=jax.ShapeDtypeStruct((m, LANES), jnp.float32),
        compiler_params=_cparams(("parallel",)),
        name=name,
    )(x, w_t)


def _matmul_kouter_body(x_ref, w_ref, res_ref, o_ref, acc_ref, *, nk):
    k = pl.program_id(2)
    j = pl.program_id(3)
    prod = jnp.dot(x_ref[...], w_ref[...].astype(x_ref.dtype), preferred_element_type=jnp.float32)

    @pl.when(k == 0)
    def _():
        acc_ref[j] = prod

    @pl.when(jnp.logical_and(k > 0, k < nk - 1))
    def _():
        acc_ref[j] += prod

    @pl.when(k == nk - 1)
    def _():
        o_ref[...] = (acc_ref[j] + prod + res_ref[...]).astype(o_ref.dtype)


def _matmul_kouter(x, w, res, *, layer, tm, tn, tk, n_split, name):
    m, kdim = x.shape
    n = w.shape[-1]
    nk = kdim // tk
    nj = n // (tn * n_split)
    assert m % tm == 0 and n % (tn * n_split) == 0 and kdim % tk == 0 and nk >= 2

    def out_index(hf, i, k, j):
        return i, hf * nj + jnp.where(k == nk - 1, j, 0)

    return pl.pallas_call(
        functools.partial(_matmul_kouter_body, nk=nk),
        grid=(n_split, m // tm, nk, nj),
        in_specs=[pl.BlockSpec((tm, tk), lambda hf, i, k, j: (i, k)),
                  pl.BlockSpec((None, tk, tn), lambda hf, i, k, j: (layer, k, hf * nj + j)),
                  pl.BlockSpec((tm, tn), out_index)],
        out_specs=pl.BlockSpec((tm, tn), out_index),
        out_shape=jax.ShapeDtypeStruct((m, n), res.dtype),
        scratch_shapes=[pltpu.VMEM((nj, tm, tn), jnp.float32)],
        compiler_params=_cparams(("parallel", "parallel", "arbitrary", "arbitrary")),
        name=name,
    )(x, w, res)


def _prep_head(x, g_ref, tabs, *, norm, n_norm, rope_blocks, shifts, scale, transpose, ones_rows):
    width = x.shape[-1]
    if norm == "full":
        ss = jnp.sum(x * x, axis=-1, keepdims=True)
        x = x * lax.rsqrt(ss * (1.0 / n_norm) + EPS) * g_ref[...]
    elif norm == "half":
        lane = lax.broadcasted_iota(jnp.int32, x.shape, 1)
        lo = lane < (LANES // 2)
        x2 = x * x
        ss_lo = jnp.sum(jnp.where(lo, x2, 0.0), axis=-1, keepdims=True)
        ss_hi = jnp.sum(jnp.where(lo, 0.0, x2), axis=-1, keepdims=True)
        ss = jnp.where(lo, ss_lo, ss_hi)
        x = x * lax.rsqrt(ss * (1.0 / n_norm) + EPS) * g_ref[...]

    if rope_blocks:
        c_ref, s1_ref, s2_ref = tabs
        blocks = []
        for b in range(width // LANES):
            xb = x[:, b * LANES:(b + 1) * LANES]
            if b in rope_blocks:
                sl = slice(b * LANES, (b + 1) * LANES)
                xb = (xb * c_ref[:, sl]
                      + pltpu.roll(xb, shifts[0], 1) * s1_ref[:, sl]
                      + pltpu.roll(xb, shifts[1], 1) * s2_ref[:, sl])
            blocks.append(xb)
        x = blocks[0] if len(blocks) == 1 else jnp.concatenate(blocks, axis=-1)

    if scale != 1.0:
        x = x * scale
    if transpose:
        x = x.T
    if ones_rows:
        x = jnp.concatenate([x, jnp.ones((ones_rows, x.shape[1]), x.dtype)], axis=0)
    return x


def _prep_body(*refs, nx, pieces, has_gain, has_tabs, **head_kw):
    x_refs = refs[:nx]
    pos = nx
    g_ref = None
    if has_gain:
        g_ref = refs[pos]
        pos += 1
    tabs = None
    if has_tabs:
        tabs = refs[pos:pos + 3]
        pos += 3
    o_ref = refs[pos]
    for h, head_pieces in enumerate(pieces):
        cols = [x_refs[src][:, off:off + LANES] for src, off in head_pieces]
        x = cols[0] if len(cols) == 1 else jnp.concatenate(cols, axis=-1)
        y = _prep_head(x, g_ref, tabs, **head_kw)
        o_ref[h] = y.astype(o_ref.dtype).reshape(o_ref.shape[1:])


def _prep(srcs, pieces, *, out_kind, tm, gain=None, norm=None, n_norm=None, tabs=None,
          rope_blocks=(), shifts=(0, 0), scale=1.0, name="prep"):
    s = srcs[0][0].shape[0]
    n_heads = len(pieces)
    width = LANES * len(pieces[0])
    in_specs, args = [], []
    for arr, bw, idx in srcs:
        in_specs.append(pl.BlockSpec((tm, bw), functools.partial(lambda i, idx: (i, idx), idx=idx)))
        args.append(arr)
    if norm is not None:
        in_specs.append(pl.BlockSpec((1, width), lambda i: (0, 0)))
        args.append(gain.reshape(1, width))
    if rope_blocks:
        for t in tabs:
            in_specs.append(pl.BlockSpec((tm, width), lambda i: (i, 0)))
            args.append(t)
    if out_kind == "rows":
        out_shape = (n_heads, s, width)
        out_spec = pl.BlockSpec((n_heads, tm, width), lambda i: (0, i, 0))
    elif out_kind == "cols":
        out_shape = (n_heads, s // tm, width, tm)
        out_spec = pl.BlockSpec((n_heads, 1, width, tm), lambda i: (0, i, 0, 0))
    else:
        out_shape = (n_heads, s // tm, width + ONES_ROWS, tm)
        out_spec = pl.BlockSpec((n_heads, 1, width + ONES_ROWS, tm), lambda i: (0, i, 0, 0))
    return pl.pallas_call(
        functools.partial(_prep_body, nx=len(srcs), pieces=tuple(tuple(p) for p in pieces),
                          has_gain=norm is not None, has_tabs=bool(rope_blocks),
                          norm=norm, n_norm=n_norm, rope_blocks=tuple(rope_blocks), shifts=shifts,
                          scale=scale, transpose=out_kind != "rows",
                          ones_rows=ONES_ROWS if out_kind == "chunks" else 0),
        grid=(s // tm,),
        in_specs=in_specs,
        out_specs=out_spec,
        out_shape=jax.ShapeDtypeStruct(out_shape, jnp.bfloat16),
        compiler_params=_cparams(("parallel",)),
        name=name,
    )(*args)


def _attn_body(*refs, nchunk, tk, tq, diff, lambda_init):
    if diff:
        qT_ref, k_ref, vT_ref, lam_ref, sub_ref, o_ref, m_ref, acc_ref, s_ref, p_ref, qq_ref = refs
    else:
        qT_ref, k_ref, vT_ref, o_ref, m_ref, acc_ref, s_ref, p_ref, qq_ref = refs

    i = pl.program_id(1)
    nqb, _, nq = qq_ref.shape
    tiles_per_block = tq // qT_ref.shape[3]
    chunk_bits = nchunk.bit_length() - 1
    assert nchunk == 1 << chunk_bits

    def scores(step):
        chunk = step & (nchunk - 1)
        block = jnp.minimum(i + (step >> chunk_bits), nqb - 1)
        start = pl.multiple_of(chunk * tk, tk)
        return jnp.dot(k_ref[0, pl.ds(start, tk), :], qq_ref[block],
                       preferred_element_type=jnp.float32)

    @pl.when(i == 0)
    def _():
        for b in range(nqb):
            tiles = [qT_ref[0, b * tiles_per_block + c] for c in range(tiles_per_block)]
            q = tiles[0] if len(tiles) == 1 else jnp.concatenate(tiles, axis=1)
            if diff:
                row = lax.broadcasted_iota(jnp.int32, q.shape, 0)
                first = row < B_QK_DIM
                zero = jnp.zeros_like(q)
                q = jnp.concatenate([jnp.where(first, q, zero), jnp.where(first, zero, q)], axis=1)
            qq_ref[b] = q
        s_ref[0] = scores(0)

    m_ref[...] = jnp.full(m_ref.shape, -jnp.inf, jnp.float32)
    acc_ref[...] = jnp.zeros(acc_ref.shape, jnp.float32)

    rb = ATTN_ROW_BLOCK
    groups = rb // SUBLANES

    def softmax(slot):
        def block(r):
            return s_ref[slot, r * rb:(r + 1) * rb, :].reshape(groups, SUBLANES, nq)

        mx = jnp.max(block(0), axis=0)
        for r in range(1, tk // rb):
            mx = jnp.maximum(mx, jnp.max(block(r), axis=0))
        m_old = m_ref[...]
        m_new = jnp.maximum(m_old, jnp.max(mx, axis=0, keepdims=True))
        alpha = jnp.exp2(m_old - m_new)
        m_rows = jnp.broadcast_to(m_new, (SUBLANES, nq))[None]
        for r in range(tk // rb):
            p = jnp.exp2(block(r) - m_rows).reshape(rb, nq)
            p_ref[slot, r * rb:(r + 1) * rb, :] = p.astype(jnp.bfloat16)
        m_ref[...] = m_new
        return alpha

    def pv_update(j, slot, alpha):
        pv = jnp.dot(vT_ref[0, j], p_ref[slot], preferred_element_type=jnp.float32)
        acc_ref[...] = acc_ref[...] * alpha + pv

    def trip(t, carry):
        for u in range(ATTN_CHUNKS_PER_TRIP):
            j = t * ATTN_CHUNKS_PER_TRIP + u
            s_ref[(u + 1) % 2] = scores(j + 1)
            alpha = softmax(u % 2)
            pv_update(j, u % 2, alpha)
        return carry

    lax.fori_loop(0, nchunk // ATTN_CHUNKS_PER_TRIP, trip, 0)

    dv = o_ref.shape[1]
    o = acc_ref[:dv, :] / acc_ref[dv:dv + 1, :]
    if diff:
        lp = lam_ref[...]
        lam = (jnp.exp(jnp.sum(lp[0:1] * lp[1:2], axis=-1, keepdims=True))
               - jnp.exp(jnp.sum(lp[2:3] * lp[3:4], axis=-1, keepdims=True))
               + lambda_init)
        o = o[:, :tq] - lam * o[:, tq:]
        ms = jnp.mean(o * o, axis=0, keepdims=True)
        o = o * lax.rsqrt(ms + EPS) * sub_ref[...] * (1.0 - lambda_init)
    o_ref[...] = o.T.astype(o_ref.dtype)


def _attention(qT, k, vT, *, tq, diff=False, lam_p=None, subln=None, lambda_init=0.0, name):
    n_heads, n_qtiles, dq, q_tile = qT.shape
    s = n_qtiles * q_tile
    n_kv = k.shape[0]
    group = n_heads // n_kv
    _, nchunk, dv_ext, tk = vT.shape
    dv = dv_ext - ONES_ROWS
    nq = 2 * tq if diff else tq
    assert tq % q_tile == 0 and ATTN_CHUNKS_PER_TRIP % 2 == 0 and nchunk % ATTN_CHUNKS_PER_TRIP == 0
    in_specs = [pl.BlockSpec((1, n_qtiles, dq, q_tile), lambda h, i: (h, 0, 0, 0)),
                pl.BlockSpec((1, s, dq), lambda h, i: (h // group, 0, 0)),
                pl.BlockSpec((1, nchunk, dv_ext, tk), lambda h, i: (h // group, 0, 0, 0))]
    args = [qT, k, vT]
    if diff:
        in_specs += [pl.BlockSpec(lam_p.shape, lambda h, i: (0, 0)),
                     pl.BlockSpec((dv, 1), lambda h, i: (0, 0))]
        args += [lam_p, subln.reshape(dv, 1)]
    return pl.pallas_call(
        functools.partial(_attn_body, nchunk=nchunk, tk=tk, tq=tq, diff=diff, lambda_init=lambda_init),
        grid=(n_heads, s // tq),
        in_specs=in_specs,
        out_specs=pl.BlockSpec((tq, dv), lambda h, i: (i, h)),
        out_shape=jax.ShapeDtypeStruct((s, n_heads * dv), jnp.bfloat16),
        scratch_shapes=[pltpu.VMEM((1, nq), jnp.float32),
                        pltpu.VMEM((dv_ext, nq), jnp.float32),
                        pltpu.VMEM((2, tk, nq), jnp.float32),
                        pltpu.VMEM((2, tk, nq), jnp.bfloat16),
                        pltpu.VMEM((s // tq, dq, nq), jnp.bfloat16)],
        compiler_params=_cparams(("arbitrary", "arbitrary")),
        name=name,
    )(*args)


def _rope_cos_sin(pos, dim, theta):
    inv = theta ** (-jnp.arange(0, dim, 2, dtype=jnp.float32) / dim)
    ang = pos.astype(jnp.float32)[:, None] * inv[None, :]
    return jnp.cos(ang), jnp.sin(ang)


def _rotate_half_tables(cos, sin):
    zero = jnp.zeros_like(sin)
    return (jnp.concatenate([cos, cos], axis=-1),
            jnp.concatenate([-sin, zero], axis=-1),
            jnp.concatenate([zero, sin], axis=-1))


def _rope_tables(s):
    t = jnp.arange(s, dtype=jnp.int32)
    row = t // GRID_W
    col = t % GRID_W
    half = A_HEAD_DIM // 2
    a_row = _rotate_half_tables(*_rope_cos_sin(row, half, A_ROPE_THETA))
    a_col = _rotate_half_tables(*_rope_cos_sin(col, half, A_ROPE_THETA))
    a_tabs = tuple(jnp.concatenate([r, c], axis=-1) for r, c in zip(a_row, a_col))

    b_rot = _rotate_half_tables(*_rope_cos_sin(t, B_ROPE_DIM, B_ROPE_THETA))
    rest = B_QK_DIM - B_ROPE_DIM
    fill = (jnp.ones((s, rest), jnp.float32), jnp.zeros((s, rest), jnp.float32),
            jnp.zeros((s, rest), jnp.float32))
    b_tabs = tuple(jnp.tile(jnp.concatenate([r, f], axis=-1), (1, 2)) for r, f in zip(b_rot, fill))

    c_rot = _rotate_half_tables(*_rope_cos_sin(t, C_ROPE, C_ROPE_THETA))
    ones = jnp.ones((s, C_NOPE), jnp.float32)
    zeros = jnp.zeros((s, C_NOPE), jnp.float32)
    pad = jnp.zeros((s, C_QK_PAD - C_QK_DIM), jnp.float32)
    c_tabs = (jnp.concatenate([ones, c_rot[0], pad], axis=-1),
              jnp.concatenate([zeros, c_rot[1], pad], axis=-1),
              jnp.concatenate([zeros, c_rot[2], pad], axis=-1))
    return a_tabs, b_tabs, c_tabs


def _pad_vec(g, n):
    return jnp.pad(g, (0, n - g.shape[0]))


def kernel(x, attn_norm, w_in, a_q_norm, a_k_norm, b_q_norm, b_k_norm, b_lambda, b_subln,
           c_q_a_norm, c_kv_a_norm, c_w_q_up, c_w_kv_up, c_q_norm, c_k_norm, w_out,
           ffn_norm, w_gate, w_up, w_down):
    bsz, s, d = x.shape
    assert (bsz, s, d) == (1, SEQ, D_MODEL)
    bf = jnp.bfloat16
    a_tabs, b_tabs, c_tabs = _rope_tables(s)
    tp = 512
    h = x.reshape(s, d)
    w_in_t = jnp.swapaxes(w_in, 1, 2)

    for l in range(DEPTH):
        lambda_init = 0.8 - 0.6 * math.exp(-0.3 * l)
        wq_up = c_w_q_up[l].astype(bf).reshape(C_Q_LORA, C_HEADS, C_QK_DIM)
        wq_up = jnp.pad(wq_up, ((0, 0), (0, 0), (0, C_QK_PAD - C_QK_DIM))).reshape(C_Q_LORA, C_HEADS * C_QK_PAD)
        wkv_up = c_w_kv_up[l].astype(bf)

        xn = _rmsnorm(h, attn_norm[l])
        proj = _matmul([xn], [w_in_t], out_dtype=jnp.float32, tm=1024, tn=PROJ_BLOCK, layer=l,
                       w_transposed=True, n_cols=N_IN_MAIN, name="in_proj")
        c_kr = _tail_proj(xn, w_in_t, layer=l, row0=N_IN_MAIN, tm=1024, name="in_proj_kr")

        def proj_blocks(first, count):
            return [(proj, PROJ_BLOCK, first + b) for b in range(count)]

        a_scale = LOG2E / math.sqrt(A_HEAD_DIM)
        a_q = _prep(proj_blocks(0, 3), _head_pieces(A_HEADS), out_kind="cols", tm=tp,
                    gain=a_q_norm[l], norm="full", n_norm=A_HEAD_DIM, tabs=a_tabs, rope_blocks=(0,),
                    shifts=(96, 32), scale=a_scale, name="prep_a_q")
        a_k = _prep(proj_blocks(3, 1), _head_pieces(A_KV_HEADS), out_kind="rows", tm=tp,
                    gain=a_k_norm[l], norm="full", n_norm=A_HEAD_DIM, tabs=a_tabs, rope_blocks=(0,),
                    shifts=(96, 32), name="prep_a_k")
        a_v = _prep(proj_blocks(4, 1), _head_pieces(A_KV_HEADS), out_kind="chunks", tm=ATTN_KEY_CHUNK,
                    name="prep_a_v")
        ya = _attention(a_q, a_k, a_v, tq=1024, name="attn_a")

        b_scale = LOG2E / math.sqrt(B_QK_DIM)
        b_q = _prep(proj_blocks(5, 2), _head_pieces(B_HEADS), out_kind="cols", tm=tp,
                    gain=jnp.tile(b_q_norm[l], 2), norm="half", n_norm=B_QK_DIM, tabs=b_tabs,
                    rope_blocks=(0,), shifts=(120, 8), scale=b_scale, name="prep_b_q")
        b_k = _prep(proj_blocks(7, 2), _head_pieces(B_HEADS), out_kind="rows", tm=tp,
                    gain=jnp.tile(b_k_norm[l], 2), norm="half", n_norm=B_QK_DIM, tabs=b_tabs,
                    rope_blocks=(0,), shifts=(120, 8), name="prep_b_k")
        b_v = _prep(proj_blocks(9, 2), _head_pieces(B_HEADS), out_kind="chunks", tm=ATTN_KEY_CHUNK,
                    name="prep_b_v")
        yb = _attention(b_q, b_k, b_v, tq=512, diff=True, lam_p=b_lambda[l], subln=b_subln[l],
                        lambda_init=lambda_init, name="attn_b")

        cq_lat = _prep(proj_blocks(11, 2), _head_pieces(1, blocks=C_Q_LORA // LANES), out_kind="rows", tm=tp,
                       gain=c_q_a_norm[l], norm="full", n_norm=C_Q_LORA, name="norm_c_q")[0]
        ckv_lat = _prep(proj_blocks(13, 1), _head_pieces(1, blocks=C_KV_LORA // LANES), out_kind="rows", tm=tp,
                        gain=c_kv_a_norm[l], norm="full", n_norm=C_KV_LORA, name="norm_c_kv")[0]
        c_q_raw = _matmul([cq_lat], [wq_up], out_dtype=jnp.float32, tm=1024, tn=512, name="c_q_up")
        c_kv_raw = _matmul([ckv_lat], [wkv_up], out_dtype=jnp.float32, tm=1024, tn=512, name="c_kv_up")
        c_scale = LOG2E / math.sqrt(C_QK_DIM)
        n_up = C_HEADS * C_QK_PAD // PROJ_BLOCK
        c_q = _prep([(c_q_raw, PROJ_BLOCK, b) for b in range(n_up)],
                    _head_pieces(C_HEADS, lanes_per_head=C_QK_PAD, blocks=2), out_kind="cols", tm=tp,
                    gain=_pad_vec(c_q_norm[l], C_QK_PAD), norm="full", n_norm=C_QK_DIM, tabs=c_tabs,
                    rope_blocks=(1,), shifts=(96, 32), scale=c_scale, name="prep_c_q")
        kv_srcs = [(c_kv_raw, PROJ_BLOCK, b) for b in range(n_up)]
        c_k_pieces = [[nope, (n_up, 0)] for (nope,) in _head_pieces(C_HEADS, lanes_per_head=2 * LANES)]
        c_k = _prep(kv_srcs + [(c_kr, LANES, 0)], c_k_pieces, out_kind="rows", tm=tp,
                    gain=_pad_vec(c_k_norm[l], C_QK_PAD), norm="full", n_norm=C_QK_DIM, tabs=c_tabs,
                    rope_blocks=(1,), shifts=(96, 32), name="prep_c_k")
        c_v = _prep(kv_srcs, _head_pieces(C_HEADS, lanes_per_head=2 * LANES, lane0=LANES), out_kind="chunks",
                    tm=ATTN_KEY_CHUNK, name="prep_c_v")
        yc = _attention(c_q, c_k, c_v, tq=1024, name="attn_c")

        h = _matmul([ya, yb, yc], [w_out], res=h, out_dtype=jnp.float32, tm=1024, tn=512, layer=l,
                    name="out_proj")

        hn = _rmsnorm(h, ffn_norm[l])
        act = _matmul([hn], [w_gate, w_up], out_dtype=bf, tm=1024, tn=256, layer=l, name="ffn_gate_up")
        h = _matmul_kouter(act, w_down, h, layer=l, tm=1024, tn=256, tk=FFN_HIDDEN // 2, n_split=2,
                           name="ffn_down")

    return h.reshape(bsz, s, d)
```

```python
import functools
import math

import jax
import jax.numpy as jnp
import numpy as np
from jax import lax
from jax.experimental import pallas as pl
from jax.experimental.pallas import tpu as pltpu

D_MODEL = 4096
SEQ = 8192
DEPTH = 2
GRID_W = 64
EPS = 1e-6

A_HEAD_DIM = 128
A_HEADS = 12
A_KV_HEADS = 4
A_ROPE_THETA = 10000.0

B_QK_DIM = 64
B_V_DIM = 128
B_HEADS = 8
B_ROPE_DIM = 16
B_ROPE_THETA = 500000.0

C_V_DIM = 128
C_HEADS = 12
C_Q_LORA = 1024
C_KV_LORA = 512
C_NOPE = 128
C_ROPE = 64
C_QK_DIM = C_NOPE + C_ROPE
C_QK_PAD = 256
C_ROPE_THETA = 10000.0

N_IN = 7232
FFN_HIDDEN = 11008

LANES = 128
SUBLANES = 8
ATTN_ROW_BLOCK = 64
ATTN_KEY_CHUNK = 512
ATTN_CHUNKS_PER_TRIP = 8
ATTN_MIN_DENOM = 2.0 ** -60
BOUND_SLACK = 1.01
ONES_ROWS = 16
VMEM_LIMIT_BYTES = 56 * 1024 * 1024

LOG2E = math.log2(math.e)

PROJ_BLOCK = 512
N_IN_MAIN = 14 * PROJ_BLOCK


def _head_pieces(n_heads, lanes_per_head=LANES, lane0=0, blocks=1):
    out = []
    for h in range(n_heads):
        head = []
        for b in range(blocks):
            col = lane0 + h * lanes_per_head + b * LANES
            head.append((col // PROJ_BLOCK, col % PROJ_BLOCK))
        out.append(head)
    return out


def _cparams(sem):
    return pltpu.CompilerParams(dimension_semantics=sem, vmem_limit_bytes=VMEM_LIMIT_BYTES)


def _rmsnorm_body(x_ref, g_ref, o_ref):
    x = x_ref[...]
    ms = jnp.mean(x * x, axis=-1, keepdims=True)
    o_ref[...] = (x * lax.rsqrt(ms + EPS) * g_ref[...]).astype(o_ref.dtype)


def _rmsnorm(x, g, tm=256):
    s, d = x.shape
    return pl.pallas_call(
        _rmsnorm_body,
        grid=(s // tm,),
        in_specs=[pl.BlockSpec((tm, d), lambda i: (i, 0)),
                  pl.BlockSpec((1, d), lambda i: (0, 0))],
        out_specs=pl.BlockSpec((tm, d), lambda i: (i, 0)),
        out_shape=jax.ShapeDtypeStruct((s, d), jnp.bfloat16),
        compiler_params=_cparams(("parallel",)),
        name="rmsnorm",
    )(x, g.reshape(1, d))


def _matmul_body(*refs, n_x, n_w, has_res, w_transposed):
    x_refs = refs[:n_x]
    w_refs = refs[n_x:n_x + n_w]
    pos = n_x + n_w
    res_ref = refs[pos] if has_res else None
    o_ref = refs[pos + int(has_res)]

    x = x_refs[0][...] if n_x == 1 else jnp.concatenate([r[...] for r in x_refs], axis=1)
    tiles = [w[...] if w.dtype == x.dtype else w[...].astype(x.dtype) for w in w_refs]
    contract = (((1,), (1,)), ((), ())) if w_transposed else (((1,), (0,)), ((), ()))
    prods = [lax.dot_general(x, w, contract, preferred_element_type=jnp.float32) for w in tiles]
    if n_w == 2:
        g, u = prods
        y = g * jax.nn.sigmoid(g) * u
    else:
        y = prods[0]
    if has_res:
        y = y + res_ref[...]
    o_ref[...] = y.astype(o_ref.dtype)


def _matmul(xs, ws, res=None, *, out_dtype, tm, tn, name, layer=None, w_transposed=False, n_cols=None):
    m = xs[0].shape[0]
    kdim = sum(x.shape[1] for x in xs)
    n = n_cols if n_cols is not None else ws[0].shape[-2 if w_transposed else -1]
    assert m % tm == 0 and n % tn == 0
    in_specs = [pl.BlockSpec((tm, x.shape[1]), lambda i, j: (i, 0)) for x in xs]
    w_block = (tn, kdim) if w_transposed else (kdim, tn)
    lead = () if layer is None else (None,)

    def w_index(i, j):
        pos = (j, 0) if w_transposed else (0, j)
        return pos if layer is None else (layer,) + pos

    in_specs += [pl.BlockSpec(lead + w_block, w_index) for _ in ws]
    args = [*xs, *ws]
    if res is not None:
        in_specs.append(pl.BlockSpec((tm, tn), lambda i, j: (i, j)))
        args.append(res)
    return pl.pallas_call(
        functools.partial(_matmul_body, n_x=len(xs), n_w=len(ws), has_res=res is not None,
                          w_transposed=w_transposed),
        grid=(m // tm, n // tn),
        in_specs=in_specs,
        out_specs=pl.BlockSpec((tm, tn), lambda i, j: (i, j)),
        out_shape=jax.ShapeDtypeStruct((m, n), out_dtype),
        compiler_params=_cparams(("parallel", "parallel")),
        name=name,
    )(*args)


def _tail_proj_body(x_ref, w_ref, o_ref):
    w = w_ref[...].astype(x_ref.dtype)
    w = jnp.concatenate([w, jnp.zeros_like(w)], axis=0)
    o_ref[...] = lax.dot_general(x_ref[...], w, (((1,), (1,)), ((), ())),
                                 preferred_element_type=jnp.float32)


def _tail_proj(x, w_t, *, layer, row0, tm, name):
    m, kdim = x.shape
    rows = w_t.shape[1] - row0
    assert rows * 2 == LANES and row0 % rows == 0
    return pl.pallas_call(
        _tail_proj_body,
        grid=(m // tm,),
        in_specs=[pl.BlockSpec((tm, kdim), lambda i: (i, 0)),
                  pl.BlockSpec((None, rows, kdim), lambda i: (layer, row0 // rows, 0))],
        out_specs=pl.BlockSpec((tm, LANES), lambda i: (i, 0)),
        out_shape=jax.ShapeDtypeStruct((m, LANES), jnp.float32),
        compiler_params=_cparams(("parallel",)),
        name=name,
    )(x, w_t)


def _matmul_kouter_body(x_ref, w_ref, res_ref, o_ref, acc_ref, *, nk):
    k = pl.program_id(2)
    j = pl.program_id(3)
    prod = jnp.dot(x_ref[...], w_ref[...].astype(x_ref.dtype), preferred_element_type=jnp.float32)

    @pl.when(k == 0)
    def _():
        acc_ref[j] = prod

    @pl.when(jnp.logical_and(k > 0, k < nk - 1))
    def _():
        acc_ref[j] += prod

    @pl.when(k == nk - 1)
    def _():
        o_ref[...] = (acc_ref[j] + prod + res_ref[...]).astype(o_ref.dtype)


def _matmul_kouter(x, w, res, *, layer, tm, tn, tk, n_split, name):
    m, kdim = x.shape
    n = w.shape[-1]
    nk = kdim // tk
    nj = n // (tn * n_split)
    assert m % tm == 0 and n % (tn * n_split) == 0 and kdim % tk == 0 and nk >= 2

    def out_index(hf, i, k, j):
        return i, hf * nj + jnp.where(k == nk - 1, j, 0)

    return pl.pallas_call(
        functools.partial(_matmul_kouter_body, nk=nk),
        grid=(n_split, m // tm, nk, nj),
        in_specs=[pl.BlockSpec((tm, tk), lambda hf, i, k, j: (i, k)),
                  pl.BlockSpec((None, tk, tn), lambda hf, i, k, j: (layer, k, hf * nj + j)),
                  pl.BlockSpec((tm, tn), out_index)],
        out_specs=pl.BlockSpec((tm, tn), out_index),
        out_shape=jax.ShapeDtypeStruct((m, n), res.dtype),
        scratch_shapes=[pltpu.VMEM((nj, tm, tn), jnp.float32)],
        compiler_params=_cparams(("parallel", "parallel", "arbitrary", "arbitrary")),
        name=name,
    )(x, w, res)


def _prep_head(x, g_ref, tabs, *, norm, n_norm, rope_blocks, shifts, scale, transpose, ones_rows):
    width = x.shape[-1]
    if norm == "full":
        ss = jnp.sum(x * x, axis=-1, keepdims=True)
        x = x * lax.rsqrt(ss * (1.0 / n_norm) + EPS) * g_ref[...]
    elif norm == "half":
        lane = lax.broadcasted_iota(jnp.int32, x.shape, 1)
        lo = lane < (LANES // 2)
        x2 = x * x
        ss_lo = jnp.sum(jnp.where(lo, x2, 0.0), axis=-1, keepdims=True)
        ss_hi = jnp.sum(jnp.where(lo, 0.0, x2), axis=-1, keepdims=True)
        ss = jnp.where(lo, ss_lo, ss_hi)
        x = x * lax.rsqrt(ss * (1.0 / n_norm) + EPS) * g_ref[...]

    if rope_blocks:
        c_ref, s1_ref, s2_ref = tabs
        blocks = []
        for b in range(width // LANES):
            xb = x[:, b * LANES:(b + 1) * LANES]
            if b in rope_blocks:
                sl = slice(b * LANES, (b + 1) * LANES)
                xb = (xb * c_ref[:, sl]
                      + pltpu.roll(xb, shifts[0], 1) * s1_ref[:, sl]
                      + pltpu.roll(xb, shifts[1], 1) * s2_ref[:, sl])
            blocks.append(xb)
        x = blocks[0] if len(blocks) == 1 else jnp.concatenate(blocks, axis=-1)

    if scale != 1.0:
        x = x * scale
    if transpose:
        x = x.T
    if ones_rows:
        x = jnp.concatenate([x, jnp.ones((ones_rows, x.shape[1]), x.dtype)], axis=0)
    return x


def _prep_body(*refs, nx, pieces, has_gain, has_tabs, **head_kw):
    x_refs = refs[:nx]
    pos = nx
    g_ref = None
    if has_gain:
        g_ref = refs[pos]
        pos += 1
    tabs = None
    if has_tabs:
        tabs = refs[pos:pos + 3]
        pos += 3
    o_ref = refs[pos]
    for h, head_pieces in enumerate(pieces):
        cols = [x_refs[src][:, off:off + LANES] for src, off in head_pieces]
        x = cols[0] if len(cols) == 1 else jnp.concatenate(cols, axis=-1)
        y = _prep_head(x, g_ref, tabs, **head_kw)
        o_ref[h] = y.astype(o_ref.dtype).reshape(o_ref.shape[1:])


def _prep(srcs, pieces, *, out_kind, tm, gain=None, norm=None, n_norm=None, tabs=None,
          rope_blocks=(), shifts=(0, 0), scale=1.0, name="prep"):
    s = srcs[0][0].shape[0]
    n_heads = len(pieces)
    width = LANES * len(pieces[0])
    in_specs, args = [], []
    for arr, bw, idx in srcs:
        in_specs.append(pl.BlockSpec((tm, bw), functools.partial(lambda i, idx: (i, idx), idx=idx)))
        args.append(arr)
    if norm is not None:
        in_specs.append(pl.BlockSpec((1, width), lambda i: (0, 0)))
        args.append(gain.reshape(1, width))
    if rope_blocks:
        for t in tabs:
            in_specs.append(pl.BlockSpec((tm, width), lambda i: (i, 0)))
            args.append(t)
    if out_kind == "rows":
        out_shape = (n_heads, s, width)
        out_spec = pl.BlockSpec((n_heads, tm, width), lambda i: (0, i, 0))
    elif out_kind == "cols":
        out_shape = (n_heads, s // tm, width, tm)
        out_spec = pl.BlockSpec((n_heads, 1, width, tm), lambda i: (0, i, 0, 0))
    else:
        out_shape = (n_heads, s // tm, width + ONES_ROWS, tm)
        out_spec = pl.BlockSpec((n_heads, 1, width + ONES_ROWS, tm), lambda i: (0, i, 0, 0))
    return pl.pallas_call(
        functools.partial(_prep_body, nx=len(srcs), pieces=tuple(tuple(p) for p in pieces),
                          has_gain=norm is not None, has_tabs=bool(rope_blocks),
                          norm=norm, n_norm=n_norm, rope_blocks=tuple(rope_blocks), shifts=shifts,
                          scale=scale, transpose=out_kind != "rows",
                          ones_rows=ONES_ROWS if out_kind == "chunks" else 0),
        grid=(s // tm,),
        in_specs=in_specs,
        out_specs=out_spec,
        out_shape=jax.ShapeDtypeStruct(out_shape, jnp.bfloat16),
        compiler_params=_cparams(("parallel",)),
        name=name,
    )(*args)


def _attn_body(*refs, nchunk, tk, tq, diff, lambda_init):
    if diff:
        (qT_ref, k_ref, vT_ref, shift_ref, lam_ref, sub_ref, o_ref,
         acc_ref, p_ref, qq_ref, m_ref, s_ref, ps_ref) = refs
    else:
        qT_ref, k_ref, vT_ref, shift_ref, o_ref, acc_ref, p_ref, qq_ref, m_ref, s_ref, ps_ref = refs

    i = pl.program_id(1)
    nqb, _, nq = qq_ref.shape
    dv = o_ref.shape[1]
    tiles_per_block = tq // qT_ref.shape[3]
    chunk_bits = nchunk.bit_length() - 1
    assert nchunk == 1 << chunk_bits

    def scores(step):
        chunk = step & (nchunk - 1)
        block = jnp.minimum(i + (step >> chunk_bits), nqb - 1)
        start = pl.multiple_of(chunk * tk, tk)
        return jnp.dot(k_ref[0, pl.ds(start, tk), :], qq_ref[block],
                       preferred_element_type=jnp.float32)

    def probs(step):
        return jnp.exp2(scores(step) - shift_ref[...]).astype(jnp.bfloat16)

    @pl.when(i == 0)
    def _():
        for b in range(nqb):
            tiles = [qT_ref[0, b * tiles_per_block + c] for c in range(tiles_per_block)]
            q = tiles[0] if len(tiles) == 1 else jnp.concatenate(tiles, axis=1)
            if diff:
                row = lax.broadcasted_iota(jnp.int32, q.shape, 0)
                first = row < B_QK_DIM
                zero = jnp.zeros_like(q)
                q = jnp.concatenate([jnp.where(first, q, zero), jnp.where(first, zero, q)], axis=1)
            qq_ref[b] = q
        p_ref[0] = probs(0)

    acc_ref[...] = jnp.zeros(acc_ref.shape, jnp.float32)

    def trip(t, carry):
        for u in range(ATTN_CHUNKS_PER_TRIP):
            j = t * ATTN_CHUNKS_PER_TRIP + u
            p_ref[(u + 1) % 2] = probs(j + 1)
            acc_ref[...] += jnp.dot(vT_ref[0, j], p_ref[u % 2], preferred_element_type=jnp.float32)
        return carry

    lax.fori_loop(0, nchunk // ATTN_CHUNKS_PER_TRIP, trip, 0)

    healthy = jnp.min(acc_ref[dv:dv + 1, :]) >= ATTN_MIN_DENOM

    @pl.when(jnp.logical_not(healthy))
    def _():
        rb = ATTN_ROW_BLOCK
        groups = rb // SUBLANES
        m_ref[...] = jnp.full(m_ref.shape, -jnp.inf, jnp.float32)
        acc_ref[...] = jnp.zeros(acc_ref.shape, jnp.float32)

        def block(r):
            return s_ref[r * rb:(r + 1) * rb, :].reshape(groups, SUBLANES, nq)

        def chunk(j, carry):
            s_ref[...] = scores(j)
            mx = jnp.max(block(0), axis=0)
            for r in range(1, tk // rb):
                mx = jnp.maximum(mx, jnp.max(block(r), axis=0))
            m_old = m_ref[...]
            m_new = jnp.maximum(m_old, jnp.max(mx, axis=0, keepdims=True))
            alpha = jnp.exp2(m_old - m_new)
            m_rows = jnp.broadcast_to(m_new, (SUBLANES, nq))[None]
            for r in range(tk // rb):
                p = jnp.exp2(block(r) - m_rows).reshape(rb, nq)
                ps_ref[r * rb:(r + 1) * rb, :] = p.astype(jnp.bfloat16)
            pv = jnp.dot(vT_ref[0, j], ps_ref[...], preferred_element_type=jnp.float32)
            acc_ref[...] = acc_ref[...] * alpha + pv
            m_ref[...] = m_new
            return carry

        lax.fori_loop(0, nchunk, chunk, 0)

    o = acc_ref[:dv, :] / acc_ref[dv:dv + 1, :]
    if diff:
        lp = lam_ref[...]
        lam = (jnp.exp(jnp.sum(lp[0:1] * lp[1:2], axis=-1, keepdims=True))
               - jnp.exp(jnp.sum(lp[2:3] * lp[3:4], axis=-1, keepdims=True))
               + lambda_init)
        o = o[:, :tq] - lam * o[:, tq:]
        ms = jnp.mean(o * o, axis=0, keepdims=True)
        o = o * lax.rsqrt(ms + EPS) * sub_ref[...] * (1.0 - lambda_init)
    o_ref[...] = o.T.astype(o_ref.dtype)


def _attention(qT, k, vT, score_bound, *, tq, diff=False, lam_p=None, subln=None, lambda_init=0.0, name):
    n_heads, n_qtiles, dq, q_tile = qT.shape
    s = n_qtiles * q_tile
    n_kv = k.shape[0]
    group = n_heads // n_kv
    _, nchunk, dv_ext, tk = vT.shape
    dv = dv_ext - ONES_ROWS
    nq = 2 * tq if diff else tq
    assert tq % q_tile == 0 and ATTN_CHUNKS_PER_TRIP % 2 == 0 and nchunk % ATTN_CHUNKS_PER_TRIP == 0
    in_specs = [pl.BlockSpec((1, n_qtiles, dq, q_tile), lambda h, i: (h, 0, 0, 0)),
                pl.BlockSpec((1, s, dq), lambda h, i: (h // group, 0, 0)),
                pl.BlockSpec((1, nchunk, dv_ext, tk), lambda h, i: (h // group, 0, 0, 0)),
                pl.BlockSpec((1, nq), lambda h, i: (0, 0))]
    args = [qT, k, vT, jnp.full((1, nq), score_bound, jnp.float32)]
    if diff:
        in_specs += [pl.BlockSpec(lam_p.shape, lambda h, i: (0, 0)),
                     pl.BlockSpec((dv, 1), lambda h, i: (0, 0))]
        args += [lam_p, subln.reshape(dv, 1)]
    return pl.pallas_call(
        functools.partial(_attn_body, nchunk=nchunk, tk=tk, tq=tq, diff=diff, lambda_init=lambda_init),
        grid=(n_heads, s // tq),
        in_specs=in_specs,
        out_specs=pl.BlockSpec((tq, dv), lambda h, i: (i, h)),
        out_shape=jax.ShapeDtypeStruct((s, n_heads * dv), jnp.bfloat16),
        scratch_shapes=[pltpu.VMEM((dv_ext, nq), jnp.float32),
                        pltpu.VMEM((2, tk, nq), jnp.bfloat16),
                        pltpu.VMEM((s // tq, dq, nq), jnp.bfloat16),
                        pltpu.VMEM((1, nq), jnp.float32),
                        pltpu.VMEM((tk, nq), jnp.float32),
                        pltpu.VMEM((tk, nq), jnp.bfloat16)],
        compiler_params=_cparams(("arbitrary", "arbitrary")),
        name=name,
    )(*args)


def _rope_cos_sin(pos, dim, theta):
    inv = theta ** (-jnp.arange(0, dim, 2, dtype=jnp.float32) / dim)
    ang = pos.astype(jnp.float32)[:, None] * inv[None, :]
    return jnp.cos(ang), jnp.sin(ang)


def _rotate_half_tables(cos, sin):
    zero = jnp.zeros_like(sin)
    return (jnp.concatenate([cos, cos], axis=-1),
            jnp.concatenate([-sin, zero], axis=-1),
            jnp.concatenate([zero, sin], axis=-1))


def _rope_tables(s):
    t = jnp.arange(s, dtype=jnp.int32)
    row = t // GRID_W
    col = t % GRID_W
    half = A_HEAD_DIM // 2
    a_row = _rotate_half_tables(*_rope_cos_sin(row, half, A_ROPE_THETA))
    a_col = _rotate_half_tables(*_rope_cos_sin(col, half, A_ROPE_THETA))
    a_tabs = tuple(jnp.concatenate([r, c], axis=-1) for r, c in zip(a_row, a_col))

    b_rot = _rotate_half_tables(*_rope_cos_sin(t, B_ROPE_DIM, B_ROPE_THETA))
    rest = B_QK_DIM - B_ROPE_DIM
    fill = (jnp.ones((s, rest), jnp.float32), jnp.zeros((s, rest), jnp.float32),
            jnp.zeros((s, rest), jnp.float32))
    b_tabs = tuple(jnp.tile(jnp.concatenate([r, f], axis=-1), (1, 2)) for r, f in zip(b_rot, fill))

    c_rot = _rotate_half_tables(*_rope_cos_sin(t, C_ROPE, C_ROPE_THETA))
    ones = jnp.ones((s, C_NOPE), jnp.float32)
    zeros = jnp.zeros((s, C_NOPE), jnp.float32)
    pad = jnp.zeros((s, C_QK_PAD - C_QK_DIM), jnp.float32)
    c_tabs = (jnp.concatenate([ones, c_rot[0], pad], axis=-1),
              jnp.concatenate([zeros, c_rot[1], pad], axis=-1),
              jnp.concatenate([zeros, c_rot[2], pad], axis=-1))
    return a_tabs, b_tabs, c_tabs


def _score_bound(dim, q_gain, k_gain, scale):
    return BOUND_SLACK * dim * scale * jnp.max(jnp.abs(q_gain)) * jnp.max(jnp.abs(k_gain))


def _pad_vec(g, n):
    return jnp.pad(g, (0, n - g.shape[0]))


def kernel(x, attn_norm, w_in, a_q_norm, a_k_norm, b_q_norm, b_k_norm, b_lambda, b_subln,
           c_q_a_norm, c_kv_a_norm, c_w_q_up, c_w_kv_up, c_q_norm, c_k_norm, w_out,
           ffn_norm, w_gate, w_up, w_down):
    bsz, s, d = x.shape
    assert (bsz, s, d) == (1, SEQ, D_MODEL)
    bf = jnp.bfloat16
    a_tabs, b_tabs, c_tabs = _rope_tables(s)
    tp = 512
    h = x.reshape(s, d)
    w_in_t = jnp.swapaxes(w_in, 1, 2)

    for l in range(DEPTH):
        lambda_init = 0.8 - 0.6 * math.exp(-0.3 * l)
        wq_up = c_w_q_up[l].astype(bf).reshape(C_Q_LORA, C_HEADS, C_QK_DIM)
        wq_up = jnp.pad(wq_up, ((0, 0), (0, 0), (0, C_QK_PAD - C_QK_DIM))).reshape(C_Q_LORA, C_HEADS * C_QK_PAD)
        wkv_up = c_w_kv_up[l].astype(bf)

        xn = _rmsnorm(h, attn_norm[l])
        proj = _matmul([xn], [w_in_t], out_dtype=jnp.float32, tm=1024, tn=PROJ_BLOCK, layer=l,
                       w_transposed=True, n_cols=N_IN_MAIN, name="in_proj")
        c_kr = _tail_proj(xn, w_in_t, layer=l, row0=N_IN_MAIN, tm=1024, name="in_proj_kr")

        def proj_blocks(first, count):
            return [(proj, PROJ_BLOCK, first + b) for b in range(count)]

        a_scale = LOG2E / math.sqrt(A_HEAD_DIM)
        a_q = _prep(proj_blocks(0, 3), _head_pieces(A_HEADS), out_kind="cols", tm=tp,
                    gain=a_q_norm[l], norm="full", n_norm=A_HEAD_DIM, tabs=a_tabs, rope_blocks=(0,),
                    shifts=(96, 32), scale=a_scale, name="prep_a_q")
        a_k = _prep(proj_blocks(3, 1), _head_pieces(A_KV_HEADS), out_kind="rows", tm=tp,
                    gain=a_k_norm[l], norm="full", n_norm=A_HEAD_DIM, tabs=a_tabs, rope_blocks=(0,),
                    shifts=(96, 32), name="prep_a_k")
        a_v = _prep(proj_blocks(4, 1), _head_pieces(A_KV_HEADS), out_kind="chunks", tm=ATTN_KEY_CHUNK,
                    name="prep_a_v")
        ya = _attention(a_q, a_k, a_v, _score_bound(A_HEAD_DIM, a_q_norm[l], a_k_norm[l], a_scale),
                        tq=1024, name="attn_a")

        b_scale = LOG2E / math.sqrt(B_QK_DIM)
        b_q = _prep(proj_blocks(5, 2), _head_pieces(B_HEADS), out_kind="cols", tm=tp,
                    gain=jnp.tile(b_q_norm[l], 2), norm="half", n_norm=B_QK_DIM, tabs=b_tabs,
                    rope_blocks=(0,), shifts=(120, 8), scale=b_scale, name="prep_b_q")
        b_k = _prep(proj_blocks(7, 2), _head_pieces(B_HEADS), out_kind="rows", tm=tp,
                    gain=jnp.tile(b_k_norm[l], 2), norm="half", n_norm=B_QK_DIM, tabs=b_tabs,
                    rope_blocks=(0,), shifts=(120, 8), name="prep_b_k")
        b_v = _prep(proj_blocks(9, 2), _head_pieces(B_HEADS), out_kind="chunks", tm=ATTN_KEY_CHUNK,
                    name="prep_b_v")
        yb = _attention(b_q, b_k, b_v, _score_bound(B_QK_DIM, b_q_norm[l], b_k_norm[l], b_scale),
                        tq=512, diff=True, lam_p=b_lambda[l], subln=b_subln[l],
                        lambda_init=lambda_init, name="attn_b")

        cq_lat = _prep(proj_blocks(11, 2), _head_pieces(1, blocks=C_Q_LORA // LANES), out_kind="rows", tm=tp,
                       gain=c_q_a_norm[l], norm="full", n_norm=C_Q_LORA, name="norm_c_q")[0]
        ckv_lat = _prep(proj_blocks(13, 1), _head_pieces(1, blocks=C_KV_LORA // LANES), out_kind="rows", tm=tp,
                        gain=c_kv_a_norm[l], norm="full", n_norm=C_KV_LORA, name="norm_c_kv")[0]
        c_q_raw = _matmul([cq_lat], [wq_up], out_dtype=jnp.float32, tm=1024, tn=512, name="c_q_up")
        c_kv_raw = _matmul([ckv_lat], [wkv_up], out_dtype=jnp.float32, tm=1024, tn=512, name="c_kv_up")
        c_scale = LOG2E / math.sqrt(C_QK_DIM)
        n_up = C_HEADS * C_QK_PAD // PROJ_BLOCK
        c_q = _prep([(c_q_raw, PROJ_BLOCK, b) for b in range(n_up)],
                    _head_pieces(C_HEADS, lanes_per_head=C_QK_PAD, blocks=2), out_kind="cols", tm=tp,
                    gain=_pad_vec(c_q_norm[l], C_QK_PAD), norm="full", n_norm=C_QK_DIM, tabs=c_tabs,
                    rope_blocks=(1,), shifts=(96, 32), scale=c_scale, name="prep_c_q")
        kv_srcs = [(c_kv_raw, PROJ_BLOCK, b) for b in range(n_up)]
        c_k_pieces = [[nope, (n_up, 0)] for (nope,) in _head_pieces(C_HEADS, lanes_per_head=2 * LANES)]
        c_k = _prep(kv_srcs + [(c_kr, LANES, 0)], c_k_pieces, out_kind="rows", tm=tp,
                    gain=_pad_vec(c_k_norm[l], C_QK_PAD), norm="full", n_norm=C_QK_DIM, tabs=c_tabs,
                    rope_blocks=(1,), shifts=(96, 32), name="prep_c_k")
        c_v = _prep(kv_srcs, _head_pieces(C_HEADS, lanes_per_head=2 * LANES, lane0=LANES), out_kind="chunks",
                    tm=ATTN_KEY_CHUNK, name="prep_c_v")
        yc = _attention(c_q, c_k, c_v, _score_bound(C_QK_DIM, c_q_norm[l], c_k_norm[l], c_scale),
                        tq=1024, name="attn_c")

        h = _matmul([ya, yb, yc], [w_out], res=h, out_dtype=jnp.float32, tm=1024, tn=512, layer=l,
                    name="out_proj")

        hn = _rmsnorm(h, ffn_norm[l])
        act = _matmul([hn], [w_gate, w_up], out_dtype=bf, tm=1024, tn=256, layer=l, name="ffn_gate_up")
        h = _matmul_kouter(act, w_down, h, layer=l, tm=1024, tn=256, tk=FFN_HIDDEN // 2, n_split=2,
                           name="ffn_down")

    return h.reshape(bsz, s, d)
```

```python
import functools
import math

import jax
import jax.numpy as jnp
import numpy as np
from jax import lax
from jax.experimental import pallas as pl
from jax.experimental.pallas import tpu as pltpu

D_MODEL = 4096
SEQ = 8192
DEPTH = 2
GRID_W = 64
EPS = 1e-6

A_HEAD_DIM = 128
A_HEADS = 12
A_KV_HEADS = 4
A_ROPE_THETA = 10000.0

B_QK_DIM = 64
B_V_DIM = 128
B_HEADS = 8
B_ROPE_DIM = 16
B_ROPE_THETA = 500000.0

C_V_DIM = 128
C_HEADS = 12
C_Q_LORA = 1024
C_KV_LORA = 512
C_NOPE = 128
C_ROPE = 64
C_QK_DIM = C_NOPE + C_ROPE
C_QK_PAD = 256
C_ROPE_THETA = 10000.0

N_IN = 7232
FFN_HIDDEN = 11008

LANES = 128
SUBLANES = 8
ATTN_ROW_BLOCK = 64
ATTN_KEY_CHUNK = 512
ATTN_CHUNKS_PER_TRIP = 8
ATTN_MIN_DENOM = 2.0 ** -60
BOUND_SLACK = 1.01
MXU_SUM_MAX_WIDTH = 256
ONES_ROWS = 16
VMEM_LIMIT_BYTES = 56 * 1024 * 1024

LOG2E = math.log2(math.e)

PROJ_BLOCK = 512
N_IN_MAIN = 14 * PROJ_BLOCK


def _head_pieces(n_heads, lanes_per_head=LANES, lane0=0, blocks=1):
    out = []
    for h in range(n_heads):
        head = []
        for b in range(blocks):
            col = lane0 + h * lanes_per_head + b * LANES
            head.append((col // PROJ_BLOCK, col % PROJ_BLOCK))
        out.append(head)
    return out


def _cparams(sem):
    return pltpu.CompilerParams(dimension_semantics=sem, vmem_limit_bytes=VMEM_LIMIT_BYTES)


def _rmsnorm_body(x_ref, g_ref, o_ref):
    x = x_ref[...]
    ms = jnp.mean(x * x, axis=-1, keepdims=True)
    o_ref[...] = (x * lax.rsqrt(ms + EPS) * g_ref[...]).astype(o_ref.dtype)


def _rmsnorm(x, g, tm=256):
    s, d = x.shape
    return pl.pallas_call(
        _rmsnorm_body,
        grid=(s // tm,),
        in_specs=[pl.BlockSpec((tm, d), lambda i: (i, 0)),
                  pl.BlockSpec((1, d), lambda i: (0, 0))],
        out_specs=pl.BlockSpec((tm, d), lambda i: (i, 0)),
        out_shape=jax.ShapeDtypeStruct((s, d), jnp.bfloat16),
        compiler_params=_cparams(("parallel",)),
        name="rmsnorm",
    )(x, g.reshape(1, d))


def _matmul_body(*refs, n_x, n_w, has_res, w_transposed):
    x_refs = refs[:n_x]
    w_refs = refs[n_x:n_x + n_w]
    pos = n_x + n_w
    res_ref = refs[pos] if has_res else None
    o_ref = refs[pos + int(has_res)]

    x = x_refs[0][...] if n_x == 1 else jnp.concatenate([r[...] for r in x_refs], axis=1)
    tiles = [w[...] if w.dtype == x.dtype else w[...].astype(x.dtype) for w in w_refs]
    contract = (((1,), (1,)), ((), ())) if w_transposed else (((1,), (0,)), ((), ()))
    prods = [lax.dot_general(x, w, contract, preferred_element_type=jnp.float32) for w in tiles]
    if n_w == 2:
        g, u = prods
        y = g * jax.nn.sigmoid(g) * u
    else:
        y = prods[0]
    if has_res:
        y = y + res_ref[...]
    o_ref[...] = y.astype(o_ref.dtype)


def _matmul(xs, ws, res=None, *, out_dtype, tm, tn, name, layer=None, w_transposed=False, n_cols=None):
    m = xs[0].shape[0]
    kdim = sum(x.shape[1] for x in xs)
    n = n_cols if n_cols is not None else ws[0].shape[-2 if w_transposed else -1]
    assert m % tm == 0 and n % tn == 0
    in_specs = [pl.BlockSpec((tm, x.shape[1]), lambda i, j: (i, 0)) for x in xs]
    w_block = (tn, kdim) if w_transposed else (kdim, tn)
    lead = () if layer is None else (None,)

    def w_index(i, j):
        pos = (j, 0) if w_transposed else (0, j)
        return pos if layer is None else (layer,) + pos

    in_specs += [pl.BlockSpec(lead + w_block, w_index) for _ in ws]
    args = [*xs, *ws]
    if res is not None:
        in_specs.append(pl.BlockSpec((tm, tn), lambda i, j: (i, j)))
        args.append(res)
    return pl.pallas_call(
        functools.partial(_matmul_body, n_x=len(xs), n_w=len(ws), has_res=res is not None,
                          w_transposed=w_transposed),
        grid=(m // tm, n // tn),
        in_specs=in_specs,
        out_specs=pl.BlockSpec((tm, tn), lambda i, j: (i, j)),
        out_shape=jax.ShapeDtypeStruct((m, n), out_dtype),
        compiler_params=_cparams(("parallel", "parallel")),
        name=name,
    )(*args)


def _tail_proj_body(x_ref, w_ref, o_ref):
    w = w_ref[...].astype(x_ref.dtype)
    w = jnp.concatenate([w, jnp.zeros_like(w)], axis=0)
    o_ref[...] = lax.dot_general(x_ref[...], w, (((1,), (1,)), ((), ())),
                                 preferred_element_type=jnp.float32)


def _tail_proj(x, w_t, *, layer, row0, tm, name):
    m, kdim = x.shape
    rows = w_t.shape[1] - row0
    assert rows * 2 == LANES and row0 % rows == 0
    return pl.pallas_call(
        _tail_proj_body,
        grid=(m // tm,),
        in_specs=[pl.BlockSpec((tm, kdim), lambda i: (i, 0)),
                  pl.BlockSpec((None, rows, kdim), lambda i: (layer, row0 // rows, 0))],
        out_specs=pl.BlockSpec((tm, LANES), lambda i: (i, 0)),
        out_shape=jax.ShapeDtypeStruct((m, LANES), jnp.float32),
        compiler_params=_cparams(("parallel",)),
        name=name,
    )(x, w_t)


def _matmul_kouter_body(x_ref, w_ref, res_ref, o_ref, acc_ref, *, nk):
    k = pl.program_id(2)
    j = pl.program_id(3)
    prod = jnp.dot(x_ref[...], w_ref[...].astype(x_ref.dtype), preferred_element_type=jnp.float32)

    @pl.when(k == 0)
    def _():
        acc_ref[j] = prod

    @pl.when(jnp.logical_and(k > 0, k < nk - 1))
    def _():
        acc_ref[j] += prod

    @pl.when(k == nk - 1)
    def _():
        o_ref[...] = (acc_ref[j] + prod + res_ref[...]).astype(o_ref.dtype)


def _matmul_kouter(x, w, res, *, layer, tm, tn, tk, n_split, name):
    m, kdim = x.shape
    n = w.shape[-1]
    nk = kdim // tk
    nj = n // (tn * n_split)
    assert m % tm == 0 and n % (tn * n_split) == 0 and kdim % tk == 0 and nk >= 2

    def out_index(hf, i, k, j):
        return i, hf * nj + jnp.where(k == nk - 1, j, 0)

    return pl.pallas_call(
        functools.partial(_matmul_kouter_body, nk=nk),
        grid=(n_split, m // tm, nk, nj),
        in_specs=[pl.BlockSpec((tm, tk), lambda hf, i, k, j: (i, k)),
                  pl.BlockSpec((None, tk, tn), lambda hf, i, k, j: (layer, k, hf * nj + j)),
                  pl.BlockSpec((tm, tn), out_index)],
        out_specs=pl.BlockSpec((tm, tn), out_index),
        out_shape=jax.ShapeDtypeStruct((m, n), res.dtype),
        scratch_shapes=[pltpu.VMEM((nj, tm, tn), jnp.float32)],
        compiler_params=_cparams(("parallel", "parallel", "arbitrary", "arbitrary")),
        name=name,
    )(x, w, res)


def _prep_head(x, g_ref, tabs, *, norm, n_norm, rope_blocks, shifts, scale, transpose, ones_rows):
    width = x.shape[-1]
    if norm is not None:
        if width <= MXU_SUM_MAX_WIDTH:
            r = lax.broadcasted_iota(jnp.int32, (width, width), 0)
            c = lax.broadcasted_iota(jnp.int32, (width, width), 1)
            group = LANES // 2 if norm == "half" else width
            same = (r // group == c // group).astype(jnp.bfloat16)
            ss = jnp.dot((x * x).astype(jnp.bfloat16), same, preferred_element_type=jnp.float32)
        else:
            ss = jnp.sum(x * x, axis=-1, keepdims=True)
        x = x * lax.rsqrt(ss * (1.0 / n_norm) + EPS) * g_ref[...]

    if rope_blocks:
        c_ref, s1_ref, s2_ref = tabs
        blocks = []
        for b in range(width // LANES):
            xb = x[:, b * LANES:(b + 1) * LANES]
            if b in rope_blocks:
                sl = slice(b * LANES, (b + 1) * LANES)
                xb = (xb * c_ref[:, sl]
                      + pltpu.roll(xb, shifts[0], 1) * s1_ref[:, sl]
                      + pltpu.roll(xb, shifts[1], 1) * s2_ref[:, sl])
            blocks.append(xb)
        x = blocks[0] if len(blocks) == 1 else jnp.concatenate(blocks, axis=-1)

    if scale != 1.0:
        x = x * scale
    x = x.astype(jnp.bfloat16)
    if transpose:
        eye = (lax.broadcasted_iota(jnp.int32, (width, width), 0)
               == lax.broadcasted_iota(jnp.int32, (width, width), 1)).astype(jnp.bfloat16)
        x = lax.dot_general(eye, x, (((1,), (1,)), ((), ())),
                            preferred_element_type=jnp.float32).astype(jnp.bfloat16)
    if ones_rows:
        x = jnp.concatenate([x, jnp.ones((ones_rows, x.shape[1]), x.dtype)], axis=0)
    return x


def _prep_body(*refs, nx, pieces, has_gain, has_tabs, **head_kw):
    x_refs = refs[:nx]
    pos = nx
    g_ref = None
    if has_gain:
        g_ref = refs[pos]
        pos += 1
    tabs = None
    if has_tabs:
        tabs = refs[pos:pos + 3]
        pos += 3
    o_ref = refs[pos]
    for h, head_pieces in enumerate(pieces):
        cols = [x_refs[src][:, off:off + LANES] for src, off in head_pieces]
        x = cols[0] if len(cols) == 1 else jnp.concatenate(cols, axis=-1)
        y = _prep_head(x, g_ref, tabs, **head_kw)
        o_ref[h] = y.reshape(o_ref.shape[1:])


def _prep(srcs, pieces, *, out_kind, tm, gain=None, norm=None, n_norm=None, tabs=None,
          rope_blocks=(), shifts=(0, 0), scale=1.0, name="prep"):
    s = srcs[0][0].shape[0]
    n_heads = len(pieces)
    width = LANES * len(pieces[0])
    in_specs, args = [], []
    for arr, bw, idx in srcs:
        in_specs.append(pl.BlockSpec((tm, bw), functools.partial(lambda i, idx: (i, idx), idx=idx)))
        args.append(arr)
    if norm is not None:
        in_specs.append(pl.BlockSpec((1, width), lambda i: (0, 0)))
        args.append(gain.reshape(1, width))
    if rope_blocks:
        for t in tabs:
            in_specs.append(pl.BlockSpec((tm, width), lambda i: (i, 0)))
            args.append(t)
    if out_kind == "rows":
        out_shape = (n_heads, s, width)
        out_spec = pl.BlockSpec((n_heads, tm, width), lambda i: (0, i, 0))
    elif out_kind == "cols":
        out_shape = (n_heads, s // tm, width, tm)
        out_spec = pl.BlockSpec((n_heads, 1, width, tm), lambda i: (0, i, 0, 0))
    else:
        out_shape = (n_heads, s // tm, width + ONES_ROWS, tm)
        out_spec = pl.BlockSpec((n_heads, 1, width + ONES_ROWS, tm), lambda i: (0, i, 0, 0))
    return pl.pallas_call(
        functools.partial(_prep_body, nx=len(srcs), pieces=tuple(tuple(p) for p in pieces),
                          has_gain=norm is not None, has_tabs=bool(rope_blocks),
                          norm=norm, n_norm=n_norm, rope_blocks=tuple(rope_blocks), shifts=shifts,
                          scale=scale, transpose=out_kind != "rows",
                          ones_rows=ONES_ROWS if out_kind == "chunks" else 0),
        grid=(s // tm,),
        in_specs=in_specs,
        out_specs=out_spec,
        out_shape=jax.ShapeDtypeStruct(out_shape, jnp.bfloat16),
        compiler_params=_cparams(("parallel",)),
        name=name,
    )(*args)


def _attn_body(*refs, nchunk, tk, tq, diff, lambda_init):
    if diff:
        (qT_ref, k_ref, vT_ref, shift_ref, lam_ref, sub_ref, o_ref,
         acc_ref, l_ref, lnext_ref, p_ref, qq_ref, m_ref, s_ref, ps_ref) = refs
    else:
        (qT_ref, k_ref, vT_ref, shift_ref, o_ref,
         acc_ref, l_ref, lnext_ref, p_ref, qq_ref, m_ref, s_ref, ps_ref) = refs

    i = pl.program_id(1)
    nqb, _, nq = qq_ref.shape
    dv = o_ref.shape[1]
    tiles_per_block = tq // qT_ref.shape[3]
    chunk_bits = nchunk.bit_length() - 1
    assert nchunk == 1 << chunk_bits

    def scores(step):
        chunk = step & (nchunk - 1)
        block = jnp.minimum(i + (step >> chunk_bits), nqb - 1)
        start = pl.multiple_of(chunk * tk, tk)
        return jnp.dot(k_ref[0, pl.ds(start, tk), :], qq_ref[block],
                       preferred_element_type=jnp.float32)

    def probs(step):
        p = jnp.exp2(scores(step) - shift_ref[...])
        part = jnp.sum(p.reshape(tk // SUBLANES, SUBLANES, nq), axis=0)
        return p.astype(jnp.bfloat16), part

    @pl.when(i == 0)
    def _():
        for b in range(nqb):
            tiles = [qT_ref[0, b * tiles_per_block + c] for c in range(tiles_per_block)]
            q = tiles[0] if len(tiles) == 1 else jnp.concatenate(tiles, axis=1)
            if diff:
                row = lax.broadcasted_iota(jnp.int32, q.shape, 0)
                first = row < B_QK_DIM
                zero = jnp.zeros_like(q)
                q = jnp.concatenate([jnp.where(first, q, zero), jnp.where(first, zero, q)], axis=1)
            qq_ref[b] = q
        p_ref[0], lnext_ref[...] = probs(0)

    acc_ref[...] = jnp.zeros(acc_ref.shape, jnp.float32)
    l_ref[...] = lnext_ref[...]

    def trip(t, carry):
        for u in range(ATTN_CHUNKS_PER_TRIP):
            j = t * ATTN_CHUNKS_PER_TRIP + u
            p_next, part = probs(j + 1)
            p_ref[(u + 1) % 2] = p_next
            if u + 1 < ATTN_CHUNKS_PER_TRIP:
                l_ref[...] += part
            else:
                beyond = j + 1 == nchunk
                l_ref[...] += jnp.where(beyond, 0.0, part)
                lnext_ref[...] = part
            acc_ref[:dv, :] += jnp.dot(vT_ref[0, j, :dv, :], p_ref[u % 2],
                                       preferred_element_type=jnp.float32)
        return carry

    lax.fori_loop(0, nchunk // ATTN_CHUNKS_PER_TRIP, trip, 0)
    acc_ref[dv:dv + 1, :] = jnp.sum(l_ref[...], axis=0, keepdims=True)

    healthy = jnp.min(acc_ref[dv:dv + 1, :]) >= ATTN_MIN_DENOM

    @pl.when(jnp.logical_not(healthy))
    def _():
        rb = ATTN_ROW_BLOCK
        groups = rb // SUBLANES
        m_ref[...] = jnp.full(m_ref.shape, -jnp.inf, jnp.float32)
        acc_ref[...] = jnp.zeros(acc_ref.shape, jnp.float32)

        def block(r):
            return s_ref[r * rb:(r + 1) * rb, :].reshape(groups, SUBLANES, nq)

        def chunk(j, carry):
            s_ref[...] = scores(j)
            mx = jnp.max(block(0), axis=0)
            for r in range(1, tk // rb):
                mx = jnp.maximum(mx, jnp.max(block(r), axis=0))
            m_old = m_ref[...]
            m_new = jnp.maximum(m_old, jnp.max(mx, axis=0, keepdims=True))
            alpha = jnp.exp2(m_old - m_new)
            m_rows = jnp.broadcast_to(m_new, (SUBLANES, nq))[None]
            for r in range(tk // rb):
                p = jnp.exp2(block(r) - m_rows).reshape(rb, nq)
                ps_ref[r * rb:(r + 1) * rb, :] = p.astype(jnp.bfloat16)
            pv = jnp.dot(vT_ref[0, j], ps_ref[...], preferred_element_type=jnp.float32)
            acc_ref[...] = acc_ref[...] * alpha + pv
            m_ref[...] = m_new
            return carry

        lax.fori_loop(0, nchunk, chunk, 0)

    o = acc_ref[:dv, :] / acc_ref[dv:dv + 1, :]
    if diff:
        lp = lam_ref[...]
        lam = (jnp.exp(jnp.sum(lp[0:1] * lp[1:2], axis=-1, keepdims=True))
               - jnp.exp(jnp.sum(lp[2:3] * lp[3:4], axis=-1, keepdims=True))
               + lambda_init)
        o = o[:, :tq] - lam * o[:, tq:]
        ms = jnp.mean(o * o, axis=0, keepdims=True)
        o = o * lax.rsqrt(ms + EPS) * sub_ref[...] * (1.0 - lambda_init)
    o_ref[...] = o.T.astype(o_ref.dtype)


def _attention(qT, k, vT, score_bound, *, tq, diff=False, lam_p=None, subln=None, lambda_init=0.0, name):
    n_heads, n_qtiles, dq, q_tile = qT.shape
    s = n_qtiles * q_tile
    n_kv = k.shape[0]
    group = n_heads // n_kv
    _, nchunk, dv_ext, tk = vT.shape
    dv = dv_ext - ONES_ROWS
    nq = 2 * tq if diff else tq
    assert tq % q_tile == 0 and ATTN_CHUNKS_PER_TRIP % 2 == 0 and nchunk % ATTN_CHUNKS_PER_TRIP == 0
    in_specs = [pl.BlockSpec((1, n_qtiles, dq, q_tile), lambda h, i: (h, 0, 0, 0)),
                pl.BlockSpec((1, s, dq), lambda h, i: (h // group, 0, 0)),
                pl.BlockSpec((1, nchunk, dv_ext, tk), lambda h, i: (h // group, 0, 0, 0)),
                pl.BlockSpec((1, nq), lambda h, i: (0, 0))]
    args = [qT, k, vT, jnp.full((1, nq), score_bound, jnp.float32)]
    if diff:
        in_specs += [pl.BlockSpec(lam_p.shape, lambda h, i: (0, 0)),
                     pl.BlockSpec((dv, 1), lambda h, i: (0, 0))]
        args += [lam_p, subln.reshape(dv, 1)]
    return pl.pallas_call(
        functools.partial(_attn_body, nchunk=nchunk, tk=tk, tq=tq, diff=diff, lambda_init=lambda_init),
        grid=(n_heads, s // tq),
        in_specs=in_specs,
        out_specs=pl.BlockSpec((tq, dv), lambda h, i: (i, h)),
        out_shape=jax.ShapeDtypeStruct((s, n_heads * dv), jnp.bfloat16),
        scratch_shapes=[pltpu.VMEM((dv_ext, nq), jnp.float32),
                        pltpu.VMEM((SUBLANES, nq), jnp.float32),
                        pltpu.VMEM((SUBLANES, nq), jnp.float32),
                        pltpu.VMEM((2, tk, nq), jnp.bfloat16),
                        pltpu.VMEM((s // tq, dq, nq), jnp.bfloat16),
                        pltpu.VMEM((1, nq), jnp.float32),
                        pltpu.VMEM((tk, nq), jnp.float32),
                        pltpu.VMEM((tk, nq), jnp.bfloat16)],
        compiler_params=_cparams(("arbitrary", "arbitrary")),
        name=name,
    )(*args)


def _rope_cos_sin(pos, dim, theta):
    inv = theta ** (-jnp.arange(0, dim, 2, dtype=jnp.float32) / dim)
    ang = pos.astype(jnp.float32)[:, None] * inv[None, :]
    return jnp.cos(ang), jnp.sin(ang)


def _rotate_half_tables(cos, sin):
    zero = jnp.zeros_like(sin)
    return (jnp.concatenate([cos, cos], axis=-1),
            jnp.concatenate([-sin, zero], axis=-1),
            jnp.concatenate([zero, sin], axis=-1))


def _rope_tables(s):
    t = jnp.arange(s, dtype=jnp.int32)
    row = t // GRID_W
    col = t % GRID_W
    half = A_HEAD_DIM // 2
    a_row = _rotate_half_tables(*_rope_cos_sin(row, half, A_ROPE_THETA))
    a_col = _rotate_half_tables(*_rope_cos_sin(col, half, A_ROPE_THETA))
    a_tabs = tuple(jnp.concatenate([r, c], axis=-1) for r, c in zip(a_row, a_col))

    b_rot = _rotate_half_tables(*_rope_cos_sin(t, B_ROPE_DIM, B_ROPE_THETA))
    rest = B_QK_DIM - B_ROPE_DIM
    fill = (jnp.ones((s, rest), jnp.float32), jnp.zeros((s, rest), jnp.float32),
            jnp.zeros((s, rest), jnp.float32))
    b_tabs = tuple(jnp.tile(jnp.concatenate([r, f], axis=-1), (1, 2)) for r, f in zip(b_rot, fill))

    c_rot = _rotate_half_tables(*_rope_cos_sin(t, C_ROPE, C_ROPE_THETA))
    ones = jnp.ones((s, C_NOPE), jnp.float32)
    zeros = jnp.zeros((s, C_NOPE), jnp.float32)
    pad = jnp.zeros((s, C_QK_PAD - C_QK_DIM), jnp.float32)
    c_tabs = (jnp.concatenate([ones, c_rot[0], pad], axis=-1),
              jnp.concatenate([zeros, c_rot[1], pad], axis=-1),
              jnp.concatenate([zeros, c_rot[2], pad], axis=-1))
    return a_tabs, b_tabs, c_tabs


def _score_bound(dim, q_gain, k_gain, scale):
    return BOUND_SLACK * dim * scale * jnp.max(jnp.abs(q_gain)) * jnp.max(jnp.abs(k_gain))


def _pad_vec(g, n):
    return jnp.pad(g, (0, n - g.shape[0]))


def kernel(x, attn_norm, w_in, a_q_norm, a_k_norm, b_q_norm, b_k_norm, b_lambda, b_subln,
           c_q_a_norm, c_kv_a_norm, c_w_q_up, c_w_kv_up, c_q_norm, c_k_norm, w_out,
           ffn_norm, w_gate, w_up, w_down):
    bsz, s, d = x.shape
    assert (bsz, s, d) == (1, SEQ, D_MODEL)
    bf = jnp.bfloat16
    a_tabs, b_tabs, c_tabs = _rope_tables(s)
    tp = 512
    h = x.reshape(s, d)
    w_in_t = jnp.swapaxes(w_in, 1, 2)

    for l in range(DEPTH):
        lambda_init = 0.8 - 0.6 * math.exp(-0.3 * l)
        wq_up = c_w_q_up[l].astype(bf).reshape(C_Q_LORA, C_HEADS, C_QK_DIM)
        wq_up = jnp.pad(wq_up, ((0, 0), (0, 0), (0, C_QK_PAD - C_QK_DIM))).reshape(C_Q_LORA, C_HEADS * C_QK_PAD)
        wkv_up = c_w_kv_up[l].astype(bf)

        xn = _rmsnorm(h, attn_norm[l])
        proj = _matmul([xn], [w_in_t], out_dtype=jnp.float32, tm=1024, tn=PROJ_BLOCK, layer=l,
                       w_transposed=True, n_cols=N_IN_MAIN, name="in_proj")
        c_kr = _tail_proj(xn, w_in_t, layer=l, row0=N_IN_MAIN, tm=1024, name="in_proj_kr")

        def proj_blocks(first, count):
            return [(proj, PROJ_BLOCK, first + b) for b in range(count)]

        a_scale = LOG2E / math.sqrt(A_HEAD_DIM)
        a_q = _prep(proj_blocks(0, 3), _head_pieces(A_HEADS), out_kind="cols", tm=tp,
                    gain=a_q_norm[l], norm="full", n_norm=A_HEAD_DIM, tabs=a_tabs, rope_blocks=(0,),
                    shifts=(96, 32), scale=a_scale, name="prep_a_q")
        a_k = _prep(proj_blocks(3, 1), _head_pieces(A_KV_HEADS), out_kind="rows", tm=tp,
                    gain=a_k_norm[l], norm="full", n_norm=A_HEAD_DIM, tabs=a_tabs, rope_blocks=(0,),
                    shifts=(96, 32), name="prep_a_k")
        a_v = _prep(proj_blocks(4, 1), _head_pieces(A_KV_HEADS), out_kind="chunks", tm=ATTN_KEY_CHUNK,
                    name="prep_a_v")
        ya = _attention(a_q, a_k, a_v, _score_bound(A_HEAD_DIM, a_q_norm[l], a_k_norm[l], a_scale),
                        tq=1024, name="attn_a")

        b_scale = LOG2E / math.sqrt(B_QK_DIM)
        b_q = _prep(proj_blocks(5, 2), _head_pieces(B_HEADS), out_kind="cols", tm=tp,
                    gain=jnp.tile(b_q_norm[l], 2), norm="half", n_norm=B_QK_DIM, tabs=b_tabs,
                    rope_blocks=(0,), shifts=(120, 8), scale=b_scale, name="prep_b_q")
        b_k = _prep(proj_blocks(7, 2), _head_pieces(B_HEADS), out_kind="rows", tm=tp,
                    gain=jnp.tile(b_k_norm[l], 2), norm="half", n_norm=B_QK_DIM, tabs=b_tabs,
                    rope_blocks=(0,), shifts=(120, 8), name="prep_b_k")
        b_v = _prep(proj_blocks(9, 2), _head_pieces(B_HEADS), out_kind="chunks", tm=ATTN_KEY_CHUNK,
                    name="prep_b_v")
        yb = _attention(b_q, b_k, b_v, _score_bound(B_QK_DIM, b_q_norm[l], b_k_norm[l], b_scale),
                        tq=512, diff=True, lam_p=b_lambda[l], subln=b_subln[l],
                        lambda_init=lambda_init, name="attn_b")

        cq_lat = _prep(proj_blocks(11, 2), _head_pieces(1, blocks=C_Q_LORA // LANES), out_kind="rows", tm=tp,
                       gain=c_q_a_norm[l], norm="full", n_norm=C_Q_LORA, name="norm_c_q")[0]
        ckv_lat = _prep(proj_blocks(13, 1), _head_pieces(1, blocks=C_KV_LORA // LANES), out_kind="rows", tm=tp,
                        gain=c_kv_a_norm[l], norm="full", n_norm=C_KV_LORA, name="norm_c_kv")[0]
        c_q_raw = _matmul([cq_lat], [wq_up], out_dtype=jnp.float32, tm=1024, tn=512, name="c_q_up")
        c_kv_raw = _matmul([ckv_lat], [wkv_up], out_dtype=jnp.float32, tm=1024, tn=512, name="c_kv_up")
        c_scale = LOG2E / math.sqrt(C_QK_DIM)
        n_up = C_HEADS * C_QK_PAD // PROJ_BLOCK
        c_q = _prep([(c_q_raw, PROJ_BLOCK, b) for b in range(n_up)],
                    _head_pieces(C_HEADS, lanes_per_head=C_QK_PAD, blocks=2), out_kind="cols", tm=tp,
                    gain=_pad_vec(c_q_norm[l], C_QK_PAD), norm="full", n_norm=C_QK_DIM, tabs=c_tabs,
                    rope_blocks=(1,), shifts=(96, 32), scale=c_scale, name="prep_c_q")
        kv_srcs = [(c_kv_raw, PROJ_BLOCK, b) for b in range(n_up)]
        c_k_pieces = [[nope, (n_up, 0)] for (nope,) in _head_pieces(C_HEADS, lanes_per_head=2 * LANES)]
        c_k = _prep(kv_srcs + [(c_kr, LANES, 0)], c_k_pieces, out_kind="rows", tm=tp,
                    gain=_pad_vec(c_k_norm[l], C_QK_PAD), norm="full", n_norm=C_QK_DIM, tabs=c_tabs,
                    rope_blocks=(1,), shifts=(96, 32), name="prep_c_k")
        c_v = _prep(kv_srcs, _head_pieces(C_HEADS, lanes_per_head=2 * LANES, lane0=LANES), out_kind="chunks",
                    tm=ATTN_KEY_CHUNK, name="prep_c_v")
        yc = _attention(c_q, c_k, c_v, _score_bound(C_QK_DIM, c_q_norm[l], c_k_norm[l], c_scale),
                        tq=1024, name="attn_c")

        h = _matmul([ya, yb, yc], [w_out], res=h, out_dtype=jnp.float32, tm=1024, tn=512, layer=l,
                    name="out_proj")

        hn = _rmsnorm(h, ffn_norm[l])
        act = _matmul([hn], [w_gate, w_up], out_dtype=bf, tm=1024, tn=256, layer=l, name="ffn_gate_up")
        h = _matmul_kouter(act, w_down, h, layer=l, tm=1024, tn=256, tk=FFN_HIDDEN // 2, n_split=2,
                           name="ffn_down")

    return h.reshape(bsz, s, d)
```

```python
import functools
import math

import jax
import jax.numpy as jnp
import numpy as np
from jax import lax
from jax.experimental import pallas as pl
from jax.experimental.pallas import tpu as pltpu

D_MODEL = 4096
SEQ = 8192
DEPTH = 2
GRID_W = 64
EPS = 1e-6

A_HEAD_DIM = 128
A_HEADS = 12
A_KV_HEADS = 4
A_ROPE_THETA = 10000.0

B_QK_DIM = 64
B_V_DIM = 128
B_HEADS = 8
B_ROPE_DIM = 16
B_ROPE_THETA = 500000.0

C_V_DIM = 128
C_HEADS = 12
C_Q_LORA = 1024
C_KV_LORA = 512
C_NOPE = 128
C_ROPE = 64
C_QK_DIM = C_NOPE + C_ROPE
C_QK_PAD = 256
C_ROPE_THETA = 10000.0

N_IN = 7232
FFN_HIDDEN = 11008

LANES = 128
SUBLANES = 8
ATTN_ROW_BLOCK = 64
ATTN_KEY_CHUNK = 512
ATTN_CHUNKS_PER_TRIP = 8
ATTN_MIN_DENOM = 2.0 ** -60
BOUND_SLACK = 1.01
MXU_SUM_MAX_WIDTH = 256
ONES_ROWS = 16
VMEM_LIMIT_BYTES = 56 * 1024 * 1024

LOG2E = math.log2(math.e)

PROJ_BLOCK = 512
N_IN_MAIN = 14 * PROJ_BLOCK


def _head_pieces(n_heads, lanes_per_head=LANES, lane0=0, blocks=1):
    out = []
    for h in range(n_heads):
        head = []
        for b in range(blocks):
            col = lane0 + h * lanes_per_head + b * LANES
            head.append((col // PROJ_BLOCK, col % PROJ_BLOCK))
        out.append(head)
    return out


def _cparams(sem):
    return pltpu.CompilerParams(dimension_semantics=sem, vmem_limit_bytes=VMEM_LIMIT_BYTES)


def _rmsnorm_body(x_ref, g_ref, o_ref):
    x = x_ref[...]
    ms = jnp.mean(x * x, axis=-1, keepdims=True)
    o_ref[...] = (x * lax.rsqrt(ms + EPS) * g_ref[...]).astype(o_ref.dtype)


def _rmsnorm(x, g, tm=256):
    s, d = x.shape
    return pl.pallas_call(
        _rmsnorm_body,
        grid=(s // tm,),
        in_specs=[pl.BlockSpec((tm, d), lambda i: (i, 0)),
                  pl.BlockSpec((1, d), lambda i: (0, 0))],
        out_specs=pl.BlockSpec((tm, d), lambda i: (i, 0)),
        out_shape=jax.ShapeDtypeStruct((s, d), jnp.bfloat16),
        compiler_params=_cparams(("parallel",)),
        name="rmsnorm",
    )(x, g.reshape(1, d))


def _matmul_body(*refs, n_x, n_w, has_res, w_transposed):
    x_refs = refs[:n_x]
    w_refs = refs[n_x:n_x + n_w]
    pos = n_x + n_w
    res_ref = refs[pos] if has_res else None
    o_ref = refs[pos + int(has_res)]

    x = x_refs[0][...] if n_x == 1 else jnp.concatenate([r[...] for r in x_refs], axis=1)
    tiles = [w[...] if w.dtype == x.dtype else w[...].astype(x.dtype) for w in w_refs]
    contract = (((1,), (1,)), ((), ())) if w_transposed else (((1,), (0,)), ((), ()))
    prods = [lax.dot_general(x, w, contract, preferred_element_type=jnp.float32) for w in tiles]
    if n_w == 2:
        g, u = prods
        y = g * jax.nn.sigmoid(g) * u
    else:
        y = prods[0]
    if has_res:
        y = y + res_ref[...]
    o_ref[...] = y.astype(o_ref.dtype)


def _matmul(xs, ws, res=None, *, out_dtype, tm, tn, name, layer=None, w_transposed=False, n_cols=None):
    m = xs[0].shape[0]
    kdim = sum(x.shape[1] for x in xs)
    n = n_cols if n_cols is not None else ws[0].shape[-2 if w_transposed else -1]
    assert m % tm == 0 and n % tn == 0
    in_specs = [pl.BlockSpec((tm, x.shape[1]), lambda i, j: (i, 0)) for x in xs]
    w_block = (tn, kdim) if w_transposed else (kdim, tn)
    lead = () if layer is None else (None,)

    def w_index(i, j):
        pos = (j, 0) if w_transposed else (0, j)
        return pos if layer is None else (layer,) + pos

    in_specs += [pl.BlockSpec(lead + w_block, w_index) for _ in ws]
    args = [*xs, *ws]
    if res is not None:
        in_specs.append(pl.BlockSpec((tm, tn), lambda i, j: (i, j)))
        args.append(res)
    return pl.pallas_call(
        functools.partial(_matmul_body, n_x=len(xs), n_w=len(ws), has_res=res is not None,
                          w_transposed=w_transposed),
        grid=(m // tm, n // tn),
        in_specs=in_specs,
        out_specs=pl.BlockSpec((tm, tn), lambda i, j: (i, j)),
        out_shape=jax.ShapeDtypeStruct((m, n), out_dtype),
        compiler_params=_cparams(("parallel", "parallel")),
        name=name,
    )(*args)


def _tail_proj_body(x_ref, w_ref, o_ref):
    w = w_ref[...].astype(x_ref.dtype)
    w = jnp.concatenate([w, jnp.zeros_like(w)], axis=0)
    o_ref[...] = lax.dot_general(x_ref[...], w, (((1,), (1,)), ((), ())),
                                 preferred_element_type=jnp.float32)


def _tail_proj(x, w_t, *, layer, row0, tm, name):
    m, kdim = x.shape
    rows = w_t.shape[1] - row0
    assert rows * 2 == LANES and row0 % rows == 0
    return pl.pallas_call(
        _tail_proj_body,
        grid=(m // tm,),
        in_specs=[pl.BlockSpec((tm, kdim), lambda i: (i, 0)),
                  pl.BlockSpec((None, rows, kdim), lambda i: (layer, row0 // rows, 0))],
        out_specs=pl.BlockSpec((tm, LANES), lambda i: (i, 0)),
        out_shape=jax.ShapeDtypeStruct((m, LANES), jnp.float32),
        compiler_params=_cparams(("parallel",)),
        name=name,
    )(x, w_t)


def _matmul_kouter_body(x_ref, w_ref, res_ref, o_ref, acc_ref, *, nk):
    k = pl.program_id(2)
    j = pl.program_id(3)
    prod = jnp.dot(x_ref[...], w_ref[...].astype(x_ref.dtype), preferred_element_type=jnp.float32)

    @pl.when(k == 0)
    def _():
        acc_ref[j] = prod

    @pl.when(jnp.logical_and(k > 0, k < nk - 1))
    def _():
        acc_ref[j] += prod

    @pl.when(k == nk - 1)
    def _():
        o_ref[...] = (acc_ref[j] + prod + res_ref[...]).astype(o_ref.dtype)


def _matmul_kouter(x, w, res, *, layer, tm, tn, tk, n_split, name):
    m, kdim = x.shape
    n = w.shape[-1]
    nk = kdim // tk
    nj = n // (tn * n_split)
    assert m % tm == 0 and n % (tn * n_split) == 0 and kdim % tk == 0 and nk >= 2

    def out_index(hf, i, k, j):
        return i, hf * nj + jnp.where(k == nk - 1, j, 0)

    return pl.pallas_call(
        functools.partial(_matmul_kouter_body, nk=nk),
        grid=(n_split, m // tm, nk, nj),
        in_specs=[pl.BlockSpec((tm, tk), lambda hf, i, k, j: (i, k)),
                  pl.BlockSpec((None, tk, tn), lambda hf, i, k, j: (layer, k, hf * nj + j)),
                  pl.BlockSpec((tm, tn), out_index)],
        out_specs=pl.BlockSpec((tm, tn), out_index),
        out_shape=jax.ShapeDtypeStruct((m, n), res.dtype),
        scratch_shapes=[pltpu.VMEM((nj, tm, tn), jnp.float32)],
        compiler_params=_cparams(("parallel", "parallel", "arbitrary", "arbitrary")),
        name=name,
    )(x, w, res)


def _prep_head(x, g_ref, tabs, *, norm, n_norm, rope_blocks, shifts, scale, transpose, ones_rows):
    width = x.shape[-1]
    if norm is not None:
        if width <= MXU_SUM_MAX_WIDTH:
            r = lax.broadcasted_iota(jnp.int32, (width, width), 0)
            c = lax.broadcasted_iota(jnp.int32, (width, width), 1)
            group = LANES // 2 if norm == "half" else width
            same = (r // group == c // group).astype(jnp.bfloat16)
            ss = jnp.dot((x * x).astype(jnp.bfloat16), same, preferred_element_type=jnp.float32)
        else:
            ss = jnp.sum(x * x, axis=-1, keepdims=True)
        x = x * lax.rsqrt(ss * (1.0 / n_norm) + EPS) * g_ref[...]

    if rope_blocks:
        c_ref, s1_ref, s2_ref = tabs
        blocks = []
        for b in range(width // LANES):
            xb = x[:, b * LANES:(b + 1) * LANES]
            if b in rope_blocks:
                sl = slice(b * LANES, (b + 1) * LANES)
                xb = (xb * c_ref[:, sl]
                      + pltpu.roll(xb, shifts[0], 1) * s1_ref[:, sl]
                      + pltpu.roll(xb, shifts[1], 1) * s2_ref[:, sl])
            blocks.append(xb)
        x = blocks[0] if len(blocks) == 1 else jnp.concatenate(blocks, axis=-1)

    if scale != 1.0:
        x = x * scale
    x = x.astype(jnp.bfloat16)
    if transpose:
        eye = (lax.broadcasted_iota(jnp.int32, (width, width), 0)
               == lax.broadcasted_iota(jnp.int32, (width, width), 1)).astype(jnp.bfloat16)
        x = lax.dot_general(eye, x, (((1,), (1,)), ((), ())),
                            preferred_element_type=jnp.float32).astype(jnp.bfloat16)
    if ones_rows:
        x = jnp.concatenate([x, jnp.ones((ones_rows, x.shape[1]), x.dtype)], axis=0)
    return x


def _with_ones_lane(y, lane):
    tm, width = y.shape
    if lane >= width:
        col = lax.broadcasted_iota(jnp.int32, (tm, LANES), 1)
        extra = jnp.where(col == lane - width, 1.0, 0.0).astype(y.dtype)
        return jnp.concatenate([y, extra], axis=1)
    col = lax.broadcasted_iota(jnp.int32, y.shape, 1)
    return jnp.where(col == lane, jnp.ones_like(y), y)


def _prep_body(*refs, nx, pieces, has_gain, has_tabs, ones_lane, **head_kw):
    x_refs = refs[:nx]
    pos = nx
    g_ref = None
    if has_gain:
        g_ref = refs[pos]
        pos += 1
    tabs = None
    if has_tabs:
        tabs = refs[pos:pos + 3]
        pos += 3
    o_ref = refs[pos]
    for h, head_pieces in enumerate(pieces):
        cols = [x_refs[src][:, off:off + LANES] for src, off in head_pieces]
        x = cols[0] if len(cols) == 1 else jnp.concatenate(cols, axis=-1)
        y = _prep_head(x, g_ref, tabs, **head_kw)
        if ones_lane is not None:
            y = _with_ones_lane(y, ones_lane)
        o_ref[h] = y.reshape(o_ref.shape[1:])


def _prep(srcs, pieces, *, out_kind, tm, gain=None, norm=None, n_norm=None, tabs=None,
          rope_blocks=(), shifts=(0, 0), scale=1.0, ones_lane=None, name="prep"):
    s = srcs[0][0].shape[0]
    n_heads = len(pieces)
    width = LANES * len(pieces[0])
    in_specs, args = [], []
    for arr, bw, idx in srcs:
        in_specs.append(pl.BlockSpec((tm, bw), functools.partial(lambda i, idx: (i, idx), idx=idx)))
        args.append(arr)
    if norm is not None:
        in_specs.append(pl.BlockSpec((1, width), lambda i: (0, 0)))
        args.append(gain.reshape(1, width))
    if rope_blocks:
        for t in tabs:
            in_specs.append(pl.BlockSpec((tm, width), lambda i: (i, 0)))
            args.append(t)
    if out_kind == "rows":
        out_w = width + LANES if ones_lane is not None and ones_lane >= width else width
        out_shape = (n_heads, s, out_w)
        out_spec = pl.BlockSpec((n_heads, tm, out_w), lambda i: (0, i, 0))
    elif out_kind == "cols":
        out_shape = (n_heads, s // tm, width, tm)
        out_spec = pl.BlockSpec((n_heads, 1, width, tm), lambda i: (0, i, 0, 0))
    else:
        out_shape = (n_heads, s // tm, width + ONES_ROWS, tm)
        out_spec = pl.BlockSpec((n_heads, 1, width + ONES_ROWS, tm), lambda i: (0, i, 0, 0))
    return pl.pallas_call(
        functools.partial(_prep_body, nx=len(srcs), pieces=tuple(tuple(p) for p in pieces),
                          has_gain=norm is not None, has_tabs=bool(rope_blocks), ones_lane=ones_lane,
                          norm=norm, n_norm=n_norm, rope_blocks=tuple(rope_blocks), shifts=shifts,
                          scale=scale, transpose=out_kind != "rows",
                          ones_rows=ONES_ROWS if out_kind == "chunks" else 0),
        grid=(s // tm,),
        in_specs=in_specs,
        out_specs=out_spec,
        out_shape=jax.ShapeDtypeStruct(out_shape, jnp.bfloat16),
        compiler_params=_cparams(("parallel",)),
        name=name,
    )(*args)


def _attn_body(*refs, nchunk, tk, tq, diff, lambda_init, shift_row):
    if diff:
        (qT_ref, k_ref, vT_ref, shift_ref, lam_ref, sub_ref, o_ref,
         acc_ref, p_ref, qq_ref, m_ref, s_ref, ps_ref) = refs
    else:
        qT_ref, k_ref, vT_ref, shift_ref, o_ref, acc_ref, p_ref, qq_ref, m_ref, s_ref, ps_ref = refs

    i = pl.program_id(1)
    nqb, _, nq = qq_ref.shape
    dv = o_ref.shape[1]
    tiles_per_block = tq // qT_ref.shape[3]
    chunk_bits = nchunk.bit_length() - 1
    assert nchunk == 1 << chunk_bits

    def scores(step):
        chunk = step & (nchunk - 1)
        block = jnp.minimum(i + (step >> chunk_bits), nqb - 1)
        start = pl.multiple_of(chunk * tk, tk)
        return jnp.dot(k_ref[0, pl.ds(start, tk), :], qq_ref[block],
                       preferred_element_type=jnp.float32)

    def probs(step):
        return jnp.exp2(scores(step)).astype(jnp.bfloat16)

    @pl.when(i == 0)
    def _():
        dq = qT_ref.shape[2]
        dk = qq_ref.shape[1]
        neg_shift = jnp.broadcast_to(-shift_ref[...], (dk, nq)).astype(jnp.bfloat16)
        row = lax.broadcasted_iota(jnp.int32, (dk, nq), 0)
        for b in range(nqb):
            tiles = [qT_ref[0, b * tiles_per_block + c] for c in range(tiles_per_block)]
            q = tiles[0] if len(tiles) == 1 else jnp.concatenate(tiles, axis=1)
            if diff:
                first = lax.broadcasted_iota(jnp.int32, q.shape, 0) < B_QK_DIM
                zero = jnp.zeros_like(q)
                q = jnp.concatenate([jnp.where(first, q, zero), jnp.where(first, zero, q)], axis=1)
            if dk > dq:
                q = jnp.concatenate([q, jnp.zeros((dk - dq, nq), q.dtype)], axis=0)
            qq_ref[b] = jnp.where(row == shift_row, neg_shift, q)
        p_ref[0] = probs(0)

    acc_ref[...] = jnp.zeros(acc_ref.shape, jnp.float32)

    def trip(t, carry):
        for u in range(ATTN_CHUNKS_PER_TRIP):
            j = t * ATTN_CHUNKS_PER_TRIP + u
            p_ref[(u + 1) % 2] = probs(j + 1)
            acc_ref[...] += jnp.dot(vT_ref[0, j], p_ref[u % 2], preferred_element_type=jnp.float32)
        return carry

    lax.fori_loop(0, nchunk // ATTN_CHUNKS_PER_TRIP, trip, 0)

    healthy = jnp.min(acc_ref[dv:dv + 1, :]) >= ATTN_MIN_DENOM

    @pl.when(jnp.logical_not(healthy))
    def _():
        rb = ATTN_ROW_BLOCK
        groups = rb // SUBLANES
        m_ref[...] = jnp.full(m_ref.shape, -jnp.inf, jnp.float32)
        acc_ref[...] = jnp.zeros(acc_ref.shape, jnp.float32)

        def block(r):
            return s_ref[r * rb:(r + 1) * rb, :].reshape(groups, SUBLANES, nq)

        def chunk(j, carry):
            s_ref[...] = scores(j)
            mx = jnp.max(block(0), axis=0)
            for r in range(1, tk // rb):
                mx = jnp.maximum(mx, jnp.max(block(r), axis=0))
            m_old = m_ref[...]
            m_new = jnp.maximum(m_old, jnp.max(mx, axis=0, keepdims=True))
            alpha = jnp.exp2(m_old - m_new)
            m_rows = jnp.broadcast_to(m_new, (SUBLANES, nq))[None]
            for r in range(tk // rb):
                p = jnp.exp2(block(r) - m_rows).reshape(rb, nq)
                ps_ref[r * rb:(r + 1) * rb, :] = p.astype(jnp.bfloat16)
            pv = jnp.dot(vT_ref[0, j], ps_ref[...], preferred_element_type=jnp.float32)
            acc_ref[...] = acc_ref[...] * alpha + pv
            m_ref[...] = m_new
            return carry

        lax.fori_loop(0, nchunk, chunk, 0)

    o = acc_ref[:dv, :] / acc_ref[dv:dv + 1, :]
    if diff:
        lp = lam_ref[...]
        lam = (jnp.exp(jnp.sum(lp[0:1] * lp[1:2], axis=-1, keepdims=True))
               - jnp.exp(jnp.sum(lp[2:3] * lp[3:4], axis=-1, keepdims=True))
               + lambda_init)
        o = o[:, :tq] - lam * o[:, tq:]
        ms = jnp.mean(o * o, axis=0, keepdims=True)
        o = o * lax.rsqrt(ms + EPS) * sub_ref[...] * (1.0 - lambda_init)
    o_ref[...] = o.T.astype(o_ref.dtype)


def _attention(qT, k, vT, score_bound, *, tq, shift_row, diff=False, lam_p=None, subln=None,
               lambda_init=0.0, name):
    n_heads, n_qtiles, dq, q_tile = qT.shape
    s = n_qtiles * q_tile
    n_kv = k.shape[0]
    group = n_heads // n_kv
    _, nchunk, dv_ext, tk = vT.shape
    dv = dv_ext - ONES_ROWS
    nq = 2 * tq if diff else tq
    dk = k.shape[2]
    assert tq % q_tile == 0 and ATTN_CHUNKS_PER_TRIP % 2 == 0 and nchunk % ATTN_CHUNKS_PER_TRIP == 0
    assert dq <= shift_row < dk or (dq == dk and shift_row < dk)
    in_specs = [pl.BlockSpec((1, n_qtiles, dq, q_tile), lambda h, i: (h, 0, 0, 0)),
                pl.BlockSpec((1, s, dk), lambda h, i: (h // group, 0, 0)),
                pl.BlockSpec((1, nchunk, dv_ext, tk), lambda h, i: (h // group, 0, 0, 0)),
                pl.BlockSpec((1, nq), lambda h, i: (0, 0))]
    args = [qT, k, vT, jnp.full((1, nq), score_bound, jnp.float32)]
    if diff:
        in_specs += [pl.BlockSpec(lam_p.shape, lambda h, i: (0, 0)),
                     pl.BlockSpec((dv, 1), lambda h, i: (0, 0))]
        args += [lam_p, subln.reshape(dv, 1)]
    return pl.pallas_call(
        functools.partial(_attn_body, nchunk=nchunk, tk=tk, tq=tq, diff=diff, lambda_init=lambda_init,
                          shift_row=shift_row),
        grid=(n_heads, s // tq),
        in_specs=in_specs,
        out_specs=pl.BlockSpec((tq, dv), lambda h, i: (i, h)),
        out_shape=jax.ShapeDtypeStruct((s, n_heads * dv), jnp.bfloat16),
        scratch_shapes=[pltpu.VMEM((dv_ext, nq), jnp.float32),
                        pltpu.VMEM((2, tk, nq), jnp.bfloat16),
                        pltpu.VMEM((s // tq, dk, nq), jnp.bfloat16),
                        pltpu.VMEM((1, nq), jnp.float32),
                        pltpu.VMEM((tk, nq), jnp.float32),
                        pltpu.VMEM((tk, nq), jnp.bfloat16)],
        compiler_params=_cparams(("arbitrary", "arbitrary")),
        name=name,
    )(*args)


def _rope_cos_sin(pos, dim, theta):
    inv = theta ** (-jnp.arange(0, dim, 2, dtype=jnp.float32) / dim)
    ang = pos.astype(jnp.float32)[:, None] * inv[None, :]
    return jnp.cos(ang), jnp.sin(ang)


def _rotate_half_tables(cos, sin):
    zero = jnp.zeros_like(sin)
    return (jnp.concatenate([cos, cos], axis=-1),
            jnp.concatenate([-sin, zero], axis=-1),
            jnp.concatenate([zero, sin], axis=-1))


def _rope_tables(s):
    t = jnp.arange(s, dtype=jnp.int32)
    row = t // GRID_W
    col = t % GRID_W
    half = A_HEAD_DIM // 2
    a_row = _rotate_half_tables(*_rope_cos_sin(row, half, A_ROPE_THETA))
    a_col = _rotate_half_tables(*_rope_cos_sin(col, half, A_ROPE_THETA))
    a_tabs = tuple(jnp.concatenate([r, c], axis=-1) for r, c in zip(a_row, a_col))

    b_rot = _rotate_half_tables(*_rope_cos_sin(t, B_ROPE_DIM, B_ROPE_THETA))
    rest = B_QK_DIM - B_ROPE_DIM
    fill = (jnp.ones((s, rest), jnp.float32), jnp.zeros((s, rest), jnp.float32),
            jnp.zeros((s, rest), jnp.float32))
    b_tabs = tuple(jnp.tile(jnp.concatenate([r, f], axis=-1), (1, 2)) for r, f in zip(b_rot, fill))

    c_rot = _rotate_half_tables(*_rope_cos_sin(t, C_ROPE, C_ROPE_THETA))
    ones = jnp.ones((s, C_NOPE), jnp.float32)
    zeros = jnp.zeros((s, C_NOPE), jnp.float32)
    pad = jnp.zeros((s, C_QK_PAD - C_QK_DIM), jnp.float32)
    c_tabs = (jnp.concatenate([ones, c_rot[0], pad], axis=-1),
              jnp.concatenate([zeros, c_rot[1], pad], axis=-1),
              jnp.concatenate([zeros, c_rot[2], pad], axis=-1))
    return a_tabs, b_tabs, c_tabs


def _score_bound(dim, q_gain, k_gain, scale):
    return BOUND_SLACK * dim * scale * jnp.max(jnp.abs(q_gain)) * jnp.max(jnp.abs(k_gain))


def _pad_vec(g, n):
    return jnp.pad(g, (0, n - g.shape[0]))


def kernel(x, attn_norm, w_in, a_q_norm, a_k_norm, b_q_norm, b_k_norm, b_lambda, b_subln,
           c_q_a_norm, c_kv_a_norm, c_w_q_up, c_w_kv_up, c_q_norm, c_k_norm, w_out,
           ffn_norm, w_gate, w_up, w_down):
    bsz, s, d = x.shape
    assert (bsz, s, d) == (1, SEQ, D_MODEL)
    bf = jnp.bfloat16
    a_tabs, b_tabs, c_tabs = _rope_tables(s)
    tp = 512
    h = x.reshape(s, d)
    w_in_t = jnp.swapaxes(w_in, 1, 2)

    for l in range(DEPTH):
        lambda_init = 0.8 - 0.6 * math.exp(-0.3 * l)
        wq_up = c_w_q_up[l].astype(bf).reshape(C_Q_LORA, C_HEADS, C_QK_DIM)
        wq_up = jnp.pad(wq_up, ((0, 0), (0, 0), (0, C_QK_PAD - C_QK_DIM))).reshape(C_Q_LORA, C_HEADS * C_QK_PAD)
        wkv_up = c_w_kv_up[l].astype(bf)

        xn = _rmsnorm(h, attn_norm[l])
        proj = _matmul([xn], [w_in_t], out_dtype=jnp.float32, tm=1024, tn=PROJ_BLOCK, layer=l,
                       w_transposed=True, n_cols=N_IN_MAIN, name="in_proj")
        c_kr = _tail_proj(xn, w_in_t, layer=l, row0=N_IN_MAIN, tm=1024, name="in_proj_kr")

        def proj_blocks(first, count):
            return [(proj, PROJ_BLOCK, first + b) for b in range(count)]

        a_scale = LOG2E / math.sqrt(A_HEAD_DIM)
        a_q = _prep(proj_blocks(0, 3), _head_pieces(A_HEADS), out_kind="cols", tm=tp,
                    gain=a_q_norm[l], norm="full", n_norm=A_HEAD_DIM, tabs=a_tabs, rope_blocks=(0,),
                    shifts=(96, 32), scale=a_scale, name="prep_a_q")
        a_k = _prep(proj_blocks(3, 1), _head_pieces(A_KV_HEADS), out_kind="rows", tm=tp,
                    gain=a_k_norm[l], norm="full", n_norm=A_HEAD_DIM, tabs=a_tabs, rope_blocks=(0,),
                    shifts=(96, 32), ones_lane=A_HEAD_DIM, name="prep_a_k")
        a_v = _prep(proj_blocks(4, 1), _head_pieces(A_KV_HEADS), out_kind="chunks", tm=ATTN_KEY_CHUNK,
                    name="prep_a_v")
        ya = _attention(a_q, a_k, a_v, _score_bound(A_HEAD_DIM, a_q_norm[l], a_k_norm[l], a_scale),
                        tq=1024, shift_row=A_HEAD_DIM, name="attn_a")

        b_scale = LOG2E / math.sqrt(B_QK_DIM)
        b_q = _prep(proj_blocks(5, 2), _head_pieces(B_HEADS), out_kind="cols", tm=tp,
                    gain=jnp.tile(b_q_norm[l], 2), norm="half", n_norm=B_QK_DIM, tabs=b_tabs,
                    rope_blocks=(0,), shifts=(120, 8), scale=b_scale, name="prep_b_q")
        b_k = _prep(proj_blocks(7, 2), _head_pieces(B_HEADS), out_kind="rows", tm=tp,
                    gain=jnp.tile(b_k_norm[l], 2), norm="half", n_norm=B_QK_DIM, tabs=b_tabs,
                    rope_blocks=(0,), shifts=(120, 8), ones_lane=2 * B_QK_DIM, name="prep_b_k")
        b_v = _prep(proj_blocks(9, 2), _head_pieces(B_HEADS), out_kind="chunks", tm=ATTN_KEY_CHUNK,
                    name="prep_b_v")
        yb = _attention(b_q, b_k, b_v, _score_bound(B_QK_DIM, b_q_norm[l], b_k_norm[l], b_scale),
                        tq=512, shift_row=2 * B_QK_DIM, diff=True, lam_p=b_lambda[l], subln=b_subln[l],
                        lambda_init=lambda_init, name="attn_b")

        cq_lat = _prep(proj_blocks(11, 2), _head_pieces(1, blocks=C_Q_LORA // LANES), out_kind="rows", tm=tp,
                       gain=c_q_a_norm[l], norm="full", n_norm=C_Q_LORA, name="norm_c_q")[0]
        ckv_lat = _prep(proj_blocks(13, 1), _head_pieces(1, blocks=C_KV_LORA // LANES), out_kind="rows", tm=tp,
                        gain=c_kv_a_norm[l], norm="full", n_norm=C_KV_LORA, name="norm_c_kv")[0]
        c_q_raw = _matmul([cq_lat], [wq_up], out_dtype=jnp.float32, tm=1024, tn=512, name="c_q_up")
        c_kv_raw = _matmul([ckv_lat], [wkv_up], out_dtype=jnp.float32, tm=1024, tn=512, name="c_kv_up")
        c_scale = LOG2E / math.sqrt(C_QK_DIM)
        n_up = C_HEADS * C_QK_PAD // PROJ_BLOCK
        c_q = _prep([(c_q_raw, PROJ_BLOCK, b) for b in range(n_up)],
                    _head_pieces(C_HEADS, lanes_per_head=C_QK_PAD, blocks=2), out_kind="cols", tm=tp,
                    gain=_pad_vec(c_q_norm[l], C_QK_PAD), norm="full", n_norm=C_QK_DIM, tabs=c_tabs,
                    rope_blocks=(1,), shifts=(96, 32), scale=c_scale, name="prep_c_q")
        kv_srcs = [(c_kv_raw, PROJ_BLOCK, b) for b in range(n_up)]
        c_k_pieces = [[nope, (n_up, 0)] for (nope,) in _head_pieces(C_HEADS, lanes_per_head=2 * LANES)]
        c_k = _prep(kv_srcs + [(c_kr, LANES, 0)], c_k_pieces, out_kind="rows", tm=tp,
                    gain=_pad_vec(c_k_norm[l], C_QK_PAD), norm="full", n_norm=C_QK_DIM, tabs=c_tabs,
                    rope_blocks=(1,), shifts=(96, 32), ones_lane=C_QK_DIM, name="prep_c_k")
        c_v = _prep(kv_srcs, _head_pieces(C_HEADS, lanes_per_head=2 * LANES, lane0=LANES), out_kind="chunks",
                    tm=ATTN_KEY_CHUNK, name="prep_c_v")
        yc = _attention(c_q, c_k, c_v, _score_bound(C_QK_DIM, c_q_norm[l], c_k_norm[l], c_scale),
                        tq=1024, shift_row=C_QK_DIM, name="attn_c")

        h = _matmul([ya, yb, yc], [w_out], res=h, out_dtype=jnp.float32, tm=1024, tn=512, layer=l,
                    name="out_proj")

        hn = _rmsnorm(h, ffn_norm[l])
        act = _matmul([hn], [w_gate, w_up], out_dtype=bf, tm=1024, tn=256, layer=l, name="ffn_gate_up")
        h = _matmul_kouter(act, w_down, h, layer=l, tm=1024, tn=256, tk=FFN_HIDDEN // 2, n_split=2,
                           name="ffn_down")

    return h.reshape(bsz, s, d)
```

```python
import functools
import math

import jax
import jax.numpy as jnp
from jax import lax
from jax.experimental import pallas as pl
from jax.experimental.pallas import tpu as pltpu

D_MODEL = 4096
SEQ = 8192
DEPTH = 2
GRID_W = 64
EPS = 1e-6

A_HEAD_DIM = 128
A_HEADS = 12
A_KV_HEADS = 4
A_ROPE_THETA = 10000.0

B_QK_DIM = 64
B_V_DIM = 128
B_HEADS = 8
B_ROPE_DIM = 16
B_ROPE_THETA = 500000.0

C_V_DIM = 128
C_HEADS = 12
C_Q_LORA = 1024
C_KV_LORA = 512
C_NOPE = 128
C_ROPE = 64
C_QK_DIM = C_NOPE + C_ROPE
C_QK_PAD = 256
C_ROPE_THETA = 10000.0

FFN_HIDDEN = 11008

LANES = 128
SUBLANES = 8
ATTN_ROW_BLOCK = 64
ATTN_KEY_CHUNK = 512
ATTN_CHUNKS_PER_TRIP = 8
ATTN_MIN_DENOM = 2.0 ** -60
BOUND_SLACK = 1.01
MXU_SUM_MAX_WIDTH = 256
ONES_ROWS = 16
VMEM_LIMIT_BYTES = 56 * 1024 * 1024

LOG2E = math.log2(math.e)

PROJ_BLOCK = 512
N_IN_MAIN = 14 * PROJ_BLOCK


def _head_pieces(n_heads, lanes_per_head=LANES, lane0=0, blocks=1):
    out = []
    for h in range(n_heads):
        head = []
        for b in range(blocks):
            col = lane0 + h * lanes_per_head + b * LANES
            head.append((col // PROJ_BLOCK, col % PROJ_BLOCK))
        out.append(head)
    return out


def _cparams(sem):
    return pltpu.CompilerParams(dimension_semantics=sem, vmem_limit_bytes=VMEM_LIMIT_BYTES)


def _rmsnorm_body(x_ref, g_ref, o_ref):
    x = x_ref[...]
    ms = jnp.mean(x * x, axis=-1, keepdims=True)
    o_ref[...] = (x * lax.rsqrt(ms + EPS) * g_ref[...]).astype(o_ref.dtype)


def _rmsnorm(x, g, tm=256):
    s, d = x.shape
    return pl.pallas_call(
        _rmsnorm_body,
        grid=(s // tm,),
        in_specs=[pl.BlockSpec((tm, d), lambda i: (i, 0)),
                  pl.BlockSpec((1, d), lambda i: (0, 0))],
        out_specs=pl.BlockSpec((tm, d), lambda i: (i, 0)),
        out_shape=jax.ShapeDtypeStruct((s, d), jnp.bfloat16),
        compiler_params=_cparams(("parallel",)),
        name="rmsnorm",
    )(x, g.reshape(1, d))


def _matmul_body(*refs, n_x, n_w, has_res, w_transposed):
    x_refs = refs[:n_x]
    w_refs = refs[n_x:n_x + n_w]
    pos = n_x + n_w
    res_ref = refs[pos] if has_res else None
    o_ref = refs[pos + int(has_res)]

    x = x_refs[0][...] if n_x == 1 else jnp.concatenate([r[...] for r in x_refs], axis=1)
    tiles = [w[...] if w.dtype == x.dtype else w[...].astype(x.dtype) for w in w_refs]
    contract = (((1,), (1,)), ((), ())) if w_transposed else (((1,), (0,)), ((), ()))
    prods = [lax.dot_general(x, w, contract, preferred_element_type=jnp.float32) for w in tiles]
    if n_w == 2:
        g, u = prods
        y = g * jax.nn.sigmoid(g) * u
    else:
        y = prods[0]
    if has_res:
        y = y + res_ref[...]
    o_ref[...] = y.astype(o_ref.dtype)


def _matmul(xs, ws, res=None, *, out_dtype, tm, tn, name, layer=None, w_transposed=False, n_cols=None):
    m = xs[0].shape[0]
    kdim = sum(x.shape[1] for x in xs)
    n = n_cols if n_cols is not None else ws[0].shape[-2 if w_transposed else -1]
    assert m % tm == 0 and n % tn == 0
    in_specs = [pl.BlockSpec((tm, x.shape[1]), lambda i, j: (i, 0)) for x in xs]
    w_block = (tn, kdim) if w_transposed else (kdim, tn)
    lead = () if layer is None else (None,)

    def w_index(i, j):
        pos = (j, 0) if w_transposed else (0, j)
        return pos if layer is None else (layer,) + pos

    in_specs += [pl.BlockSpec(lead + w_block, w_index) for _ in ws]
    args = [*xs, *ws]
    if res is not None:
        in_specs.append(pl.BlockSpec((tm, tn), lambda i, j: (i, j)))
        args.append(res)
    return pl.pallas_call(
        functools.partial(_matmul_body, n_x=len(xs), n_w=len(ws), has_res=res is not None,
                          w_transposed=w_transposed),
        grid=(m // tm, n // tn),
        in_specs=in_specs,
        out_specs=pl.BlockSpec((tm, tn), lambda i, j: (i, j)),
        out_shape=jax.ShapeDtypeStruct((m, n), out_dtype),
        compiler_params=_cparams(("parallel", "parallel")),
        name=name,
    )(*args)


def _tail_proj_body(x_ref, w_ref, o_ref):
    w = w_ref[...].astype(x_ref.dtype)
    w = jnp.concatenate([w, jnp.zeros_like(w)], axis=0)
    o_ref[...] = lax.dot_general(x_ref[...], w, (((1,), (1,)), ((), ())),
                                 preferred_element_type=jnp.float32)


def _tail_proj(x, w_t, *, layer, row0, tm, name):
    m, kdim = x.shape
    rows = w_t.shape[1] - row0
    assert rows * 2 == LANES and row0 % rows == 0
    return pl.pallas_call(
        _tail_proj_body,
        grid=(m // tm,),
        in_specs=[pl.BlockSpec((tm, kdim), lambda i: (i, 0)),
                  pl.BlockSpec((None, rows, kdim), lambda i: (layer, row0 // rows, 0))],
        out_specs=pl.BlockSpec((tm, LANES), lambda i: (i, 0)),
        out_shape=jax.ShapeDtypeStruct((m, LANES), jnp.float32),
        compiler_params=_cparams(("parallel",)),
        name=name,
    )(x, w_t)


def _matmul_kouter_body(x_ref, w_ref, res_ref, o_ref, acc_ref, *, nk):
    k = pl.program_id(2)
    j = pl.program_id(3)
    prod = jnp.dot(x_ref[...], w_ref[...].astype(x_ref.dtype), preferred_element_type=jnp.float32)

    @pl.when(k == 0)
    def _():
        acc_ref[j] = prod

    if nk > 2:
        @pl.when(jnp.logical_and(k > 0, k < nk - 1))
        def _():
            acc_ref[j] += prod

    @pl.when(k == nk - 1)
    def _():
        o_ref[...] = (acc_ref[j] + prod + res_ref[...]).astype(o_ref.dtype)


def _matmul_kouter(x, w, res, *, layer, tm, tn, tk, n_split, name):
    m, kdim = x.shape
    n = w.shape[-1]
    nk = kdim // tk
    nj = n // (tn * n_split)
    assert m % tm == 0 and n % (tn * n_split) == 0 and kdim % tk == 0 and nk >= 2

    def out_index(hf, i, k, j):
        return i, hf * nj + jnp.where(k == nk - 1, j, 0)

    return pl.pallas_call(
        functools.partial(_matmul_kouter_body, nk=nk),
        grid=(n_split, m // tm, nk, nj),
        in_specs=[pl.BlockSpec((tm, tk), lambda hf, i, k, j: (i, k)),
                  pl.BlockSpec((None, tk, tn), lambda hf, i, k, j: (layer, k, hf * nj + j)),
                  pl.BlockSpec((tm, tn), out_index)],
        out_specs=pl.BlockSpec((tm, tn), out_index),
        out_shape=jax.ShapeDtypeStruct((m, n), res.dtype),
        scratch_shapes=[pltpu.VMEM((nj, tm, tn), jnp.float32)],
        compiler_params=_cparams(("parallel", "parallel", "arbitrary", "arbitrary")),
        name=name,
    )(x, w, res)


def _prep_head(x, g_ref, tabs, *, norm, n_norm, rope_blocks, shifts, scale, transpose, ones_rows):
    width = x.shape[-1]
    if norm is not None:
        if width <= MXU_SUM_MAX_WIDTH:
            r = lax.broadcasted_iota(jnp.int32, (width, width), 0)
            c = lax.broadcasted_iota(jnp.int32, (width, width), 1)
            group = LANES // 2 if norm == "half" else width
            same = (r // group == c // group).astype(jnp.bfloat16)
            ss = jnp.dot((x * x).astype(jnp.bfloat16), same, preferred_element_type=jnp.float32)
        else:
            ss = jnp.sum(x * x, axis=-1, keepdims=True)
        x = x * lax.rsqrt(ss * (1.0 / n_norm) + EPS) * g_ref[...]

    if rope_blocks:
        c_ref, s1_ref, s2_ref = tabs
        blocks = []
        for b in range(width // LANES):
            xb = x[:, b * LANES:(b + 1) * LANES]
            if b in rope_blocks:
                sl = slice(b * LANES, (b + 1) * LANES)
                xb = (xb * c_ref[:, sl]
                      + pltpu.roll(xb, shifts[0], 1) * s1_ref[:, sl]
                      + pltpu.roll(xb, shifts[1], 1) * s2_ref[:, sl])
            blocks.append(xb)
        x = blocks[0] if len(blocks) == 1 else jnp.concatenate(blocks, axis=-1)

    if scale != 1.0:
        x = x * scale
    x = x.astype(jnp.bfloat16)
    if transpose:
        eye = (lax.broadcasted_iota(jnp.int32, (width, width), 0)
               == lax.broadcasted_iota(jnp.int32, (width, width), 1)).astype(jnp.bfloat16)
        x = lax.dot_general(eye, x, (((1,), (1,)), ((), ())),
                            preferred_element_type=jnp.float32).astype(jnp.bfloat16)
    if ones_rows:
        x = jnp.concatenate([x, jnp.ones((ones_rows, x.shape[1]), x.dtype)], axis=0)
    return x


def _prep_body(*refs, nx, pieces, has_gain, has_tabs, **head_kw):
    x_refs = refs[:nx]
    pos = nx
    g_ref = None
    if has_gain:
        g_ref = refs[pos]
        pos += 1
    tabs = None
    if has_tabs:
        tabs = refs[pos:pos + 3]
        pos += 3
    o_ref = refs[pos]
    for h, head_pieces in enumerate(pieces):
        cols = [x_refs[src][:, off:off + LANES] for src, off in head_pieces]
        x = cols[0] if len(cols) == 1 else jnp.concatenate(cols, axis=-1)
        y = _prep_head(x, g_ref, tabs, **head_kw)
        o_ref[h] = y.reshape(o_ref.shape[1:])


def _prep(srcs, pieces, *, out_kind, tm, gain=None, norm=None, n_norm=None, tabs=None,
          rope_blocks=(), shifts=(0, 0), scale=1.0, name="prep"):
    s = srcs[0][0].shape[0]
    n_heads = len(pieces)
    width = LANES * len(pieces[0])
    in_specs, args = [], []
    for arr, bw, idx in srcs:
        in_specs.append(pl.BlockSpec((tm, bw), functools.partial(lambda i, idx: (i, idx), idx=idx)))
        args.append(arr)
    if norm is not None:
        in_specs.append(pl.BlockSpec((1, width), lambda i: (0, 0)))
        args.append(gain.reshape(1, width))
    if rope_blocks:
        for t in tabs:
            in_specs.append(pl.BlockSpec((tm, width), lambda i: (i, 0)))
            args.append(t)
    if out_kind == "rows":
        out_shape = (n_heads, s, width)
        out_spec = pl.BlockSpec((n_heads, tm, width), lambda i: (0, i, 0))
    elif out_kind == "cols":
        out_shape = (n_heads, s // tm, width, tm)
        out_spec = pl.BlockSpec((n_heads, 1, width, tm), lambda i: (0, i, 0, 0))
    else:
        out_shape = (n_heads, s // tm, width + ONES_ROWS, tm)
        out_spec = pl.BlockSpec((n_heads, 1, width + ONES_ROWS, tm), lambda i: (0, i, 0, 0))
    return pl.pallas_call(
        functools.partial(_prep_body, nx=len(srcs), pieces=tuple(tuple(p) for p in pieces),
                          has_gain=norm is not None, has_tabs=bool(rope_blocks),
                          norm=norm, n_norm=n_norm, rope_blocks=tuple(rope_blocks), shifts=shifts,
                          scale=scale, transpose=out_kind != "rows",
                          ones_rows=ONES_ROWS if out_kind == "chunks" else 0),
        grid=(s // tm,),
        in_specs=in_specs,
        out_specs=out_spec,
        out_shape=jax.ShapeDtypeStruct(out_shape, jnp.bfloat16),
        compiler_params=_cparams(("parallel",)),
        name=name,
    )(*args)


def _attn_body(*refs, nchunk, tk, tq, diff, lambda_init):
    if diff:
        (qT_ref, k_ref, vT_ref, shift_ref, lam_ref, sub_ref, o_ref,
         acc_ref, p_ref, qq_ref, m_ref, s_ref, ps_ref) = refs
    else:
        qT_ref, k_ref, vT_ref, shift_ref, o_ref, acc_ref, p_ref, qq_ref, m_ref, s_ref, ps_ref = refs

    i = pl.program_id(1)
    nqb, _, nq = qq_ref.shape
    dv = o_ref.shape[1]
    tiles_per_block = tq // qT_ref.shape[3]
    chunk_bits = nchunk.bit_length() - 1
    assert nchunk == 1 << chunk_bits

    def scores(step):
        chunk = step & (nchunk - 1)
        block = jnp.minimum(i + (step >> chunk_bits), nqb - 1)
        start = pl.multiple_of(chunk * tk, tk)
        return jnp.dot(k_ref[0, pl.ds(start, tk), :], qq_ref[block],
                       preferred_element_type=jnp.float32)

    def probs(step):
        return jnp.exp2(scores(step) - shift_ref[...]).astype(jnp.bfloat16)

    @pl.when(i == 0)
    def _():
        for b in range(nqb):
            tiles = [qT_ref[0, b * tiles_per_block + c] for c in range(tiles_per_block)]
            q = tiles[0] if len(tiles) == 1 else jnp.concatenate(tiles, axis=1)
            if diff:
                first = lax.broadcasted_iota(jnp.int32, q.shape, 0) < B_QK_DIM
                zero = jnp.zeros_like(q)
                q = jnp.concatenate([jnp.where(first, q, zero), jnp.where(first, zero, q)], axis=1)
            qq_ref[b] = q
        p_ref[0] = probs(0)

    acc_ref[...] = jnp.zeros(acc_ref.shape, jnp.float32)

    def trip(t, carry):
        for u in range(ATTN_CHUNKS_PER_TRIP):
            j = t * ATTN_CHUNKS_PER_TRIP + u
            p_ref[(u + 1) % 2] = probs(j + 1)
            acc_ref[...] += jnp.dot(vT_ref[0, j], p_ref[u % 2], preferred_element_type=jnp.float32)
        return carry

    lax.fori_loop(0, nchunk // ATTN_CHUNKS_PER_TRIP, trip, 0)

    healthy = jnp.min(acc_ref[dv:dv + 1, :]) >= ATTN_MIN_DENOM

    @pl.when(jnp.logical_not(healthy))
    def _():
        rb = ATTN_ROW_BLOCK
        groups = rb // SUBLANES
        m_ref[...] = jnp.full(m_ref.shape, -jnp.inf, jnp.float32)
        acc_ref[...] = jnp.zeros(acc_ref.shape, jnp.float32)

        def block(r):
            return s_ref[r * rb:(r + 1) * rb, :].reshape(groups, SUBLANES, nq)

        def chunk(j, carry):
            s_ref[...] = scores(j)
            mx = jnp.max(block(0), axis=0)
            for r in range(1, tk // rb):
                mx = jnp.maximum(mx, jnp.max(block(r), axis=0))
            m_old = m_ref[...]
            m_new = jnp.maximum(m_old, jnp.max(mx, axis=0, keepdims=True))
            alpha = jnp.exp2(m_old - m_new)
            m_rows = jnp.broadcast_to(m_new, (SUBLANES, nq))[None]
            for r in range(tk // rb):
                p = jnp.exp2(block(r) - m_rows).reshape(rb, nq)
                ps_ref[r * rb:(r + 1) * rb, :] = p.astype(jnp.bfloat16)
            pv = jnp.dot(vT_ref[0, j], ps_ref[...], preferred_element_type=jnp.float32)
            acc_ref[...] = acc_ref[...] * alpha + pv
            m_ref[...] = m_new
            return carry

        lax.fori_loop(0, nchunk, chunk, 0)

    o = acc_ref[:dv, :] / acc_ref[dv:dv + 1, :]
    if diff:
        lp = lam_ref[...]
        lam = (jnp.exp(jnp.sum(lp[0:1] * lp[1:2], axis=-1, keepdims=True))
               - jnp.exp(jnp.sum(lp[2:3] * lp[3:4], axis=-1, keepdims=True))
               + lambda_init)
        o = o[:, :tq] - lam * o[:, tq:]
        ms = jnp.mean(o * o, axis=0, keepdims=True)
        o = o * lax.rsqrt(ms + EPS) * sub_ref[...] * (1.0 - lambda_init)
    o_ref[...] = o.T.astype(o_ref.dtype)


def _attention(qT, k, vT, score_bound, *, tq, diff=False, lam_p=None, subln=None, lambda_init=0.0, name):
    n_heads, n_qtiles, dq, q_tile = qT.shape
    s = n_qtiles * q_tile
    n_kv = k.shape[0]
    group = n_heads // n_kv
    _, nchunk, dv_ext, tk = vT.shape
    dv = dv_ext - ONES_ROWS
    nq = 2 * tq if diff else tq
    assert tq % q_tile == 0 and ATTN_CHUNKS_PER_TRIP % 2 == 0 and nchunk % ATTN_CHUNKS_PER_TRIP == 0
    in_specs = [pl.BlockSpec((1, n_qtiles, dq, q_tile), lambda h, i: (h, 0, 0, 0)),
                pl.BlockSpec((1, s, dq), lambda h, i: (h // group, 0, 0)),
                pl.BlockSpec((1, nchunk, dv_ext, tk), lambda h, i: (h // group, 0, 0, 0)),
                pl.BlockSpec((1, nq), lambda h, i: (0, 0))]
    args = [qT, k, vT, jnp.full((1, nq), score_bound, jnp.float32)]
    if diff:
        in_specs += [pl.BlockSpec(lam_p.shape, lambda h, i: (0, 0)),
                     pl.BlockSpec((dv, 1), lambda h, i: (0, 0))]
        args += [lam_p, subln.reshape(dv, 1)]
    return pl.pallas_call(
        functools.partial(_attn_body, nchunk=nchunk, tk=tk, tq=tq, diff=diff, lambda_init=lambda_init),
        grid=(n_heads, s // tq),
        in_specs=in_specs,
        out_specs=pl.BlockSpec((tq, dv), lambda h, i: (i, h)),
        out_shape=jax.ShapeDtypeStruct((s, n_heads * dv), jnp.bfloat16),
        scratch_shapes=[pltpu.VMEM((dv_ext, nq), jnp.float32),
                        pltpu.VMEM((2, tk, nq), jnp.bfloat16),
                        pltpu.VMEM((s // tq, dq, nq), jnp.bfloat16),
                        pltpu.VMEM((1, nq), jnp.float32),
                        pltpu.VMEM((tk, nq), jnp.float32),
                        pltpu.VMEM((tk, nq), jnp.bfloat16)],
        compiler_params=_cparams(("arbitrary", "arbitrary")),
        name=name,
    )(*args)


def _rope_cos_sin(pos, dim, theta):
    inv = theta ** (-jnp.arange(0, dim, 2, dtype=jnp.float32) / dim)
    ang = pos.astype(jnp.float32)[:, None] * inv[None, :]
    return jnp.cos(ang), jnp.sin(ang)


def _rotate_half_tables(cos, sin):
    zero = jnp.zeros_like(sin)
    return (jnp.concatenate([cos, cos], axis=-1),
            jnp.concatenate([-sin, zero], axis=-1),
            jnp.concatenate([zero, sin], axis=-1))


def _rope_tables(s):
    t = jnp.arange(s, dtype=jnp.int32)
    row = t // GRID_W
    col = t % GRID_W
    half = A_HEAD_DIM // 2
    a_row = _rotate_half_tables(*_rope_cos_sin(row, half, A_ROPE_THETA))
    a_col = _rotate_half_tables(*_rope_cos_sin(col, half, A_ROPE_THETA))
    a_tabs = tuple(jnp.concatenate([r, c], axis=-1) for r, c in zip(a_row, a_col))

    b_rot = _rotate_half_tables(*_rope_cos_sin(t, B_ROPE_DIM, B_ROPE_THETA))
    rest = B_QK_DIM - B_ROPE_DIM
    fill = (jnp.ones((s, rest), jnp.float32), jnp.zeros((s, rest), jnp.float32),
            jnp.zeros((s, rest), jnp.float32))
    b_tabs = tuple(jnp.tile(jnp.concatenate([r, f], axis=-1), (1, 2)) for r, f in zip(b_rot, fill))

    c_rot = _rotate_half_tables(*_rope_cos_sin(t, C_ROPE, C_ROPE_THETA))
    ones = jnp.ones((s, C_NOPE), jnp.float32)
    zeros = jnp.zeros((s, C_NOPE), jnp.float32)
    pad = jnp.zeros((s, C_QK_PAD - C_QK_DIM), jnp.float32)
    c_tabs = (jnp.concatenate([ones, c_rot[0], pad], axis=-1),
              jnp.concatenate([zeros, c_rot[1], pad], axis=-1),
              jnp.concatenate([zeros, c_rot[2], pad], axis=-1))
    return a_tabs, b_tabs, c_tabs


def _score_bound(dim, q_gain, k_gain, scale):
    return BOUND_SLACK * dim * scale * jnp.max(jnp.abs(q_gain)) * jnp.max(jnp.abs(k_gain))


def _pad_vec(g, n):
    return jnp.pad(g, (0, n - g.shape[0]))


def kernel(x, attn_norm, w_in, a_q_norm, a_k_norm, b_q_norm, b_k_norm, b_lambda, b_subln,
           c_q_a_norm, c_kv_a_norm, c_w_q_up, c_w_kv_up, c_q_norm, c_k_norm, w_out,
           ffn_norm, w_gate, w_up, w_down):
    bsz, s, d = x.shape
    assert (bsz, s, d) == (1, SEQ, D_MODEL)
    bf = jnp.bfloat16
    a_tabs, b_tabs, c_tabs = _rope_tables(s)
    tp = 512
    h = x.reshape(s, d)
    w_in_t = jnp.swapaxes(w_in, 1, 2)

    for l in range(DEPTH):
        lambda_init = 0.8 - 0.6 * math.exp(-0.3 * l)
        wq_up = c_w_q_up[l].astype(bf).reshape(C_Q_LORA, C_HEADS, C_QK_DIM)
        wq_up = jnp.pad(wq_up, ((0, 0), (0, 0), (0, C_QK_PAD - C_QK_DIM))).reshape(C_Q_LORA, C_HEADS * C_QK_PAD)
        wkv_up = c_w_kv_up[l].astype(bf)

        xn = _rmsnorm(h, attn_norm[l])
        proj = _matmul([xn], [w_in_t], out_dtype=jnp.float32, tm=1024, tn=PROJ_BLOCK, layer=l,
                       w_transposed=True, n_cols=N_IN_MAIN, name="in_proj")
        c_kr = _tail_proj(xn, w_in_t, layer=l, row0=N_IN_MAIN, tm=1024, name="in_proj_kr")

        def proj_blocks(first, count):
            return [(proj, PROJ_BLOCK, first + b) for b in range(count)]

        a_scale = LOG2E / math.sqrt(A_HEAD_DIM)
        a_q = _prep(proj_blocks(0, 3), _head_pieces(A_HEADS), out_kind="cols", tm=tp,
                    gain=a_q_norm[l], norm="full", n_norm=A_HEAD_DIM, tabs=a_tabs, rope_blocks=(0,),
                    shifts=(96, 32), scale=a_scale, name="prep_a_q")
        a_k = _prep(proj_blocks(3, 1), _head_pieces(A_KV_HEADS), out_kind="rows", tm=tp,
                    gain=a_k_norm[l], norm="full", n_norm=A_HEAD_DIM, tabs=a_tabs, rope_blocks=(0,),
                    shifts=(96, 32), name="prep_a_k")
        a_v = _prep(proj_blocks(4, 1), _head_pieces(A_KV_HEADS), out_kind="chunks", tm=ATTN_KEY_CHUNK,
                    name="prep_a_v")
        ya = _attention(a_q, a_k, a_v, _score_bound(A_HEAD_DIM, a_q_norm[l], a_k_norm[l], a_scale),
                        tq=1024, name="attn_a")

        b_scale = LOG2E / math.sqrt(B_QK_DIM)
        b_q = _prep(proj_blocks(5, 2), _head_pieces(B_HEADS), out_kind="cols", tm=tp,
                    gain=jnp.tile(b_q_norm[l], 2), norm="half", n_norm=B_QK_DIM, tabs=b_tabs,
                    rope_blocks=(0,), shifts=(120, 8), scale=b_scale, name="prep_b_q")
        b_k = _prep(proj_blocks(7, 2), _head_pieces(B_HEADS), out_kind="rows", tm=tp,
                    gain=jnp.tile(b_k_norm[l], 2), norm="half", n_norm=B_QK_DIM, tabs=b_tabs,
                    rope_blocks=(0,), shifts=(120, 8), name="prep_b_k")
        b_v = _prep(proj_blocks(9, 2), _head_pieces(B_HEADS), out_kind="chunks", tm=ATTN_KEY_CHUNK,
                    name="prep_b_v")
        yb = _attention(b_q, b_k, b_v, _score_bound(B_QK_DIM, b_q_norm[l], b_k_norm[l], b_scale),
                        tq=512, diff=True, lam_p=b_lambda[l], subln=b_subln[l],
                        lambda_init=lambda_init, name="attn_b")

        cq_lat = _prep(proj_blocks(11, 2), _head_pieces(1, blocks=C_Q_LORA // LANES), out_kind="rows", tm=tp,
                       gain=c_q_a_norm[l], norm="full", n_norm=C_Q_LORA, name="norm_c_q")[0]
        ckv_lat = _prep(proj_blocks(13, 1), _head_pieces(1, blocks=C_KV_LORA // LANES), out_kind="rows", tm=tp,
                        gain=c_kv_a_norm[l], norm="full", n_norm=C_KV_LORA, name="norm_c_kv")[0]
        c_q_raw = _matmul([cq_lat], [wq_up], out_dtype=jnp.float32, tm=1024, tn=1536, name="c_q_up")
        c_kv_raw = _matmul([ckv_lat], [wkv_up], out_dtype=jnp.float32, tm=1024, tn=1536, name="c_kv_up")
        c_scale = LOG2E / math.sqrt(C_QK_DIM)
        n_up = C_HEADS * C_QK_PAD // PROJ_BLOCK
        c_q = _prep([(c_q_raw, PROJ_BLOCK, b) for b in range(n_up)],
                    _head_pieces(C_HEADS, lanes_per_head=C_QK_PAD, blocks=2), out_kind="cols", tm=tp,
                    gain=_pad_vec(c_q_norm[l], C_QK_PAD), norm="full", n_norm=C_QK_DIM, tabs=c_tabs,
                    rope_blocks=(1,), shifts=(96, 32), scale=c_scale, name="prep_c_q")
        kv_srcs = [(c_kv_raw, PROJ_BLOCK, b) for b in range(n_up)]
        c_k_pieces = [[nope, (n_up, 0)] for (nope,) in _head_pieces(C_HEADS, lanes_per_head=2 * LANES)]
        c_k = _prep(kv_srcs + [(c_kr, LANES, 0)], c_k_pieces, out_kind="rows", tm=tp,
                    gain=_pad_vec(c_k_norm[l], C_QK_PAD), norm="full", n_norm=C_QK_DIM, tabs=c_tabs,
                    rope_blocks=(1,), shifts=(96, 32), name="prep_c_k")
        c_v = _prep(kv_srcs, _head_pieces(C_HEADS, lanes_per_head=2 * LANES, lane0=LANES), out_kind="chunks",
                    tm=ATTN_KEY_CHUNK, name="prep_c_v")
        yc = _attention(c_q, c_k, c_v, _score_bound(C_QK_DIM, c_q_norm[l], c_k_norm[l], c_scale),
                        tq=1024, name="attn_c")

        h = _matmul([ya, yb, yc], [w_out], res=h, out_dtype=jnp.float32, tm=1024, tn=512, layer=l,
                    name="out_proj")

        hn = _rmsnorm(h, ffn_norm[l])
        act = _matmul([hn], [w_gate, w_up], out_dtype=bf, tm=1024, tn=256, layer=l, name="ffn_gate_up")
        h = _matmul_kouter(act, w_down, h, layer=l, tm=1024, tn=256, tk=FFN_HIDDEN // 2, n_split=2,
                           name="ffn_down")

    return h.reshape(bsz, s, d)
```

```python
import functools
import math

import jax
import jax.numpy as jnp
from jax import lax
from jax.experimental import pallas as pl
from jax.experimental.pallas import tpu as pltpu

D_MODEL = 4096
SEQ = 8192
DEPTH = 2
GRID_W = 64
EPS = 1e-6

A_HEAD_DIM = 128
A_HEADS = 12
A_KV_HEADS = 4
A_ROPE_THETA = 10000.0

B_QK_DIM = 64
B_V_DIM = 128
B_HEADS = 8
B_ROPE_DIM = 16
B_ROPE_THETA = 500000.0

C_V_DIM = 128
C_HEADS = 12
C_Q_LORA = 1024
C_KV_LORA = 512
C_NOPE = 128
C_ROPE = 64
C_QK_DIM = C_NOPE + C_ROPE
C_QK_PAD = 256
C_ROPE_THETA = 10000.0

FFN_HIDDEN = 11008

LANES = 128
SUBLANES = 8
ATTN_ROW_BLOCK = 64
ATTN_KEY_CHUNK = 512
ATTN_CHUNKS_PER_TRIP = 8
ATTN_MIN_DENOM = 2.0 ** -60
BOUND_SLACK = 1.01
MXU_SUM_MAX_WIDTH = 256
ONES_ROWS = 16
VMEM_LIMIT_BYTES = 56 * 1024 * 1024
LOG2E = math.log2(math.e)

PROJ_BLOCK = 512
N_IN_MAIN = 14 * PROJ_BLOCK


def _head_pieces(n_heads, lanes_per_head=LANES, lane0=0, blocks=1):
    out = []
    for h in range(n_heads):
        head = []
        for b in range(blocks):
            col = lane0 + h * lanes_per_head + b * LANES
            head.append((col // PROJ_BLOCK, col % PROJ_BLOCK))
        out.append(head)
    return out


def _cparams(sem):
    return pltpu.CompilerParams(dimension_semantics=sem, vmem_limit_bytes=VMEM_LIMIT_BYTES)


def _rmsnorm_body(x_ref, g_ref, o_ref):
    x = x_ref[...]
    ms = jnp.mean(x * x, axis=-1, keepdims=True)
    o_ref[...] = (x * lax.rsqrt(ms + EPS) * g_ref[...]).astype(o_ref.dtype)


def _rmsnorm(x, g, tm=256):
    s, d = x.shape
    return pl.pallas_call(
        _rmsnorm_body,
        grid=(s // tm,),
        in_specs=[pl.BlockSpec((tm, d), lambda i: (i, 0)),
                  pl.BlockSpec((1, d), lambda i: (0, 0))],
        out_specs=pl.BlockSpec((tm, d), lambda i: (i, 0)),
        out_shape=jax.ShapeDtypeStruct((s, d), jnp.bfloat16),
        compiler_params=_cparams(("parallel",)),
        name="rmsnorm",
    )(x, g.reshape(1, d))


def _matmul_body(*refs, n_x, n_w, has_res, w_transposed):
    x_refs = refs[:n_x]
    w_refs = refs[n_x:n_x + n_w]
    pos = n_x + n_w
    res_ref = refs[pos] if has_res else None
    o_ref = refs[pos + int(has_res)]

    x = x_refs[0][...] if n_x == 1 else jnp.concatenate([r[...] for r in x_refs], axis=1)
    tiles = [w[...] if w.dtype == x.dtype else w[...].astype(x.dtype) for w in w_refs]
    contract = (((1,), (1,)), ((), ())) if w_transposed else (((1,), (0,)), ((), ()))
    prods = [lax.dot_general(x, w, contract, preferred_element_type=jnp.float32) for w in tiles]
    if n_w == 2:
        g, u = prods
        y = g * jax.nn.sigmoid(g) * u
    else:
        y = prods[0]
    if has_res:
        y = y + res_ref[...]
    o_ref[...] = y.astype(o_ref.dtype)


def _matmul(xs, ws, res=None, *, out_dtype, tm, tn, name, layer=None, w_transposed=False, n_cols=None):
    m = xs[0].shape[0]
    kdim = sum(x.shape[1] for x in xs)
    n = n_cols if n_cols is not None else ws[0].shape[-2 if w_transposed else -1]
    assert m % tm == 0 and n % tn == 0
    in_specs = [pl.BlockSpec((tm, x.shape[1]), lambda i, j: (i, 0)) for x in xs]
    w_block = (tn, kdim) if w_transposed else (kdim, tn)
    lead = () if layer is None else (None,)

    def w_index(i, j):
        pos = (j, 0) if w_transposed else (0, j)
        return pos if layer is None else (layer,) + pos

    in_specs += [pl.BlockSpec(lead + w_block, w_index) for _ in ws]
    args = [*xs, *ws]
    if res is not None:
        in_specs.append(pl.BlockSpec((tm, tn), lambda i, j: (i, j)))
        args.append(res)
    return pl.pallas_call(
        functools.partial(_matmul_body, n_x=len(xs), n_w=len(ws), has_res=res is not None,
                          w_transposed=w_transposed),
        grid=(m // tm, n // tn),
        in_specs=in_specs,
        out_specs=pl.BlockSpec((tm, tn), lambda i, j: (i, j)),
        out_shape=jax.ShapeDtypeStruct((m, n), out_dtype),
        compiler_params=_cparams(("parallel", "parallel")),
        name=name,
    )(*args)


def _tail_proj_body(x_ref, w_ref, o_ref):
    w = w_ref[...].astype(x_ref.dtype)
    w = jnp.concatenate([w, jnp.zeros_like(w)], axis=0)
    o_ref[...] = lax.dot_general(x_ref[...], w, (((1,), (1,)), ((), ())),
                                 preferred_element_type=jnp.float32)


def _tail_proj(x, w_t, *, layer, row0, tm, name):
    m, kdim = x.shape
    rows = w_t.shape[1] - row0
    assert rows * 2 == LANES and row0 % rows == 0
    return pl.pallas_call(
        _tail_proj_body,
        grid=(m // tm,),
        in_specs=[pl.BlockSpec((tm, kdim), lambda i: (i, 0)),
                  pl.BlockSpec((None, rows, kdim), lambda i: (layer, row0 // rows, 0))],
        out_specs=pl.BlockSpec((tm, LANES), lambda i: (i, 0)),
        out_shape=jax.ShapeDtypeStruct((m, LANES), jnp.float32),
        compiler_params=_cparams(("parallel",)),
        name=name,
    )(x, w_t)


def _matmul_kouter_body(x_ref, w_ref, res_ref, o_ref, acc_ref, *, nk):
    k = pl.program_id(2)
    j = pl.program_id(3)
    prod = jnp.dot(x_ref[...], w_ref[...].astype(x_ref.dtype), preferred_element_type=jnp.float32)

    @pl.when(k == 0)
    def _():
        acc_ref[j] = prod

    if nk > 2:
        @pl.when(jnp.logical_and(k > 0, k < nk - 1))
        def _():
            acc_ref[j] += prod

    @pl.when(k == nk - 1)
    def _():
        o_ref[...] = (acc_ref[j] + prod + res_ref[...]).astype(o_ref.dtype)


def _matmul_kouter(x, w, res, *, layer, tm, tn, tk, n_split, name):
    m, kdim = x.shape
    n = w.shape[-1]
    nk = kdim // tk
    nj = n // (tn * n_split)
    assert m % tm == 0 and n % (tn * n_split) == 0 and kdim % tk == 0 and nk >= 2

    def out_index(hf, i, k, j):
        return i, hf * nj + jnp.where(k == nk - 1, j, 0)

    return pl.pallas_call(
        functools.partial(_matmul_kouter_body, nk=nk),
        grid=(n_split, m // tm, nk, nj),
        in_specs=[pl.BlockSpec((tm, tk), lambda hf, i, k, j: (i, k)),
                  pl.BlockSpec((None, tk, tn), lambda hf, i, k, j: (layer, k, hf * nj + j)),
                  pl.BlockSpec((tm, tn), out_index)],
        out_specs=pl.BlockSpec((tm, tn), out_index),
        out_shape=jax.ShapeDtypeStruct((m, n), res.dtype),
        scratch_shapes=[pltpu.VMEM((nj, tm, tn), jnp.float32)],
        compiler_params=_cparams(("parallel", "parallel", "arbitrary", "arbitrary")),
        name=name,
    )(x, w, res)


def _prep_head(x, g_ref, tabs, *, norm, n_norm, rope_blocks, shifts, scale, transpose, ones_rows):
    width = x.shape[-1]
    if norm is not None:
        if width <= MXU_SUM_MAX_WIDTH:
            r = lax.broadcasted_iota(jnp.int32, (width, width), 0)
            c = lax.broadcasted_iota(jnp.int32, (width, width), 1)
            group = LANES // 2 if norm == "half" else width
            same = (r // group == c // group).astype(jnp.bfloat16)
            ss = jnp.dot((x * x).astype(jnp.bfloat16), same, preferred_element_type=jnp.float32)
        else:
            ss = jnp.sum(x * x, axis=-1, keepdims=True)
        x = x * lax.rsqrt(ss * (1.0 / n_norm) + EPS) * g_ref[...]

    if rope_blocks:
        c_ref, s1_ref, s2_ref = tabs
        blocks = []
        for b in range(width // LANES):
            xb = x[:, b * LANES:(b + 1) * LANES]
            if b in rope_blocks:
                sl = slice(b * LANES, (b + 1) * LANES)
                xb = (xb * c_ref[:, sl]
                      + pltpu.roll(xb, shifts[0], 1) * s1_ref[:, sl]
                      + pltpu.roll(xb, shifts[1], 1) * s2_ref[:, sl])
            blocks.append(xb)
        x = blocks[0] if len(blocks) == 1 else jnp.concatenate(blocks, axis=-1)

    if scale != 1.0:
        x = x * scale
    x = x.astype(jnp.bfloat16)
    if transpose:
        eye = (lax.broadcasted_iota(jnp.int32, (width, width), 0)
               == lax.broadcasted_iota(jnp.int32, (width, width), 1)).astype(jnp.bfloat16)
        x = lax.dot_general(eye, x, (((1,), (1,)), ((), ())),
                            preferred_element_type=jnp.float32).astype(jnp.bfloat16)
    if ones_rows:
        x = jnp.concatenate([x, jnp.ones((ones_rows, x.shape[1]), x.dtype)], axis=0)
    return x


def _prep_body(*refs, nx, pieces, has_gain, has_tabs, **head_kw):
    x_refs = refs[:nx]
    pos = nx
    g_ref = None
    if has_gain:
        g_ref = refs[pos]
        pos += 1
    tabs = None
    if has_tabs:
        tabs = refs[pos:pos + 3]
        pos += 3
    o_ref = refs[pos]
    for h, head_pieces in enumerate(pieces):
        cols = [x_refs[src][:, off:off + LANES] for src, off in head_pieces]
        x = cols[0] if len(cols) == 1 else jnp.concatenate(cols, axis=-1)
        y = _prep_head(x, g_ref, tabs, **head_kw)
        o_ref[h] = y.reshape(o_ref.shape[1:])


def _prep(srcs, pieces, *, out_kind, tm, gain=None, norm=None, n_norm=None, tabs=None,
          rope_blocks=(), shifts=(0, 0), scale=1.0, name="prep"):
    s = srcs[0][0].shape[0]
    n_heads = len(pieces)
    width = LANES * len(pieces[0])
    in_specs, args = [], []
    for arr, bw, idx in srcs:
        in_specs.append(pl.BlockSpec((tm, bw), functools.partial(lambda i, idx: (i, idx), idx=idx)))
        args.append(arr)
    if norm is not None:
        in_specs.append(pl.BlockSpec((1, width), lambda i: (0, 0)))
        args.append(gain.reshape(1, width))
    if rope_blocks:
        for t in tabs:
            in_specs.append(pl.BlockSpec((tm, width), lambda i: (i, 0)))
            args.append(t)
    if out_kind == "rows":
        out_shape = (n_heads, s, width)
        out_spec = pl.BlockSpec((n_heads, tm, width), lambda i: (0, i, 0))
    elif out_kind == "cols":
        out_shape = (n_heads, s // tm, width, tm)
        out_spec = pl.BlockSpec((n_heads, 1, width, tm), lambda i: (0, i, 0, 0))
    else:
        out_shape = (n_heads, s // tm, width + ONES_ROWS, tm)
        out_spec = pl.BlockSpec((n_heads, 1, width + ONES_ROWS, tm), lambda i: (0, i, 0, 0))
    return pl.pallas_call(
        functools.partial(_prep_body, nx=len(srcs), pieces=tuple(tuple(p) for p in pieces),
                          has_gain=norm is not None, has_tabs=bool(rope_blocks),
                          norm=norm, n_norm=n_norm, rope_blocks=tuple(rope_blocks), shifts=shifts,
                          scale=scale, transpose=out_kind != "rows",
                          ones_rows=ONES_ROWS if out_kind == "chunks" else 0),
        grid=(s // tm,),
        in_specs=in_specs,
        out_specs=out_spec,
        out_shape=jax.ShapeDtypeStruct(out_shape, jnp.bfloat16),
        compiler_params=_cparams(("parallel",)),
        name=name,
    )(*args)


def _attn_body(*refs, nchunk, tk, tq, diff, lambda_init):
    if diff:
        (qT_ref, k_ref, vT_ref, shift_ref, lam_ref, sub_ref, o_ref,
         acc_ref, p_ref, qq_ref, m_ref, s_ref, ps_ref) = refs
    else:
        qT_ref, k_ref, vT_ref, shift_ref, o_ref, acc_ref, p_ref, qq_ref, m_ref, s_ref, ps_ref = refs

    i = pl.program_id(1)
    nqb, _, nq = qq_ref.shape
    dv = o_ref.shape[1]
    tiles_per_block = tq // qT_ref.shape[3]
    chunk_bits = nchunk.bit_length() - 1
    assert nchunk == 1 << chunk_bits

    def scores(step):
        chunk = step & (nchunk - 1)
        block = jnp.minimum(i + (step >> chunk_bits), nqb - 1)
        start = pl.multiple_of(chunk * tk, tk)
        return jnp.dot(k_ref[0, pl.ds(start, tk), :], qq_ref[block],
                       preferred_element_type=jnp.float32)

    def probs(step):
        return jnp.exp2(scores(step) - shift_ref[...]).astype(jnp.bfloat16)

    @pl.when(i == 0)
    def _():
        for b in range(nqb):
            tiles = [qT_ref[0, b * tiles_per_block + c] for c in range(tiles_per_block)]
            q = tiles[0] if len(tiles) == 1 else jnp.concatenate(tiles, axis=1)
            if diff:
                first = lax.broadcasted_iota(jnp.int32, q.shape, 0) < B_QK_DIM
                zero = jnp.zeros_like(q)
                q = jnp.concatenate([jnp.where(first, q, zero), jnp.where(first, zero, q)], axis=1)
            qq_ref[b] = q
        p_ref[0] = probs(0)

    acc_ref[...] = jnp.zeros(acc_ref.shape, jnp.float32)

    def trip(t, carry):
        for u in range(ATTN_CHUNKS_PER_TRIP):
            j = t * ATTN_CHUNKS_PER_TRIP + u
            p_ref[(u + 1) % 2] = probs(j + 1)
            acc_ref[...] += jnp.dot(vT_ref[0, j], p_ref[u % 2], preferred_element_type=jnp.float32)
        return carry

    lax.fori_loop(0, nchunk // ATTN_CHUNKS_PER_TRIP, trip, 0)

    healthy = jnp.min(acc_ref[dv:dv + 1, :]) >= ATTN_MIN_DENOM

    @pl.when(jnp.logical_not(healthy))
    def _():
        rb = ATTN_ROW_BLOCK
        groups = rb // SUBLANES
        m_ref[...] = jnp.full(m_ref.shape, -jnp.inf, jnp.float32)
        acc_ref[...] = jnp.zeros(acc_ref.shape, jnp.float32)

        def block(r):
            return s_ref[r * rb:(r + 1) * rb, :].reshape(groups, SUBLANES, nq)

        def chunk(j, carry):
            s_ref[...] = scores(j)
            mx = jnp.max(block(0), axis=0)
            for r in range(1, tk // rb):
                mx = jnp.maximum(mx, jnp.max(block(r), axis=0))
            m_old = m_ref[...]
            m_new = jnp.maximum(m_old, jnp.max(mx, axis=0, keepdims=True))
            alpha = jnp.exp2(m_old - m_new)
            m_rows = jnp.broadcast_to(m_new, (SUBLANES, nq))[None]
            for r in range(tk // rb):
                p = jnp.exp2(block(r) - m_rows).reshape(rb, nq)
                ps_ref[r * rb:(r + 1) * rb, :] = p.astype(jnp.bfloat16)
            pv = jnp.dot(vT_ref[0, j], ps_ref[...], preferred_element_type=jnp.float32)
            acc_ref[...] = acc_ref[...] * alpha + pv
            m_ref[...] = m_new
            return carry

        lax.fori_loop(0, nchunk, chunk, 0)

    o = acc_ref[:dv, :] / acc_ref[dv:dv + 1, :]
    if diff:
        lp = lam_ref[...]
        lam = (jnp.exp(jnp.sum(lp[0:1] * lp[1:2], axis=-1, keepdims=True))
               - jnp.exp(jnp.sum(lp[2:3] * lp[3:4], axis=-1, keepdims=True))
               + lambda_init)
        o = o[:, :tq] - lam * o[:, tq:]
        ms = jnp.mean(o * o, axis=0, keepdims=True)
        o = o * lax.rsqrt(ms + EPS) * sub_ref[...] * (1.0 - lambda_init)
    o_ref[...] = o.T.astype(o_ref.dtype)


def _attention(qT, k, vT, score_bound, *, tq, diff=False, lam_p=None, subln=None, lambda_init=0.0, name):
    n_heads, n_qtiles, dq, q_tile = qT.shape
    s = n_qtiles * q_tile
    n_kv = k.shape[0]
    group = n_heads // n_kv
    _, nchunk, dv_ext, tk = vT.shape
    dv = dv_ext - ONES_ROWS
    nq = 2 * tq if diff else tq
    assert tq % q_tile == 0 and ATTN_CHUNKS_PER_TRIP % 2 == 0 and nchunk % ATTN_CHUNKS_PER_TRIP == 0
    in_specs = [pl.BlockSpec((1, n_qtiles, dq, q_tile), lambda h, i: (h, 0, 0, 0)),
                pl.BlockSpec((1, s, dq), lambda h, i: (h // group, 0, 0)),
                pl.BlockSpec((1, nchunk, dv_ext, tk), lambda h, i: (h // group, 0, 0, 0)),
                pl.BlockSpec((1, nq), lambda h, i: (0, 0))]
    args = [qT, k, vT, jnp.full((1, nq), score_bound, jnp.float32)]
    if diff:
        in_specs += [pl.BlockSpec(lam_p.shape, lambda h, i: (0, 0)),
                     pl.BlockSpec((dv, 1), lambda h, i: (0, 0))]
        args += [lam_p, subln.reshape(dv, 1)]
    return pl.pallas_call(
        functools.partial(_attn_body, nchunk=nchunk, tk=tk, tq=tq, diff=diff, lambda_init=lambda_init),
        grid=(n_heads, s // tq),
        in_specs=in_specs,
        out_specs=pl.BlockSpec((tq, dv), lambda h, i: (i, h)),
        out_shape=jax.ShapeDtypeStruct((s, n_heads * dv), jnp.bfloat16),
        scratch_shapes=[pltpu.VMEM((dv_ext, nq), jnp.float32),
                        pltpu.VMEM((2, tk, nq), jnp.bfloat16),
                        pltpu.VMEM((s // tq, dq, nq), jnp.bfloat16),
                        pltpu.VMEM((1, nq), jnp.float32),
                        pltpu.VMEM((tk, nq), jnp.float32),
                        pltpu.VMEM((tk, nq), jnp.bfloat16)],
        compiler_params=_cparams(("arbitrary", "arbitrary")),
        name=name,
    )(*args)


def _rope_cos_sin(pos, dim, theta):
    inv = theta ** (-jnp.arange(0, dim, 2, dtype=jnp.float32) / dim)
    ang = pos.astype(jnp.float32)[:, None] * inv[None, :]
    return jnp.cos(ang), jnp.sin(ang)


def _rotate_half_tables(cos, sin):
    zero = jnp.zeros_like(sin)
    return (jnp.concatenate([cos, cos], axis=-1),
            jnp.concatenate([-sin, zero], axis=-1),
            jnp.concatenate([zero, sin], axis=-1))


def _rope_tables(s):
    t = jnp.arange(s, dtype=jnp.int32)
    row = t // GRID_W
    col = t % GRID_W
    half = A_HEAD_DIM // 2
    a_row = _rotate_half_tables(*_rope_cos_sin(row, half, A_ROPE_THETA))
    a_col = _rotate_half_tables(*_rope_cos_sin(col, half, A_ROPE_THETA))
    a_tabs = tuple(jnp.concatenate([r, c], axis=-1) for r, c in zip(a_row, a_col))

    b_rot = _rotate_half_tables(*_rope_cos_sin(t, B_ROPE_DIM, B_ROPE_THETA))
    rest = B_QK_DIM - B_ROPE_DIM
    fill = (jnp.ones((s, rest), jnp.float32), jnp.zeros((s, rest), jnp.float32),
            jnp.zeros((s, rest), jnp.float32))
    b_tabs = tuple(jnp.tile(jnp.concatenate([r, f], axis=-1), (1, 2)) for r, f in zip(b_rot, fill))

    c_rot = _rotate_half_tables(*_rope_cos_sin(t, C_ROPE, C_ROPE_THETA))
    ones = jnp.ones((s, C_NOPE), jnp.float32)
    zeros = jnp.zeros((s, C_NOPE), jnp.float32)
    pad = jnp.zeros((s, C_QK_PAD - C_QK_DIM), jnp.float32)
    c_tabs = (jnp.concatenate([ones, c_rot[0], pad], axis=-1),
              jnp.concatenate([zeros, c_rot[1], pad], axis=-1),
              jnp.concatenate([zeros, c_rot[2], pad], axis=-1))
    return a_tabs, b_tabs, c_tabs


def _score_bound(dim, q_gain, k_gain, scale):
    return BOUND_SLACK * dim * scale * jnp.max(jnp.abs(q_gain)) * jnp.max(jnp.abs(k_gain))


def _pad_vec(g, n):
    return jnp.pad(g, (0, n - g.shape[0]))


def kernel(x, attn_norm, w_in, a_q_norm, a_k_norm, b_q_norm, b_k_norm, b_lambda, b_subln,
           c_q_a_norm, c_kv_a_norm, c_w_q_up, c_w_kv_up, c_q_norm, c_k_norm, w_out,
           ffn_norm, w_gate, w_up, w_down):
    bsz, s, d = x.shape
    assert (bsz, s, d) == (1, SEQ, D_MODEL)
    bf = jnp.bfloat16
    a_tabs, b_tabs, c_tabs = _rope_tables(s)
    tp = 512
    h = x.reshape(s, d)
    w_in_t = jnp.swapaxes(w_in, 1, 2)

    for l in range(DEPTH):
        lambda_init = 0.8 - 0.6 * math.exp(-0.3 * l)
        wq_up = c_w_q_up[l].astype(bf).reshape(C_Q_LORA, C_HEADS, C_QK_DIM)
        wq_up = jnp.pad(wq_up, ((0, 0), (0, 0), (0, C_QK_PAD - C_QK_DIM))).reshape(C_Q_LORA, C_HEADS * C_QK_PAD)
        wkv_up = c_w_kv_up[l].astype(bf)

        xn = _rmsnorm(h, attn_norm[l])
        proj = _matmul([xn], [w_in_t], out_dtype=jnp.float32, tm=1024, tn=PROJ_BLOCK, layer=l,
                       w_transposed=True, n_cols=N_IN_MAIN, name="in_proj")
        c_kr = _tail_proj(xn, w_in_t, layer=l, row0=N_IN_MAIN, tm=1024, name="in_proj_kr")

        def proj_blocks(first, count):
            return [(proj, PROJ_BLOCK, first + b) for b in range(count)]

        a_scale = LOG2E / math.sqrt(A_HEAD_DIM)
        a_q = _prep(proj_blocks(0, 3), _head_pieces(A_HEADS), out_kind="cols", tm=tp,
                    gain=a_q_norm[l], norm="full", n_norm=A_HEAD_DIM, tabs=a_tabs, rope_blocks=(0,),
                    shifts=(96, 32), scale=a_scale, name="prep_a_q")
        a_k = _prep(proj_blocks(3, 1), _head_pieces(A_KV_HEADS), out_kind="rows", tm=tp,
                    gain=a_k_norm[l], norm="full", n_norm=A_HEAD_DIM, tabs=a_tabs, rope_blocks=(0,),
                    shifts=(96, 32), name="prep_a_k")
        a_v = _prep(proj_blocks(4, 1), _head_pieces(A_KV_HEADS), out_kind="chunks", tm=ATTN_KEY_CHUNK,
                    name="prep_a_v")
        ya = _attention(a_q, a_k, a_v, _score_bound(A_HEAD_DIM, a_q_norm[l], a_k_norm[l], a_scale),
                        tq=1024, name="attn_a")

        b_scale = LOG2E / math.sqrt(B_QK_DIM)
        b_q = _prep(proj_blocks(5, 2), _head_pieces(B_HEADS), out_kind="cols", tm=tp,
                    gain=jnp.tile(b_q_norm[l], 2), norm="half", n_norm=B_QK_DIM, tabs=b_tabs,
                    rope_blocks=(0,), shifts=(120, 8), scale=b_scale, name="prep_b_q")
        b_k = _prep(proj_blocks(7, 2), _head_pieces(B_HEADS), out_kind="rows", tm=tp,
                    gain=jnp.tile(b_k_norm[l], 2), norm="half", n_norm=B_QK_DIM, tabs=b_tabs,
                    rope_blocks=(0,), shifts=(120, 8), name="prep_b_k")
        b_v = _prep(proj_blocks(9, 2), _head_pieces(B_HEADS), out_kind="chunks", tm=ATTN_KEY_CHUNK,
                    name="prep_b_v")
        yb = _attention(b_q, b_k, b_v, _score_bound(B_QK_DIM, b_q_norm[l], b_k_norm[l], b_scale),
                        tq=512, diff=True, lam_p=b_lambda[l], subln=b_subln[l],
                        lambda_init=lambda_init, name="attn_b")

        cq_lat = _prep(proj_blocks(11, 2), _head_pieces(1, blocks=C_Q_LORA // LANES), out_kind="rows", tm=tp,
                       gain=c_q_a_norm[l], norm="full", n_norm=C_Q_LORA, name="norm_c_q")[0]
        ckv_lat = _prep(proj_blocks(13, 1), _head_pieces(1, blocks=C_KV_LORA // LANES), out_kind="rows", tm=tp,
                        gain=c_kv_a_norm[l], norm="full", n_norm=C_KV_LORA, name="norm_c_kv")[0]
        c_q_raw = _matmul([cq_lat], [wq_up], out_dtype=jnp.float32, tm=1024, tn=1536, name="c_q_up")
        c_kv_raw = _matmul([ckv_lat], [wkv_up], out_dtype=jnp.float32, tm=1024, tn=1536, name="c_kv_up")
        c_scale = LOG2E / math.sqrt(C_QK_DIM)
        n_up = C_HEADS * C_QK_PAD // PROJ_BLOCK
        c_q = _prep([(c_q_raw, PROJ_BLOCK, b) for b in range(n_up)],
                    _head_pieces(C_HEADS, lanes_per_head=C_QK_PAD, blocks=2), out_kind="cols", tm=tp,
                    gain=_pad_vec(c_q_norm[l], C_QK_PAD), norm="full", n_norm=C_QK_DIM, tabs=c_tabs,
                    rope_blocks=(1,), shifts=(96, 32), scale=c_scale, name="prep_c_q")
        kv_srcs = [(c_kv_raw, PROJ_BLOCK, b) for b in range(n_up)]
        c_k_pieces = [[nope, (n_up, 0)] for (nope,) in _head_pieces(C_HEADS, lanes_per_head=2 * LANES)]
        c_k = _prep(kv_srcs + [(c_kr, LANES, 0)], c_k_pieces, out_kind="rows", tm=tp,
                    gain=_pad_vec(c_k_norm[l], C_QK_PAD), norm="full", n_norm=C_QK_DIM, tabs=c_tabs,
                    rope_blocks=(1,), shifts=(96, 32), name="prep_c_k")
        c_v = _prep(kv_srcs, _head_pieces(C_HEADS, lanes_per_head=2 * LANES, lane0=LANES), out_kind="chunks",
                    tm=ATTN_KEY_CHUNK, name="prep_c_v")
        yc = _attention(c_q, c_k, c_v, _score_bound(C_QK_DIM, c_q_norm[l], c_k_norm[l], c_scale),
                        tq=1024, name="attn_c")

        h = _matmul([ya, yb, yc], [w_out], res=h, out_dtype=jnp.float32, tm=1024, tn=512, layer=l,
                    name="out_proj")

        hn = _rmsnorm(h, ffn_norm[l])
        act = _matmul([hn], [w_gate, w_up], out_dtype=bf, tm=1024, tn=256, layer=l, name="ffn_gate_up")
        h = _matmul_kouter(act, w_down, h, layer=l, tm=1024, tn=256, tk=FFN_HIDDEN // 2, n_split=1,
                           name="ffn_down")

    return h.reshape(bsz, s, d)
```

```python
import functools
import math

import jax
import jax.numpy as jnp
from jax import lax
from jax.experimental import pallas as pl
from jax.experimental.pallas import tpu as pltpu

D_MODEL = 4096
SEQ = 8192
DEPTH = 2
GRID_W = 64
EPS = 1e-6

A_HEAD_DIM = 128
A_HEADS = 12
A_KV_HEADS = 4
A_ROPE_THETA = 10000.0

B_QK_DIM = 64
B_V_DIM = 128
B_HEADS = 8
B_ROPE_DIM = 16
B_ROPE_THETA = 500000.0

C_V_DIM = 128
C_HEADS = 12
C_Q_LORA = 1024
C_KV_LORA = 512
C_NOPE = 128
C_ROPE = 64
C_QK_DIM = C_NOPE + C_ROPE
C_QK_PAD = 256
C_ROPE_THETA = 10000.0

FFN_HIDDEN = 11008

LANES = 128
SUBLANES = 8
ATTN_ROW_BLOCK = 64
ATTN_KEY_CHUNK = 512
ATTN_CHUNKS_PER_TRIP = 8
ATTN_MIN_DENOM = 2.0 ** -60
BOUND_SLACK = 1.01
MXU_SUM_MAX_WIDTH = 256
ONES_ROWS = 16
VMEM_LIMIT_BYTES = 56 * 1024 * 1024

LOG2E = math.log2(math.e)

PROJ_BLOCK = 512
N_IN_MAIN = 14 * PROJ_BLOCK


def _head_pieces(n_heads, lanes_per_head=LANES, lane0=0, blocks=1):
    out = []
    for h in range(n_heads):
        head = []
        for b in range(blocks):
            col = lane0 + h * lanes_per_head + b * LANES
            head.append((col // PROJ_BLOCK, col % PROJ_BLOCK))
        out.append(head)
    return out


def _cparams(sem):
    return pltpu.CompilerParams(dimension_semantics=sem, vmem_limit_bytes=VMEM_LIMIT_BYTES)


def _rmsnorm_body(x_ref, g_ref, o_ref):
    x = x_ref[...]
    ms = jnp.mean(x * x, axis=-1, keepdims=True)
    o_ref[...] = (x * lax.rsqrt(ms + EPS) * g_ref[...]).astype(o_ref.dtype)


def _rmsnorm(x, g, tm=256):
    s, d = x.shape
    return pl.pallas_call(
        _rmsnorm_body,
        grid=(s // tm,),
        in_specs=[pl.BlockSpec((tm, d), lambda i: (i, 0)),
                  pl.BlockSpec((1, d), lambda i: (0, 0))],
        out_specs=pl.BlockSpec((tm, d), lambda i: (i, 0)),
        out_shape=jax.ShapeDtypeStruct((s, d), jnp.bfloat16),
        compiler_params=_cparams(("parallel",)),
        name="rmsnorm",
    )(x, g.reshape(1, d))


def _matmul_body(*refs, n_x, n_w, has_res, w_transposed):
    x_refs = refs[:n_x]
    w_refs = refs[n_x:n_x + n_w]
    pos = n_x + n_w
    res_ref = refs[pos] if has_res else None
    o_ref = refs[pos + int(has_res)]

    x = x_refs[0][...] if n_x == 1 else jnp.concatenate([r[...] for r in x_refs], axis=1)
    tiles = [w[...] if w.dtype == x.dtype else w[...].astype(x.dtype) for w in w_refs]
    contract = (((1,), (1,)), ((), ())) if w_transposed else (((1,), (0,)), ((), ()))
    prods = [lax.dot_general(x, w, contract, preferred_element_type=jnp.float32) for w in tiles]
    if n_w == 2:
        g, u = prods
        y = g * jax.nn.sigmoid(g) * u
    else:
        y = prods[0]
    if has_res:
        y = y + res_ref[...]
    o_ref[...] = y.astype(o_ref.dtype)


def _matmul(xs, ws, res=None, *, out_dtype, tm, tn, name, layer=None, w_transposed=False, n_cols=None):
    m = xs[0].shape[0]
    kdim = sum(x.shape[1] for x in xs)
    n = n_cols if n_cols is not None else ws[0].shape[-2 if w_transposed else -1]
    assert m % tm == 0 and n % tn == 0
    in_specs = [pl.BlockSpec((tm, x.shape[1]), lambda i, j: (i, 0)) for x in xs]
    w_block = (tn, kdim) if w_transposed else (kdim, tn)
    lead = () if layer is None else (None,)

    def w_index(i, j):
        pos = (j, 0) if w_transposed else (0, j)
        return pos if layer is None else (layer,) + pos

    in_specs += [pl.BlockSpec(lead + w_block, w_index) for _ in ws]
    args = [*xs, *ws]
    if res is not None:
        in_specs.append(pl.BlockSpec((tm, tn), lambda i, j: (i, j)))
        args.append(res)
    return pl.pallas_call(
        functools.partial(_matmul_body, n_x=len(xs), n_w=len(ws), has_res=res is not None,
                          w_transposed=w_transposed),
        grid=(m // tm, n // tn),
        in_specs=in_specs,
        out_specs=pl.BlockSpec((tm, tn), lambda i, j: (i, j)),
        out_shape=jax.ShapeDtypeStruct((m, n), out_dtype),
        compiler_params=_cparams(("parallel", "parallel")),
        name=name,
    )(*args)


def _tail_proj_body(x_ref, w_ref, o_ref):
    w = w_ref[...].astype(x_ref.dtype)
    w = jnp.concatenate([w, jnp.zeros_like(w)], axis=0)
    o_ref[...] = lax.dot_general(x_ref[...], w, (((1,), (1,)), ((), ())),
                                 preferred_element_type=jnp.float32)


def _tail_proj(x, w_t, *, layer, row0, tm, name):
    m, kdim = x.shape
    rows = w_t.shape[1] - row0
    assert rows * 2 == LANES and row0 % rows == 0
    return pl.pallas_call(
        _tail_proj_body,
        grid=(m // tm,),
        in_specs=[pl.BlockSpec((tm, kdim), lambda i: (i, 0)),
                  pl.BlockSpec((None, rows, kdim), lambda i: (layer, row0 // rows, 0))],
        out_specs=pl.BlockSpec((tm, LANES), lambda i: (i, 0)),
        out_shape=jax.ShapeDtypeStruct((m, LANES), jnp.float32),
        compiler_params=_cparams(("parallel",)),
        name=name,
    )(x, w_t)


def _matmul_kouter_body(x_ref, w_ref, res_ref, o_ref, acc_ref, *, nk):
    k = pl.program_id(2)
    j = pl.program_id(3)
    prod = jnp.dot(x_ref[...], w_ref[...].astype(x_ref.dtype), preferred_element_type=jnp.float32)

    @pl.when(k == 0)
    def _():
        acc_ref[j] = prod

    if nk > 2:
        @pl.when(jnp.logical_and(k > 0, k < nk - 1))
        def _():
            acc_ref[j] += prod

    @pl.when(k == nk - 1)
    def _():
        o_ref[...] = (acc_ref[j] + prod + res_ref[...]).astype(o_ref.dtype)


def _matmul_kouter(x, w, res, *, layer, tm, tn, tk, n_split, name):
    m, kdim = x.shape
    n = w.shape[-1]
    nk = kdim // tk
    nj = n // (tn * n_split)
    assert m % tm == 0 and n % (tn * n_split) == 0 and kdim % tk == 0 and nk >= 2

    def out_index(hf, i, k, j):
        return i, hf * nj + jnp.where(k == nk - 1, j, 0)

    return pl.pallas_call(
        functools.partial(_matmul_kouter_body, nk=nk),
        grid=(n_split, m // tm, nk, nj),
        in_specs=[pl.BlockSpec((tm, tk), lambda hf, i, k, j: (i, k)),
                  pl.BlockSpec((None, tk, tn), lambda hf, i, k, j: (layer, k, hf * nj + j)),
                  pl.BlockSpec((tm, tn), out_index)],
        out_specs=pl.BlockSpec((tm, tn), out_index),
        out_shape=jax.ShapeDtypeStruct((m, n), res.dtype),
        scratch_shapes=[pltpu.VMEM((nj, tm, tn), jnp.float32)],
        compiler_params=_cparams(("parallel", "parallel", "arbitrary", "arbitrary")),
        name=name,
    )(x, w, res)


def _prep_head(x, g_ref, tabs, *, norm, n_norm, rope_blocks, shifts, scale, transpose, ones_rows):
    width = x.shape[-1]
    if norm is not None:
        if width <= MXU_SUM_MAX_WIDTH:
            r = lax.broadcasted_iota(jnp.int32, (width, width), 0)
            c = lax.broadcasted_iota(jnp.int32, (width, width), 1)
            group = LANES // 2 if norm == "half" else width
            same = (r // group == c // group).astype(jnp.bfloat16)
            ss = jnp.dot((x * x).astype(jnp.bfloat16), same, preferred_element_type=jnp.float32)
        else:
            ss = jnp.sum(x * x, axis=-1, keepdims=True)
        x = x * lax.rsqrt(ss * (1.0 / n_norm) + EPS) * g_ref[...]

    if rope_blocks:
        c_ref, s1_ref, s2_ref = tabs
        blocks = []
        for b in range(width // LANES):
            xb = x[:, b * LANES:(b + 1) * LANES]
            if b in rope_blocks:
                sl = slice(b * LANES, (b + 1) * LANES)
                xb = (xb * c_ref[:, sl]
                      + pltpu.roll(xb, shifts[0], 1) * s1_ref[:, sl]
                      + pltpu.roll(xb, shifts[1], 1) * s2_ref[:, sl])
            blocks.append(xb)
        x = blocks[0] if len(blocks) == 1 else jnp.concatenate(blocks, axis=-1)

    if scale != 1.0:
        x = x * scale
    x = x.astype(jnp.bfloat16)
    if transpose:
        eye = (lax.broadcasted_iota(jnp.int32, (width, width), 0)
               == lax.broadcasted_iota(jnp.int32, (width, width), 1)).astype(jnp.bfloat16)
        x = lax.dot_general(eye, x, (((1,), (1,)), ((), ())),
                            preferred_element_type=jnp.float32).astype(jnp.bfloat16)
    if ones_rows:
        x = jnp.concatenate([x, jnp.ones((ones_rows, x.shape[1]), x.dtype)], axis=0)
    return x


def _prep_body(*refs, nx, pieces, has_gain, has_tabs, **head_kw):
    x_refs = refs[:nx]
    pos = nx
    g_ref = None
    if has_gain:
        g_ref = refs[pos]
        pos += 1
    tabs = None
    if has_tabs:
        tabs = refs[pos:pos + 3]
        pos += 3
    o_ref = refs[pos]
    for h, head_pieces in enumerate(pieces):
        cols = [x_refs[src][:, off:off + LANES] for src, off in head_pieces]
        x = cols[0] if len(cols) == 1 else jnp.concatenate(cols, axis=-1)
        y = _prep_head(x, g_ref, tabs, **head_kw)
        o_ref[h] = y.reshape(o_ref.shape[1:])


def _prep(srcs, pieces, *, out_kind, tm, gain=None, norm=None, n_norm=None, tabs=None,
          rope_blocks=(), shifts=(0, 0), scale=1.0, name="prep"):
    s = srcs[0][0].shape[0]
    n_heads = len(pieces)
    width = LANES * len(pieces[0])
    in_specs, args = [], []
    for arr, bw, idx in srcs:
        in_specs.append(pl.BlockSpec((tm, bw), functools.partial(lambda i, idx: (i, idx), idx=idx)))
        args.append(arr)
    if norm is not None:
        in_specs.append(pl.BlockSpec((1, width), lambda i: (0, 0)))
        args.append(gain.reshape(1, width))
    if rope_blocks:
        for t in tabs:
            in_specs.append(pl.BlockSpec((tm, width), lambda i: (i, 0)))
            args.append(t)
    if out_kind == "rows":
        out_shape = (n_heads, s, width)
        out_spec = pl.BlockSpec((n_heads, tm, width), lambda i: (0, i, 0))
    elif out_kind == "cols":
        out_shape = (n_heads, s // tm, width, tm)
        out_spec = pl.BlockSpec((n_heads, 1, width, tm), lambda i: (0, i, 0, 0))
    else:
        out_shape = (n_heads, s // tm, width + ONES_ROWS, tm)
        out_spec = pl.BlockSpec((n_heads, 1, width + ONES_ROWS, tm), lambda i: (0, i, 0, 0))
    return pl.pallas_call(
        functools.partial(_prep_body, nx=len(srcs), pieces=tuple(tuple(p) for p in pieces),
                          has_gain=norm is not None, has_tabs=bool(rope_blocks),
                          norm=norm, n_norm=n_norm, rope_blocks=tuple(rope_blocks), shifts=shifts,
                          scale=scale, transpose=out_kind != "rows",
                          ones_rows=ONES_ROWS if out_kind == "chunks" else 0),
        grid=(s // tm,),
        in_specs=in_specs,
        out_specs=out_spec,
        out_shape=jax.ShapeDtypeStruct(out_shape, jnp.bfloat16),
        compiler_params=_cparams(("parallel",)),
        name=name,
    )(*args)


def _attn_body(*refs, nchunk, tk, tq, diff, lambda_init):
    if diff:
        (qT_ref, k_ref, vT_ref, shift_ref, lam_ref, sub_ref, o_ref,
         acc_ref, p_ref, qq_ref, m_ref, s_ref, ps_ref) = refs
    else:
        qT_ref, k_ref, vT_ref, shift_ref, o_ref, acc_ref, p_ref, qq_ref, m_ref, s_ref, ps_ref = refs

    i = pl.program_id(1)
    nqb, _, nq = qq_ref.shape
    dv = o_ref.shape[1]
    tiles_per_block = tq // qT_ref.shape[3]
    chunk_bits = nchunk.bit_length() - 1
    assert nchunk == 1 << chunk_bits

    def scores(step):
        chunk = step & (nchunk - 1)
        block = jnp.minimum(i + (step >> chunk_bits), nqb - 1)
        start = pl.multiple_of(chunk * tk, tk)
        return jnp.dot(k_ref[0, pl.ds(start, tk), :], qq_ref[block],
                       preferred_element_type=jnp.float32)

    def probs(step):
        return jnp.exp2(scores(step) - shift_ref[...]).astype(jnp.bfloat16)

    @pl.when(i == 0)
    def _():
        for b in range(nqb):
            tiles = [qT_ref[0, b * tiles_per_block + c] for c in range(tiles_per_block)]
            q = tiles[0] if len(tiles) == 1 else jnp.concatenate(tiles, axis=1)
            if diff:
                first = lax.broadcasted_iota(jnp.int32, q.shape, 0) < B_QK_DIM
                zero = jnp.zeros_like(q)
                q = jnp.concatenate([jnp.where(first, q, zero), jnp.where(first, zero, q)], axis=1)
            qq_ref[b] = q
        p_ref[0] = probs(0)

    acc_ref[...] = jnp.zeros(acc_ref.shape, jnp.float32)

    def trip(t, carry):
        for u in range(ATTN_CHUNKS_PER_TRIP):
            j = t * ATTN_CHUNKS_PER_TRIP + u
            p_ref[(u + 1) % 2] = probs(j + 1)
            acc_ref[...] += jnp.dot(vT_ref[0, j], p_ref[u % 2], preferred_element_type=jnp.float32)
        return carry

    lax.fori_loop(0, nchunk // ATTN_CHUNKS_PER_TRIP, trip, 0)

    healthy = jnp.min(acc_ref[dv:dv + 1, :]) >= ATTN_MIN_DENOM

    @pl.when(jnp.logical_not(healthy))
    def _():
        rb = ATTN_ROW_BLOCK
        groups = rb // SUBLANES
        m_ref[...] = jnp.full(m_ref.shape, -jnp.inf, jnp.float32)
        acc_ref[...] = jnp.zeros(acc_ref.shape, jnp.float32)

        def block(r):
            return s_ref[r * rb:(r + 1) * rb, :].reshape(groups, SUBLANES, nq)

        def chunk(j, carry):
            s_ref[...] = scores(j)
            mx = jnp.max(block(0), axis=0)
            for r in range(1, tk // rb):
                mx = jnp.maximum(mx, jnp.max(block(r), axis=0))
            m_old = m_ref[...]
            m_new = jnp.maximum(m_old, jnp.max(mx, axis=0, keepdims=True))
            alpha = jnp.exp2(m_old - m_new)
            m_rows = jnp.broadcast_to(m_new, (SUBLANES, nq))[None]
            for r in range(tk // rb):
                p = jnp.exp2(block(r) - m_rows).reshape(rb, nq)
                ps_ref[r * rb:(r + 1) * rb, :] = p.astype(jnp.bfloat16)
            pv = jnp.dot(vT_ref[0, j], ps_ref[...], preferred_element_type=jnp.float32)
            acc_ref[...] = acc_ref[...] * alpha + pv
            m_ref[...] = m_new
            return carry

        lax.fori_loop(0, nchunk, chunk, 0)

    o = acc_ref[:dv, :] / acc_ref[dv:dv + 1, :]
    if diff:
        lp = lam_ref[...]
        lam = (jnp.exp(jnp.sum(lp[0:1] * lp[1:2], axis=-1, keepdims=True))
               - jnp.exp(jnp.sum(lp[2:3] * lp[3:4], axis=-1, keepdims=True))
               + lambda_init)
        o = o[:, :tq] - lam * o[:, tq:]
        ms = jnp.mean(o * o, axis=0, keepdims=True)
        o = o * lax.rsqrt(ms + EPS) * sub_ref[...] * (1.0 - lambda_init)
    o_ref[...] = o.T.astype(o_ref.dtype)


def _attention(qT, k, vT, score_bound, *, tq, diff=False, lam_p=None, subln=None, lambda_init=0.0, name):
    n_heads, n_qtiles, dq, q_tile = qT.shape
    s = n_qtiles * q_tile
    n_kv = k.shape[0]
    group = n_heads // n_kv
    _, nchunk, dv_ext, tk = vT.shape
    dv = dv_ext - ONES_ROWS
    nq = 2 * tq if diff else tq
    assert tq % q_tile == 0 and ATTN_CHUNKS_PER_TRIP % 2 == 0 and nchunk % ATTN_CHUNKS_PER_TRIP == 0
    in_specs = [pl.BlockSpec((1, n_qtiles, dq, q_tile), lambda h, i: (h, 0, 0, 0)),
                pl.BlockSpec((1, s, dq), lambda h, i: (h // group, 0, 0)),
                pl.BlockSpec((1, nchunk, dv_ext, tk), lambda h, i: (h // group, 0, 0, 0)),
                pl.BlockSpec((1, nq), lambda h, i: (0, 0))]
    args = [qT, k, vT, jnp.full((1, nq), score_bound, jnp.float32)]
    if diff:
        in_specs += [pl.BlockSpec(lam_p.shape, lambda h, i: (0, 0)),
                     pl.BlockSpec((dv, 1), lambda h, i: (0, 0))]
        args += [lam_p, subln.reshape(dv, 1)]
    return pl.pallas_call(
        functools.partial(_attn_body, nchunk=nchunk, tk=tk, tq=tq, diff=diff, lambda_init=lambda_init),
        grid=(n_heads, s // tq),
        in_specs=in_specs,
        out_specs=pl.BlockSpec((tq, dv), lambda h, i: (i, h)),
        out_shape=jax.ShapeDtypeStruct((s, n_heads * dv), jnp.bfloat16),
        scratch_shapes=[pltpu.VMEM((dv_ext, nq), jnp.float32),
                        pltpu.VMEM((2, tk, nq), jnp.bfloat16),
                        pltpu.VMEM((s // tq, dq, nq), jnp.bfloat16),
                        pltpu.VMEM((1, nq), jnp.float32),
                        pltpu.VMEM((tk, nq), jnp.float32),
                        pltpu.VMEM((tk, nq), jnp.bfloat16)],
        compiler_params=_cparams(("arbitrary", "arbitrary")),
        name=name,
    )(*args)


def _rope_cos_sin(pos, dim, theta):
    inv = theta ** (-jnp.arange(0, dim, 2, dtype=jnp.float32) / dim)
    ang = pos.astype(jnp.float32)[:, None] * inv[None, :]
    return jnp.cos(ang), jnp.sin(ang)


def _rotate_half_tables(cos, sin):
    zero = jnp.zeros_like(sin)
    return (jnp.concatenate([cos, cos], axis=-1),
            jnp.concatenate([-sin, zero], axis=-1),
            jnp.concatenate([zero, sin], axis=-1))


def _rope_tables(s):
    t = jnp.arange(s, dtype=jnp.int32)
    row = t // GRID_W
    col = t % GRID_W
    half = A_HEAD_DIM // 2
    a_row = _rotate_half_tables(*_rope_cos_sin(row, half, A_ROPE_THETA))
    a_col = _rotate_half_tables(*_rope_cos_sin(col, half, A_ROPE_THETA))
    a_tabs = tuple(jnp.concatenate([r, c], axis=-1) for r, c in zip(a_row, a_col))

    b_rot = _rotate_half_tables(*_rope_cos_sin(t, B_ROPE_DIM, B_ROPE_THETA))
    rest = B_QK_DIM - B_ROPE_DIM
    fill = (jnp.ones((s, rest), jnp.float32), jnp.zeros((s, rest), jnp.float32),
            jnp.zeros((s, rest), jnp.float32))
    b_tabs = tuple(jnp.tile(jnp.concatenate([r, f], axis=-1), (1, 2)) for r, f in zip(b_rot, fill))

    c_rot = _rotate_half_tables(*_rope_cos_sin(t, C_ROPE, C_ROPE_THETA))
    ones = jnp.ones((s, C_NOPE), jnp.float32)
    zeros = jnp.zeros((s, C_NOPE), jnp.float32)
    pad = jnp.zeros((s, C_QK_PAD - C_QK_DIM), jnp.float32)
    c_tabs = (jnp.concatenate([ones, c_rot[0], pad], axis=-1),
              jnp.concatenate([zeros, c_rot[1], pad], axis=-1),
              jnp.concatenate([zeros, c_rot[2], pad], axis=-1))
    return a_tabs, b_tabs, c_tabs


def _score_bound(dim, q_gain, k_gain, scale):
    return BOUND_SLACK * dim * scale * jnp.max(jnp.abs(q_gain)) * jnp.max(jnp.abs(k_gain))


def _pad_vec(g, n):
    return jnp.pad(g, (0, n - g.shape[0]))


def kernel(x, attn_norm, w_in, a_q_norm, a_k_norm, b_q_norm, b_k_norm, b_lambda, b_subln,
           c_q_a_norm, c_kv_a_norm, c_w_q_up, c_w_kv_up, c_q_norm, c_k_norm, w_out,
           ffn_norm, w_gate, w_up, w_down):
    bsz, s, d = x.shape
    assert (bsz, s, d) == (1, SEQ, D_MODEL)
    bf = jnp.bfloat16
    a_tabs, b_tabs, c_tabs = _rope_tables(s)
    tp = 512
    h = x.reshape(s, d)
    w_in_t = jnp.swapaxes(w_in, 1, 2)

    for l in range(DEPTH):
        lambda_init = 0.8 - 0.6 * math.exp(-0.3 * l)
        wq_up = c_w_q_up[l].astype(bf).reshape(C_Q_LORA, C_HEADS, C_QK_DIM)
        wq_up = jnp.pad(wq_up, ((0, 0), (0, 0), (0, C_QK_PAD - C_QK_DIM))).reshape(C_Q_LORA, C_HEADS * C_QK_PAD)
        wkv_up = c_w_kv_up[l].astype(bf)

        xn = _rmsnorm(h, attn_norm[l])
        proj = _matmul([xn], [w_in_t], out_dtype=jnp.float32, tm=2048, tn=PROJ_BLOCK // 2, layer=l,
                       w_transposed=True, n_cols=N_IN_MAIN, name="in_proj")
        c_kr = _tail_proj(xn, w_in_t, layer=l, row0=N_IN_MAIN, tm=1024, name="in_proj_kr")

        def proj_blocks(first, count):
            return [(proj, PROJ_BLOCK, first + b) for b in range(count)]

        a_scale = LOG2E / math.sqrt(A_HEAD_DIM)
        a_q = _prep(proj_blocks(0, 3), _head_pieces(A_HEADS), out_kind="cols", tm=tp,
                    gain=a_q_norm[l], norm="full", n_norm=A_HEAD_DIM, tabs=a_tabs, rope_blocks=(0,),
                    shifts=(96, 32), scale=a_scale, name="prep_a_q")
        a_k = _prep(proj_blocks(3, 1), _head_pieces(A_KV_HEADS), out_kind="rows", tm=tp,
                    gain=a_k_norm[l], norm="full", n_norm=A_HEAD_DIM, tabs=a_tabs, rope_blocks=(0,),
                    shifts=(96, 32), name="prep_a_k")
        a_v = _prep(proj_blocks(4, 1), _head_pieces(A_KV_HEADS), out_kind="chunks", tm=ATTN_KEY_CHUNK,
                    name="prep_a_v")
        ya = _attention(a_q, a_k, a_v, _score_bound(A_HEAD_DIM, a_q_norm[l], a_k_norm[l], a_scale),
                        tq=1024, name="attn_a")

        b_scale = LOG2E / math.sqrt(B_QK_DIM)
        b_q = _prep(proj_blocks(5, 2), _head_pieces(B_HEADS), out_kind="cols", tm=tp,
                    gain=jnp.tile(b_q_norm[l], 2), norm="half", n_norm=B_QK_DIM, tabs=b_tabs,
                    rope_blocks=(0,), shifts=(120, 8), scale=b_scale, name="prep_b_q")
        b_k = _prep(proj_blocks(7, 2), _head_pieces(B_HEADS), out_kind="rows", tm=tp,
                    gain=jnp.tile(b_k_norm[l], 2), norm="half", n_norm=B_QK_DIM, tabs=b_tabs,
                    rope_blocks=(0,), shifts=(120, 8), name="prep_b_k")
        b_v = _prep(proj_blocks(9, 2), _head_pieces(B_HEADS), out_kind="chunks", tm=ATTN_KEY_CHUNK,
                    name="prep_b_v")
        yb = _attention(b_q, b_k, b_v, _score_bound(B_QK_DIM, b_q_norm[l], b_k_norm[l], b_scale),
                        tq=512, diff=True, lam_p=b_lambda[l], subln=b_subln[l],
                        lambda_init=lambda_init, name="attn_b")

        cq_lat = _prep(proj_blocks(11, 2), _head_pieces(1, blocks=C_Q_LORA // LANES), out_kind="rows", tm=tp,
                       gain=c_q_a_norm[l], norm="full", n_norm=C_Q_LORA, name="norm_c_q")[0]
        ckv_lat = _prep(proj_blocks(13, 1), _head_pieces(1, blocks=C_KV_LORA // LANES), out_kind="rows", tm=tp,
                        gain=c_kv_a_norm[l], norm="full", n_norm=C_KV_LORA, name="norm_c_kv")[0]
        c_q_raw = _matmul([cq_lat], [wq_up], out_dtype=jnp.float32, tm=1024, tn=1536, name="c_q_up")
        c_kv_raw = _matmul([ckv_lat], [wkv_up], out_dtype=jnp.float32, tm=1024, tn=1536, name="c_kv_up")
        c_scale = LOG2E / math.sqrt(C_QK_DIM)
        n_up = C_HEADS * C_QK_PAD // PROJ_BLOCK
        c_q = _prep([(c_q_raw, PROJ_BLOCK, b) for b in range(n_up)],
                    _head_pieces(C_HEADS, lanes_per_head=C_QK_PAD, blocks=2), out_kind="cols", tm=tp,
                    gain=_pad_vec(c_q_norm[l], C_QK_PAD), norm="full", n_norm=C_QK_DIM, tabs=c_tabs,
                    rope_blocks=(1,), shifts=(96, 32), scale=c_scale, name="prep_c_q")
        kv_srcs = [(c_kv_raw, PROJ_BLOCK, b) for b in range(n_up)]
        c_k_pieces = [[nope, (n_up, 0)] for (nope,) in _head_pieces(C_HEADS, lanes_per_head=2 * LANES)]
        c_k = _prep(kv_srcs + [(c_kr, LANES, 0)], c_k_pieces, out_kind="rows", tm=tp,
                    gain=_pad_vec(c_k_norm[l], C_QK_PAD), norm="full", n_norm=C_QK_DIM, tabs=c_tabs,
                    rope_blocks=(1,), shifts=(96, 32), name="prep_c_k")
        c_v = _prep(kv_srcs, _head_pieces(C_HEADS, lanes_per_head=2 * LANES, lane0=LANES), out_kind="chunks",
                    tm=ATTN_KEY_CHUNK, name="prep_c_v")
        yc = _attention(c_q, c_k, c_v, _score_bound(C_QK_DIM, c_q_norm[l], c_k_norm[l], c_scale),
                        tq=1024, name="attn_c")

        h = _matmul([ya, yb, yc], [w_out], res=h, out_dtype=jnp.float32, tm=1024, tn=512, layer=l,
                    name="out_proj")

        hn = _rmsnorm(h, ffn_norm[l])
        act = _matmul([hn], [w_gate, w_up], out_dtype=bf, tm=2048, tn=256, layer=l, name="ffn_gate_up")
        h = _matmul_kouter(act, w_down, h, layer=l, tm=1024, tn=256, tk=FFN_HIDDEN // 2, n_split=1,
                           name="ffn_down")

    return h.reshape(bsz, s, d)
```

```python
import functools
import math

import jax
import jax.numpy as jnp
from jax import lax
from jax.experimental import pallas as pl
from jax.experimental.pallas import tpu as pltpu

D_MODEL = 4096
SEQ = 8192
DEPTH = 2
GRID_W = 64
EPS = 1e-6

A_HEAD_DIM = 128
A_HEADS = 12
A_KV_HEADS = 4
A_ROPE_THETA = 10000.0

B_QK_DIM = 64
B_V_DIM = 128
B_HEADS = 8
B_ROPE_DIM = 16
B_ROPE_THETA = 500000.0

C_V_DIM = 128
C_HEADS = 12
C_Q_LORA = 1024
C_KV_LORA = 512
C_NOPE = 128
C_ROPE = 64
C_QK_DIM = C_NOPE + C_ROPE
C_QK_PAD = 256
C_ROPE_THETA = 10000.0

FFN_HIDDEN = 11008

LANES = 128
SUBLANES = 8
ATTN_ROW_BLOCK = 64
ATTN_KEY_CHUNK = 512
ATTN_CHUNKS_PER_TRIP = 8
ATTN_MIN_DENOM = 2.0 ** -60
BOUND_SLACK = 1.01
MXU_SUM_MAX_WIDTH = 256
ONES_ROWS = 16
VMEM_LIMIT_BYTES = 56 * 1024 * 1024

LOG2E = math.log2(math.e)

PROJ_BLOCK = 512
N_IN_MAIN = 14 * PROJ_BLOCK


def _head_pieces(n_heads, lanes_per_head=LANES, lane0=0, blocks=1):
    out = []
    for h in range(n_heads):
        head = []
        for b in range(blocks):
            col = lane0 + h * lanes_per_head + b * LANES
            head.append((col // PROJ_BLOCK, col % PROJ_BLOCK))
        out.append(head)
    return out


def _cparams(sem):
    return pltpu.CompilerParams(dimension_semantics=sem, vmem_limit_bytes=VMEM_LIMIT_BYTES)


def _rmsnorm_body(x_ref, g_ref, o_ref):
    x = x_ref[...]
    ms = jnp.mean(x * x, axis=-1, keepdims=True)
    o_ref[...] = (x * lax.rsqrt(ms + EPS) * g_ref[...]).astype(o_ref.dtype)


def _rmsnorm(x, g, tm=256):
    s, d = x.shape
    return pl.pallas_call(
        _rmsnorm_body,
        grid=(s // tm,),
        in_specs=[pl.BlockSpec((tm, d), lambda i: (i, 0)),
                  pl.BlockSpec((1, d), lambda i: (0, 0))],
        out_specs=pl.BlockSpec((tm, d), lambda i: (i, 0)),
        out_shape=jax.ShapeDtypeStruct((s, d), jnp.bfloat16),
        compiler_params=_cparams(("parallel",)),
        name="rmsnorm",
    )(x, g.reshape(1, d))


def _matmul_body(*refs, n_x, n_w, has_res, w_transposed):
    x_refs = refs[:n_x]
    w_refs = refs[n_x:n_x + n_w]
    pos = n_x + n_w
    res_ref = refs[pos] if has_res else None
    o_ref = refs[pos + int(has_res)]

    x = x_refs[0][...] if n_x == 1 else jnp.concatenate([r[...] for r in x_refs], axis=1)
    tiles = [w[...] if w.dtype == x.dtype else w[...].astype(x.dtype) for w in w_refs]
    contract = (((1,), (1,)), ((), ())) if w_transposed else (((1,), (0,)), ((), ()))
    prods = [lax.dot_general(x, w, contract, preferred_element_type=jnp.float32) for w in tiles]
    if n_w == 2:
        g, u = prods
        y = g * jax.nn.sigmoid(g) * u
    else:
        y = prods[0]
    if has_res:
        y = y + res_ref[...]
    o_ref[...] = y.astype(o_ref.dtype)


def _matmul(xs, ws, res=None, *, out_dtype, tm, tn, name, layer=None, w_transposed=False, n_cols=None):
    m = xs[0].shape[0]
    kdim = sum(x.shape[1] for x in xs)
    n = n_cols if n_cols is not None else ws[0].shape[-2 if w_transposed else -1]
    assert m % tm == 0 and n % tn == 0
    in_specs = [pl.BlockSpec((tm, x.shape[1]), lambda i, j: (i, 0)) for x in xs]
    w_block = (tn, kdim) if w_transposed else (kdim, tn)
    lead = () if layer is None else (None,)

    def w_index(i, j):
        pos = (j, 0) if w_transposed else (0, j)
        return pos if layer is None else (layer,) + pos

    in_specs += [pl.BlockSpec(lead + w_block, w_index) for _ in ws]
    args = [*xs, *ws]
    if res is not None:
        in_specs.append(pl.BlockSpec((tm, tn), lambda i, j: (i, j)))
        args.append(res)
    return pl.pallas_call(
        functools.partial(_matmul_body, n_x=len(xs), n_w=len(ws), has_res=res is not None,
                          w_transposed=w_transposed),
        grid=(m // tm, n // tn),
        in_specs=in_specs,
        out_specs=pl.BlockSpec((tm, tn), lambda i, j: (i, j)),
        out_shape=jax.ShapeDtypeStruct((m, n), out_dtype),
        compiler_params=_cparams(("parallel", "parallel")),
        name=name,
    )(*args)


def _tail_proj_body(x_ref, w_ref, o_ref):
    w = w_ref[...].astype(x_ref.dtype)
    w = jnp.concatenate([w, jnp.zeros_like(w)], axis=0)
    o_ref[...] = lax.dot_general(x_ref[...], w, (((1,), (1,)), ((), ())),
                                 preferred_element_type=jnp.float32)


def _tail_proj(x, w_t, *, layer, row0, tm, name):
    m, kdim = x.shape
    rows = w_t.shape[1] - row0
    assert rows * 2 == LANES and row0 % rows == 0
    return pl.pallas_call(
        _tail_proj_body,
        grid=(m // tm,),
        in_specs=[pl.BlockSpec((tm, kdim), lambda i: (i, 0)),
                  pl.BlockSpec((None, rows, kdim), lambda i: (layer, row0 // rows, 0))],
        out_specs=pl.BlockSpec((tm, LANES), lambda i: (i, 0)),
        out_shape=jax.ShapeDtypeStruct((m, LANES), jnp.float32),
        compiler_params=_cparams(("parallel",)),
        name=name,
    )(x, w_t)


def _matmul_kouter_body(x_ref, w_ref, res_ref, o_ref, acc_ref, *, nk):
    k = pl.program_id(2)
    j = pl.program_id(3)
    prod = jnp.dot(x_ref[...], w_ref[...].astype(x_ref.dtype), preferred_element_type=jnp.float32)

    @pl.when(k == 0)
    def _():
        acc_ref[j] = prod

    if nk > 2:
        @pl.when(jnp.logical_and(k > 0, k < nk - 1))
        def _():
            acc_ref[j] += prod

    @pl.when(k == nk - 1)
    def _():
        o_ref[...] = (acc_ref[j] + prod + res_ref[...]).astype(o_ref.dtype)


def _matmul_kouter(x, w, res, *, layer, tm, tn, tk, n_split, name):
    m, kdim = x.shape
    n = w.shape[-1]
    nk = kdim // tk
    nj = n // (tn * n_split)
    assert m % tm == 0 and n % (tn * n_split) == 0 and kdim % tk == 0 and nk >= 2

    def out_index(hf, i, k, j):
        return i, hf * nj + jnp.where(k == nk - 1, j, 0)

    return pl.pallas_call(
        functools.partial(_matmul_kouter_body, nk=nk),
        grid=(n_split, m // tm, nk, nj),
        in_specs=[pl.BlockSpec((tm, tk), lambda hf, i, k, j: (i, k)),
                  pl.BlockSpec((None, tk, tn), lambda hf, i, k, j: (layer, k, hf * nj + j)),
                  pl.BlockSpec((tm, tn), out_index)],
        out_specs=pl.BlockSpec((tm, tn), out_index),
        out_shape=jax.ShapeDtypeStruct((m, n), res.dtype),
        scratch_shapes=[pltpu.VMEM((nj, tm, tn), jnp.float32)],
        compiler_params=_cparams(("parallel", "parallel", "arbitrary", "arbitrary")),
        name=name,
    )(x, w, res)


def _prep_head(x, g_ref, tabs, *, norm, n_norm, rope_blocks, shifts, scale, transpose, ones_rows):
    width = x.shape[-1]
    if norm is not None:
        if width <= MXU_SUM_MAX_WIDTH:
            r = lax.broadcasted_iota(jnp.int32, (width, width), 0)
            c = lax.broadcasted_iota(jnp.int32, (width, width), 1)
            group = LANES // 2 if norm == "half" else width
            same = (r // group == c // group).astype(jnp.bfloat16)
            ss = jnp.dot((x * x).astype(jnp.bfloat16), same, preferred_element_type=jnp.float32)
        else:
            ss = jnp.sum(x * x, axis=-1, keepdims=True)
        x = x * lax.rsqrt(ss * (1.0 / n_norm) + EPS) * g_ref[...]

    if rope_blocks:
        c_ref, s1_ref, s2_ref = tabs
        blocks = []
        for b in range(width // LANES):
            xb = x[:, b * LANES:(b + 1) * LANES]
            if b in rope_blocks:
                sl = slice(b * LANES, (b + 1) * LANES)
                xb = (xb * c_ref[:, sl]
                      + pltpu.roll(xb, shifts[0], 1) * s1_ref[:, sl]
                      + pltpu.roll(xb, shifts[1], 1) * s2_ref[:, sl])
            blocks.append(xb)
        x = blocks[0] if len(blocks) == 1 else jnp.concatenate(blocks, axis=-1)

    if scale != 1.0:
        x = x * scale
    x = x.astype(jnp.bfloat16)
    if transpose:
        eye = (lax.broadcasted_iota(jnp.int32, (width, width), 0)
               == lax.broadcasted_iota(jnp.int32, (width, width), 1)).astype(jnp.bfloat16)
        x = lax.dot_general(eye, x, (((1,), (1,)), ((), ())),
                            preferred_element_type=jnp.float32).astype(jnp.bfloat16)
    if ones_rows:
        x = jnp.concatenate([x, jnp.ones((ones_rows, x.shape[1]), x.dtype)], axis=0)
    return x


def _prep_body(*refs, nx, pieces, has_gain, has_tabs, **head_kw):
    x_refs = refs[:nx]
    pos = nx
    g_ref = None
    if has_gain:
        g_ref = refs[pos]
        pos += 1
    tabs = None
    if has_tabs:
        tabs = refs[pos:pos + 3]
        pos += 3
    o_ref = refs[pos]
    for h, head_pieces in enumerate(pieces):
        cols = [x_refs[src][:, off:off + LANES] for src, off in head_pieces]
        x = cols[0] if len(cols) == 1 else jnp.concatenate(cols, axis=-1)
        y = _prep_head(x, g_ref, tabs, **head_kw)
        o_ref[h] = y.reshape(o_ref.shape[1:])


def _prep(srcs, pieces, *, out_kind, tm, gain=None, norm=None, n_norm=None, tabs=None,
          rope_blocks=(), shifts=(0, 0), scale=1.0, name="prep"):
    s = srcs[0][0].shape[0]
    n_heads = len(pieces)
    width = LANES * len(pieces[0])
    in_specs, args = [], []
    for arr, bw, idx in srcs:
        in_specs.append(pl.BlockSpec((tm, bw), functools.partial(lambda i, idx: (i, idx), idx=idx)))
        args.append(arr)
    if norm is not None:
        in_specs.append(pl.BlockSpec((1, width), lambda i: (0, 0)))
        args.append(gain.reshape(1, width))
    if rope_blocks:
        for t in tabs:
            in_specs.append(pl.BlockSpec((tm, width), lambda i: (i, 0)))
            args.append(t)
    if out_kind == "rows":
        out_shape = (n_heads, s, width)
        out_spec = pl.BlockSpec((n_heads, tm, width), lambda i: (0, i, 0))
    elif out_kind == "cols":
        out_shape = (n_heads, s // tm, width, tm)
        out_spec = pl.BlockSpec((n_heads, 1, width, tm), lambda i: (0, i, 0, 0))
    else:
        out_shape = (n_heads, s // tm, width + ONES_ROWS, tm)
        out_spec = pl.BlockSpec((n_heads, 1, width + ONES_ROWS, tm), lambda i: (0, i, 0, 0))
    return pl.pallas_call(
        functools.partial(_prep_body, nx=len(srcs), pieces=tuple(tuple(p) for p in pieces),
                          has_gain=norm is not None, has_tabs=bool(rope_blocks),
                          norm=norm, n_norm=n_norm, rope_blocks=tuple(rope_blocks), shifts=shifts,
                          scale=scale, transpose=out_kind != "rows",
                          ones_rows=ONES_ROWS if out_kind == "chunks" else 0),
        grid=(s // tm,),
        in_specs=in_specs,
        out_specs=out_spec,
        out_shape=jax.ShapeDtypeStruct(out_shape, jnp.bfloat16),
        compiler_params=_cparams(("parallel",)),
        name=name,
    )(*args)


def _attn_body(*refs, nchunk, tk, tq, diff, lambda_init):
    if diff:
        (qT_ref, k_ref, vT_ref, shift_ref, lam_ref, sub_ref, o_ref,
         acc_ref, p_ref, qq_ref, m_ref, s_ref, ps_ref) = refs
    else:
        qT_ref, k_ref, vT_ref, shift_ref, o_ref, acc_ref, p_ref, qq_ref, m_ref, s_ref, ps_ref = refs

    i = pl.program_id(1)
    nqb, _, nq = qq_ref.shape
    dv = o_ref.shape[1]
    tiles_per_block = tq // qT_ref.shape[3]
    chunk_bits = nchunk.bit_length() - 1
    assert nchunk == 1 << chunk_bits

    def scores(step):
        chunk = step & (nchunk - 1)
        block = jnp.minimum(i + (step >> chunk_bits), nqb - 1)
        start = pl.multiple_of(chunk * tk, tk)
        return jnp.dot(k_ref[0, pl.ds(start, tk), :], qq_ref[block],
                       preferred_element_type=jnp.float32)

    def probs(step):
        return jnp.exp2(scores(step) - shift_ref[...]).astype(jnp.bfloat16)

    @pl.when(i == 0)
    def _():
        for b in range(nqb):
            tiles = [qT_ref[0, b * tiles_per_block + c] for c in range(tiles_per_block)]
            q = tiles[0] if len(tiles) == 1 else jnp.concatenate(tiles, axis=1)
            if diff:
                first = lax.broadcasted_iota(jnp.int32, q.shape, 0) < B_QK_DIM
                zero = jnp.zeros_like(q)
                q = jnp.concatenate([jnp.where(first, q, zero), jnp.where(first, zero, q)], axis=1)
            qq_ref[b] = q
        p_ref[0] = probs(0)

    acc_ref[...] = jnp.zeros(acc_ref.shape, jnp.float32)

    def trip(t, carry):
        for u in range(ATTN_CHUNKS_PER_TRIP):
            j = t * ATTN_CHUNKS_PER_TRIP + u
            p_ref[(u + 1) % 2] = probs(j + 1)
            acc_ref[...] += jnp.dot(vT_ref[0, j], p_ref[u % 2], preferred_element_type=jnp.float32)
        return carry

    lax.fori_loop(0, nchunk // ATTN_CHUNKS_PER_TRIP, trip, 0)

    healthy = jnp.min(acc_ref[dv:dv + 1, :]) >= ATTN_MIN_DENOM

    @pl.when(jnp.logical_not(healthy))
    def _():
        rb = ATTN_ROW_BLOCK
        groups = rb // SUBLANES
        m_ref[...] = jnp.full(m_ref.shape, -jnp.inf, jnp.float32)
        acc_ref[...] = jnp.zeros(acc_ref.shape, jnp.float32)

        def block(r):
            return s_ref[r * rb:(r + 1) * rb, :].reshape(groups, SUBLANES, nq)

        def chunk(j, carry):
            s_ref[...] = scores(j)
            mx = jnp.max(block(0), axis=0)
            for r in range(1, tk // rb):
                mx = jnp.maximum(mx, jnp.max(block(r), axis=0))
            m_old = m_ref[...]
            m_new = jnp.maximum(m_old, jnp.max(mx, axis=0, keepdims=True))
            alpha = jnp.exp2(m_old - m_new)
            m_rows = jnp.broadcast_to(m_new, (SUBLANES, nq))[None]
            for r in range(tk // rb):
                p = jnp.exp2(block(r) - m_rows).reshape(rb, nq)
                ps_ref[r * rb:(r + 1) * rb, :] = p.astype(jnp.bfloat16)
            pv = jnp.dot(vT_ref[0, j], ps_ref[...], preferred_element_type=jnp.float32)
            acc_ref[...] = acc_ref[...] * alpha + pv
            m_ref[...] = m_new
            return carry

        lax.fori_loop(0, nchunk, chunk, 0)

    o = acc_ref[:dv, :] / acc_ref[dv:dv + 1, :]
    if diff:
        lp = lam_ref[...]
        lam = (jnp.exp(jnp.sum(lp[0:1] * lp[1:2], axis=-1, keepdims=True))
               - jnp.exp(jnp.sum(lp[2:3] * lp[3:4], axis=-1, keepdims=True))
               + lambda_init)
        o = o[:, :tq] - lam * o[:, tq:]
        ms = jnp.mean(o * o, axis=0, keepdims=True)
        o = o * lax.rsqrt(ms + EPS) * sub_ref[...] * (1.0 - lambda_init)
    o_ref[...] = o.T.astype(o_ref.dtype)


def _attention(qT, k, vT, score_bound, *, tq, diff=False, lam_p=None, subln=None, lambda_init=0.0, name):
    n_heads, n_qtiles, dq, q_tile = qT.shape
    s = n_qtiles * q_tile
    n_kv = k.shape[0]
    group = n_heads // n_kv
    _, nchunk, dv_ext, tk = vT.shape
    dv = dv_ext - ONES_ROWS
    nq = 2 * tq if diff else tq
    assert tq % q_tile == 0 and ATTN_CHUNKS_PER_TRIP % 2 == 0 and nchunk % ATTN_CHUNKS_PER_TRIP == 0
    in_specs = [pl.BlockSpec((1, n_qtiles, dq, q_tile), lambda h, i: (h, 0, 0, 0)),
                pl.BlockSpec((1, s, dq), lambda h, i: (h // group, 0, 0)),
                pl.BlockSpec((1, nchunk, dv_ext, tk), lambda h, i: (h // group, 0, 0, 0)),
                pl.BlockSpec((1, nq), lambda h, i: (0, 0))]
    args = [qT, k, vT, jnp.full((1, nq), score_bound, jnp.float32)]
    if diff:
        in_specs += [pl.BlockSpec(lam_p.shape, lambda h, i: (0, 0)),
                     pl.BlockSpec((dv, 1), lambda h, i: (0, 0))]
        args += [lam_p, subln.reshape(dv, 1)]
    return pl.pallas_call(
        functools.partial(_attn_body, nchunk=nchunk, tk=tk, tq=tq, diff=diff, lambda_init=lambda_init),
        grid=(n_heads, s // tq),
        in_specs=in_specs,
        out_specs=pl.BlockSpec((tq, dv), lambda h, i: (i, h)),
        out_shape=jax.ShapeDtypeStruct((s, n_heads * dv), jnp.bfloat16),
        scratch_shapes=[pltpu.VMEM((dv_ext, nq), jnp.float32),
                        pltpu.VMEM((2, tk, nq), jnp.bfloat16),
                        pltpu.VMEM((s // tq, dq, nq), jnp.bfloat16),
                        pltpu.VMEM((1, nq), jnp.float32),
                        pltpu.VMEM((tk, nq), jnp.float32),
                        pltpu.VMEM((tk, nq), jnp.bfloat16)],
        compiler_params=_cparams(("arbitrary", "arbitrary")),
        name=name,
    )(*args)


def _rope_cos_sin(pos, dim, theta):
    inv = theta ** (-jnp.arange(0, dim, 2, dtype=jnp.float32) / dim)
    ang = pos.astype(jnp.float32)[:, None] * inv[None, :]
    return jnp.cos(ang), jnp.sin(ang)


def _rotate_half_tables(cos, sin):
    zero = jnp.zeros_like(sin)
    return (jnp.concatenate([cos, cos], axis=-1),
            jnp.concatenate([-sin, zero], axis=-1),
            jnp.concatenate([zero, sin], axis=-1))


def _rope_tables(s):
    t = jnp.arange(s, dtype=jnp.int32)
    row = t // GRID_W
    col = t % GRID_W
    half = A_HEAD_DIM // 2
    a_row = _rotate_half_tables(*_rope_cos_sin(row, half, A_ROPE_THETA))
    a_col = _rotate_half_tables(*_rope_cos_sin(col, half, A_ROPE_THETA))
    a_tabs = tuple(jnp.concatenate([r, c], axis=-1) for r, c in zip(a_row, a_col))

    b_rot = _rotate_half_tables(*_rope_cos_sin(t, B_ROPE_DIM, B_ROPE_THETA))
    rest = B_QK_DIM - B_ROPE_DIM
    fill = (jnp.ones((s, rest), jnp.float32), jnp.zeros((s, rest), jnp.float32),
            jnp.zeros((s, rest), jnp.float32))
    b_tabs = tuple(jnp.tile(jnp.concatenate([r, f], axis=-1), (1, 2)) for r, f in zip(b_rot, fill))

    c_rot = _rotate_half_tables(*_rope_cos_sin(t, C_ROPE, C_ROPE_THETA))
    ones = jnp.ones((s, C_NOPE), jnp.float32)
    zeros = jnp.zeros((s, C_NOPE), jnp.float32)
    pad = jnp.zeros((s, C_QK_PAD - C_QK_DIM), jnp.float32)
    c_tabs = (jnp.concatenate([ones, c_rot[0], pad], axis=-1),
              jnp.concatenate([zeros, c_rot[1], pad], axis=-1),
              jnp.concatenate([zeros, c_rot[2], pad], axis=-1))
    return a_tabs, b_tabs, c_tabs


def _score_bound(dim, q_gain, k_gain, scale):
    return BOUND_SLACK * dim * scale * jnp.max(jnp.abs(q_gain)) * jnp.max(jnp.abs(k_gain))


def _pad_vec(g, n):
    return jnp.pad(g, (0, n - g.shape[0]))


def kernel(x, attn_norm, w_in, a_q_norm, a_k_norm, b_q_norm, b_k_norm, b_lambda, b_subln,
           c_q_a_norm, c_kv_a_norm, c_w_q_up, c_w_kv_up, c_q_norm, c_k_norm, w_out,
           ffn_norm, w_gate, w_up, w_down):
    bsz, s, d = x.shape
    assert (bsz, s, d) == (1, SEQ, D_MODEL)
    bf = jnp.bfloat16
    a_tabs, b_tabs, c_tabs = _rope_tables(s)
    tp = 512
    h = x.reshape(s, d)
    w_in_t = jnp.swapaxes(w_in, 1, 2)

    for l in range(DEPTH):
        lambda_init = 0.8 - 0.6 * math.exp(-0.3 * l)
        wq_up = c_w_q_up[l].astype(bf).reshape(C_Q_LORA, C_HEADS, C_QK_DIM)
        wq_up = jnp.pad(wq_up, ((0, 0), (0, 0), (0, C_QK_PAD - C_QK_DIM))).reshape(C_Q_LORA, C_HEADS * C_QK_PAD)
        wkv_up = c_w_kv_up[l].astype(bf)

        xn = _rmsnorm(h, attn_norm[l])
        proj = _matmul([xn], [w_in_t], out_dtype=jnp.float32, tm=1024, tn=PROJ_BLOCK, layer=l,
                       w_transposed=True, n_cols=N_IN_MAIN, name="in_proj")
        c_kr = _tail_proj(xn, w_in_t, layer=l, row0=N_IN_MAIN, tm=1024, name="in_proj_kr")

        def proj_blocks(first, count):
            return [(proj, PROJ_BLOCK, first + b) for b in range(count)]

        a_scale = LOG2E / math.sqrt(A_HEAD_DIM)
        a_q = _prep(proj_blocks(0, 3), _head_pieces(A_HEADS), out_kind="cols", tm=tp,
                    gain=a_q_norm[l], norm="full", n_norm=A_HEAD_DIM, tabs=a_tabs, rope_blocks=(0,),
                    shifts=(96, 32), scale=a_scale, name="prep_a_q")
        a_k = _prep(proj_blocks(3, 1), _head_pieces(A_KV_HEADS), out_kind="rows", tm=tp,
                    gain=a_k_norm[l], norm="full", n_norm=A_HEAD_DIM, tabs=a_tabs, rope_blocks=(0,),
                    shifts=(96, 32), name="prep_a_k")
        a_v = _prep(proj_blocks(4, 1), _head_pieces(A_KV_HEADS), out_kind="chunks", tm=ATTN_KEY_CHUNK,
                    name="prep_a_v")
        ya = _attention(a_q, a_k, a_v, _score_bound(A_HEAD_DIM, a_q_norm[l], a_k_norm[l], a_scale),
                        tq=1024, name="attn_a")

        b_scale = LOG2E / math.sqrt(B_QK_DIM)
        b_q = _prep(proj_blocks(5, 2), _head_pieces(B_HEADS), out_kind="cols", tm=tp,
                    gain=jnp.tile(b_q_norm[l], 2), norm="half", n_norm=B_QK_DIM, tabs=b_tabs,
                    rope_blocks=(0,), shifts=(120, 8), scale=b_scale, name="prep_b_q")
        b_k = _prep(proj_blocks(7, 2), _head_pieces(B_HEADS), out_kind="rows", tm=tp,
                    gain=jnp.tile(b_k_norm[l], 2), norm="half", n_norm=B_QK_DIM, tabs=b_tabs,
                    rope_blocks=(0,), shifts=(120, 8), name="prep_b_k")
        b_v = _prep(proj_blocks(9, 2), _head_pieces(B_HEADS), out_kind="chunks", tm=ATTN_KEY_CHUNK,
                    name="prep_b_v")
        yb = _attention(b_q, b_k, b_v, _score_bound(B_QK_DIM, b_q_norm[l], b_k_norm[l], b_scale),
                        tq=1024, diff=True, lam_p=b_lambda[l], subln=b_subln[l],
                        lambda_init=lambda_init, name="attn_b")

        cq_lat = _prep(proj_blocks(11, 2), _head_pieces(1, blocks=C_Q_LORA // LANES), out_kind="rows", tm=tp,
                       gain=c_q_a_norm[l], norm="full", n_norm=C_Q_LORA, name="norm_c_q")[0]
        ckv_lat = _prep(proj_blocks(13, 1), _head_pieces(1, blocks=C_KV_LORA // LANES), out_kind="rows", tm=tp,
                        gain=c_kv_a_norm[l], norm="full", n_norm=C_KV_LORA, name="norm_c_kv")[0]
        c_q_raw = _matmul([cq_lat], [wq_up], out_dtype=jnp.float32, tm=1024, tn=1536, name="c_q_up")
        c_kv_raw = _matmul([ckv_lat], [wkv_up], out_dtype=jnp.float32, tm=1024, tn=1536, name="c_kv_up")
        c_scale = LOG2E / math.sqrt(C_QK_DIM)
        n_up = C_HEADS * C_QK_PAD // PROJ_BLOCK
        c_q = _prep([(c_q_raw, PROJ_BLOCK, b) for b in range(n_up)],
                    _head_pieces(C_HEADS, lanes_per_head=C_QK_PAD, blocks=2), out_kind="cols", tm=tp,
                    gain=_pad_vec(c_q_norm[l], C_QK_PAD), norm="full", n_norm=C_QK_DIM, tabs=c_tabs,
                    rope_blocks=(1,), shifts=(96, 32), scale=c_scale, name="prep_c_q")
        kv_srcs = [(c_kv_raw, PROJ_BLOCK, b) for b in range(n_up)]
        c_k_pieces = [[nope, (n_up, 0)] for (nope,) in _head_pieces(C_HEADS, lanes_per_head=2 * LANES)]
        c_k = _prep(kv_srcs + [(c_kr, LANES, 0)], c_k_pieces, out_kind="rows", tm=tp,
                    gain=_pad_vec(c_k_norm[l], C_QK_PAD), norm="full", n_norm=C_QK_DIM, tabs=c_tabs,
                    rope_blocks=(1,), shifts=(96, 32), name="prep_c_k")
        c_v = _prep(kv_srcs, _head_pieces(C_HEADS, lanes_per_head=2 * LANES, lane0=LANES), out_kind="chunks",
                    tm=ATTN_KEY_CHUNK, name="prep_c_v")
        yc = _attention(c_q, c_k, c_v, _score_bound(C_QK_DIM, c_q_norm[l], c_k_norm[l], c_scale),
                        tq=1024, name="attn_c")

        h = _matmul([ya, yb, yc], [w_out], res=h, out_dtype=jnp.float32, tm=1024, tn=512, layer=l,
                    name="out_proj")

        hn = _rmsnorm(h, ffn_norm[l])
        act = _matmul([hn], [w_gate, w_up], out_dtype=bf, tm=2048, tn=256, layer=l, name="ffn_gate_up")
        h = _matmul_kouter(act, w_down, h, layer=l, tm=1024, tn=256, tk=FFN_HIDDEN // 2, n_split=1,
                           name="ffn_down")

    return h.reshape(bsz, s, d)
```

```python
import functools
import math

import jax
import jax.numpy as jnp
from jax import lax
from jax.experimental import pallas as pl
from jax.experimental.pallas import tpu as pltpu

D_MODEL = 4096
SEQ = 8192
DEPTH = 2
GRID_W = 64
EPS = 1e-6

A_HEAD_DIM = 128
A_HEADS = 12
A_KV_HEADS = 4
A_ROPE_THETA = 10000.0

B_QK_DIM = 64
B_V_DIM = 128
B_HEADS = 8
B_ROPE_DIM = 16
B_ROPE_THETA = 500000.0

C_V_DIM = 128
C_HEADS = 12
C_Q_LORA = 1024
C_KV_LORA = 512
C_NOPE = 128
C_ROPE = 64
C_QK_DIM = C_NOPE + C_ROPE
C_QK_PAD = 256
C_ROPE_THETA = 10000.0

FFN_HIDDEN = 11008

LANES = 128
SUBLANES = 8
ATTN_ROW_BLOCK = 64
ATTN_KEY_CHUNK = 512
ATTN_CHUNKS_PER_TRIP = 8
ATTN_MIN_DENOM = 2.0 ** -60
BOUND_SLACK = 1.01
MXU_SUM_MAX_WIDTH = 256
ONES_ROWS = 16
VMEM_LIMIT_BYTES = 56 * 1024 * 1024

LOG2E = math.log2(math.e)

PROJ_BLOCK = 512
N_IN_MAIN = 14 * PROJ_BLOCK


def _head_pieces(n_heads, lanes_per_head=LANES, lane0=0, blocks=1):
    out = []
    for h in range(n_heads):
        head = []
        for b in range(blocks):
            col = lane0 + h * lanes_per_head + b * LANES
            head.append((col // PROJ_BLOCK, col % PROJ_BLOCK))
        out.append(head)
    return out


def _cparams(sem):
    return pltpu.CompilerParams(dimension_semantics=sem, vmem_limit_bytes=VMEM_LIMIT_BYTES)


def _rmsnorm_body(x_ref, g_ref, o_ref):
    x = x_ref[...]
    ms = jnp.mean(x * x, axis=-1, keepdims=True)
    o_ref[...] = (x * lax.rsqrt(ms + EPS) * g_ref[...]).astype(o_ref.dtype)


def _rmsnorm(x, g, tm=256):
    s, d = x.shape
    return pl.pallas_call(
        _rmsnorm_body,
        grid=(s // tm,),
        in_specs=[pl.BlockSpec((tm, d), lambda i: (i, 0)),
                  pl.BlockSpec((1, d), lambda i: (0, 0))],
        out_specs=pl.BlockSpec((tm, d), lambda i: (i, 0)),
        out_shape=jax.ShapeDtypeStruct((s, d), jnp.bfloat16),
        compiler_params=_cparams(("parallel",)),
        name="rmsnorm",
    )(x, g.reshape(1, d))


def _matmul_body(*refs, n_x, n_w, has_res, w_transposed):
    x_refs = refs[:n_x]
    w_refs = refs[n_x:n_x + n_w]
    pos = n_x + n_w
    res_ref = refs[pos] if has_res else None
    o_ref = refs[pos + int(has_res)]

    x = x_refs[0][...] if n_x == 1 else jnp.concatenate([r[...] for r in x_refs], axis=1)
    tiles = [w[...] if w.dtype == x.dtype else w[...].astype(x.dtype) for w in w_refs]
    contract = (((1,), (1,)), ((), ())) if w_transposed else (((1,), (0,)), ((), ()))
    prods = [lax.dot_general(x, w, contract, preferred_element_type=jnp.float32) for w in tiles]
    if n_w == 2:
        g, u = prods
        y = g * jax.nn.sigmoid(g) * u
    else:
        y = prods[0]
    if has_res:
        y = y + res_ref[...]
    o_ref[...] = y.astype(o_ref.dtype)


def _matmul(xs, ws, res=None, *, out_dtype, tm, tn, name, layer=None, w_transposed=False, n_cols=None):
    m = xs[0].shape[0]
    kdim = sum(x.shape[1] for x in xs)
    n = n_cols if n_cols is not None else ws[0].shape[-2 if w_transposed else -1]
    assert m % tm == 0 and n % tn == 0
    in_specs = [pl.BlockSpec((tm, x.shape[1]), lambda i, j: (i, 0)) for x in xs]
    w_block = (tn, kdim) if w_transposed else (kdim, tn)
    lead = () if layer is None else (None,)

    def w_index(i, j):
        pos = (j, 0) if w_transposed else (0, j)
        return pos if layer is None else (layer,) + pos

    in_specs += [pl.BlockSpec(lead + w_block, w_index) for _ in ws]
    args = [*xs, *ws]
    if res is not None:
        in_specs.append(pl.BlockSpec((tm, tn), lambda i, j: (i, j)))
        args.append(res)
    return pl.pallas_call(
        functools.partial(_matmul_body, n_x=len(xs), n_w=len(ws), has_res=res is not None,
                          w_transposed=w_transposed),
        grid=(m // tm, n // tn),
        in_specs=in_specs,
        out_specs=pl.BlockSpec((tm, tn), lambda i, j: (i, j)),
        out_shape=jax.ShapeDtypeStruct((m, n), out_dtype),
        compiler_params=_cparams(("parallel", "parallel")),
        name=name,
    )(*args)


def _tail_proj_body(x_ref, w_ref, o_ref):
    w = w_ref[...].astype(x_ref.dtype)
    w = jnp.concatenate([w, jnp.zeros_like(w)], axis=0)
    o_ref[...] = lax.dot_general(x_ref[...], w, (((1,), (1,)), ((), ())),
                                 preferred_element_type=jnp.float32)


def _tail_proj(x, w_t, *, layer, row0, tm, name):
    m, kdim = x.shape
    rows = w_t.shape[1] - row0
    assert rows * 2 == LANES and row0 % rows == 0
    return pl.pallas_call(
        _tail_proj_body,
        grid=(m // tm,),
        in_specs=[pl.BlockSpec((tm, kdim), lambda i: (i, 0)),
                  pl.BlockSpec((None, rows, kdim), lambda i: (layer, row0 // rows, 0))],
        out_specs=pl.BlockSpec((tm, LANES), lambda i: (i, 0)),
        out_shape=jax.ShapeDtypeStruct((m, LANES), jnp.float32),
        compiler_params=_cparams(("parallel",)),
        name=name,
    )(x, w_t)


def _matmul_kouter_body(x_ref, w_ref, res_ref, o_ref, acc_ref, *, nk):
    k = pl.program_id(2)
    j = pl.program_id(3)
    prod = jnp.dot(x_ref[...], w_ref[...].astype(x_ref.dtype), preferred_element_type=jnp.float32)

    @pl.when(k == 0)
    def _():
        acc_ref[j] = prod

    if nk > 2:
        @pl.when(jnp.logical_and(k > 0, k < nk - 1))
        def _():
            acc_ref[j] += prod

    @pl.when(k == nk - 1)
    def _():
        o_ref[...] = (acc_ref[j] + prod + res_ref[...]).astype(o_ref.dtype)


def _matmul_kouter(x, w, res, *, layer, tm, tn, tk, n_split, name):
    m, kdim = x.shape
    n = w.shape[-1]
    nk = kdim // tk
    nj = n // (tn * n_split)
    assert m % tm == 0 and n % (tn * n_split) == 0 and kdim % tk == 0 and nk >= 2

    def out_index(hf, i, k, j):
        return i, hf * nj + jnp.where(k == nk - 1, j, 0)

    return pl.pallas_call(
        functools.partial(_matmul_kouter_body, nk=nk),
        grid=(n_split, m // tm, nk, nj),
        in_specs=[pl.BlockSpec((tm, tk), lambda hf, i, k, j: (i, k)),
                  pl.BlockSpec((None, tk, tn), lambda hf, i, k, j: (layer, k, hf * nj + j)),
                  pl.BlockSpec((tm, tn), out_index)],
        out_specs=pl.BlockSpec((tm, tn), out_index),
        out_shape=jax.ShapeDtypeStruct((m, n), res.dtype),
        scratch_shapes=[pltpu.VMEM((nj, tm, tn), jnp.float32)],
        compiler_params=_cparams(("parallel", "parallel", "arbitrary", "arbitrary")),
        name=name,
    )(x, w, res)


def _prep_head(x, g_ref, tabs, *, norm, n_norm, rope_blocks, shifts, scale, transpose, ones_rows):
    width = x.shape[-1]
    if norm is not None:
        if width <= MXU_SUM_MAX_WIDTH:
            r = lax.broadcasted_iota(jnp.int32, (width, width), 0)
            c = lax.broadcasted_iota(jnp.int32, (width, width), 1)
            group = LANES // 2 if norm == "half" else width
            same = (r // group == c // group).astype(jnp.bfloat16)
            ss = jnp.dot((x * x).astype(jnp.bfloat16), same, preferred_element_type=jnp.float32)
        else:
            ss = jnp.sum(x * x, axis=-1, keepdims=True)
        x = x * lax.rsqrt(ss * (1.0 / n_norm) + EPS) * g_ref[...]

    if rope_blocks:
        c_ref, s1_ref, s2_ref = tabs
        blocks = []
        for b in range(width // LANES):
            xb = x[:, b * LANES:(b + 1) * LANES]
            if b in rope_blocks:
                sl = slice(b * LANES, (b + 1) * LANES)
                xb = (xb * c_ref[:, sl]
                      + pltpu.roll(xb, shifts[0], 1) * s1_ref[:, sl]
                      + pltpu.roll(xb, shifts[1], 1) * s2_ref[:, sl])
            blocks.append(xb)
        x = blocks[0] if len(blocks) == 1 else jnp.concatenate(blocks, axis=-1)

    if scale != 1.0:
        x = x * scale
    x = x.astype(jnp.bfloat16)
    if transpose:
        eye = (lax.broadcasted_iota(jnp.int32, (width, width), 0)
               == lax.broadcasted_iota(jnp.int32, (width, width), 1)).astype(jnp.bfloat16)
        x = lax.dot_general(eye, x, (((1,), (1,)), ((), ())),
                            preferred_element_type=jnp.float32).astype(jnp.bfloat16)
    if ones_rows:
        x = jnp.concatenate([x, jnp.ones((ones_rows, x.shape[1]), x.dtype)], axis=0)
    return x


def _prep_body(*refs, nx, pieces, has_gain, has_tabs, **head_kw):
    x_refs = refs[:nx]
    pos = nx
    g_ref = None
    if has_gain:
        g_ref = refs[pos]
        pos += 1
    tabs = None
    if has_tabs:
        tabs = refs[pos:pos + 3]
        pos += 3
    o_ref = refs[pos]
    for h, head_pieces in enumerate(pieces):
        cols = [x_refs[src][:, off:off + LANES] for src, off in head_pieces]
        x = cols[0] if len(cols) == 1 else jnp.concatenate(cols, axis=-1)
        y = _prep_head(x, g_ref, tabs, **head_kw)
        o_ref[h] = y.reshape(o_ref.shape[1:])


def _prep(srcs, pieces, *, out_kind, tm, gain=None, norm=None, n_norm=None, tabs=None,
          rope_blocks=(), shifts=(0, 0), scale=1.0, name="prep"):
    s = srcs[0][0].shape[0]
    n_heads = len(pieces)
    width = LANES * len(pieces[0])
    in_specs, args = [], []
    for arr, bw, idx in srcs:
        in_specs.append(pl.BlockSpec((tm, bw), functools.partial(lambda i, idx: (i, idx), idx=idx)))
        args.append(arr)
    if norm is not None:
        in_specs.append(pl.BlockSpec((1, width), lambda i: (0, 0)))
        args.append(gain.reshape(1, width))
    if rope_blocks:
        for t in tabs:
            in_specs.append(pl.BlockSpec((tm, width), lambda i: (i, 0)))
            args.append(t)
    if out_kind == "rows":
        out_shape = (n_heads, s, width)
        out_spec = pl.BlockSpec((n_heads, tm, width), lambda i: (0, i, 0))
    elif out_kind == "cols":
        out_shape = (n_heads, s // tm, width, tm)
        out_spec = pl.BlockSpec((n_heads, 1, width, tm), lambda i: (0, i, 0, 0))
    else:
        out_shape = (n_heads, s // tm, width + ONES_ROWS, tm)
        out_spec = pl.BlockSpec((n_heads, 1, width + ONES_ROWS, tm), lambda i: (0, i, 0, 0))
    return pl.pallas_call(
        functools.partial(_prep_body, nx=len(srcs), pieces=tuple(tuple(p) for p in pieces),
                          has_gain=norm is not None, has_tabs=bool(rope_blocks),
                          norm=norm, n_norm=n_norm, rope_blocks=tuple(rope_blocks), shifts=shifts,
                          scale=scale, transpose=out_kind != "rows",
                          ones_rows=ONES_ROWS if out_kind == "chunks" else 0),
        grid=(s // tm,),
        in_specs=in_specs,
        out_specs=out_spec,
        out_shape=jax.ShapeDtypeStruct(out_shape, jnp.bfloat16),
        compiler_params=_cparams(("parallel",)),
        name=name,
    )(*args)


def _attn_body(*refs, nchunk, tk, tq, diff, lambda_init):
    if diff:
        (qT_ref, k_ref, vT_ref, shift_ref, lam_ref, sub_ref, o_ref,
         acc_ref, p_ref, qq_ref, m_ref, s_ref, ps_ref) = refs
    else:
        qT_ref, k_ref, vT_ref, shift_ref, o_ref, acc_ref, p_ref, qq_ref, m_ref, s_ref, ps_ref = refs

    i = pl.program_id(1)
    nqb, _, nq = qq_ref.shape
    dv = o_ref.shape[1]
    tiles_per_block = tq // qT_ref.shape[3]
    chunk_bits = nchunk.bit_length() - 1
    assert nchunk == 1 << chunk_bits

    def scores(step):
        chunk = step & (nchunk - 1)
        block = jnp.minimum(i + (step >> chunk_bits), nqb - 1)
        start = pl.multiple_of(chunk * tk, tk)
        return jnp.dot(k_ref[0, pl.ds(start, tk), :], qq_ref[block],
                       preferred_element_type=jnp.float32)

    def probs(step):
        return jnp.exp2(scores(step) - shift_ref[...]).astype(jnp.bfloat16)

    @pl.when(i == 0)
    def _():
        for b in range(nqb):
            tiles = [qT_ref[0, b * tiles_per_block + c] for c in range(tiles_per_block)]
            q = tiles[0] if len(tiles) == 1 else jnp.concatenate(tiles, axis=1)
            if diff:
                first = lax.broadcasted_iota(jnp.int32, q.shape, 0) < B_QK_DIM
                zero = jnp.zeros_like(q)
                q = jnp.concatenate([jnp.where(first, q, zero), jnp.where(first, zero, q)], axis=1)
            qq_ref[b] = q
        p_ref[0] = probs(0)

    acc_ref[...] = jnp.zeros(acc_ref.shape, jnp.float32)

    def trip(t, carry):
        for u in range(ATTN_CHUNKS_PER_TRIP):
            j = t * ATTN_CHUNKS_PER_TRIP + u
            p_ref[(u + 1) % 2] = probs(j + 1)
            acc_ref[...] += jnp.dot(vT_ref[0, j], p_ref[u % 2], preferred_element_type=jnp.float32)
        return carry

    lax.fori_loop(0, nchunk // ATTN_CHUNKS_PER_TRIP, trip, 0)

    healthy = jnp.min(acc_ref[dv:dv + 1, :]) >= ATTN_MIN_DENOM

    @pl.when(jnp.logical_not(healthy))
    def _():
        rb = ATTN_ROW_BLOCK
        groups = rb // SUBLANES
        m_ref[...] = jnp.full(m_ref.shape, -jnp.inf, jnp.float32)
        acc_ref[...] = jnp.zeros(acc_ref.shape, jnp.float32)

        def block(r):
            return s_ref[r * rb:(r + 1) * rb, :].reshape(groups, SUBLANES, nq)

        def chunk(j, carry):
            s_ref[...] = scores(j)
            mx = jnp.max(block(0), axis=0)
            for r in range(1, tk // rb):
                mx = jnp.maximum(mx, jnp.max(block(r), axis=0))
            m_old = m_ref[...]
            m_new = jnp.maximum(m_old, jnp.max(mx, axis=0, keepdims=True))
            alpha = jnp.exp2(m_old - m_new)
            m_rows = jnp.broadcast_to(m_new, (SUBLANES, nq))[None]
            for r in range(tk // rb):
                p = jnp.exp2(block(r) - m_rows).reshape(rb, nq)
                ps_ref[r * rb:(r + 1) * rb, :] = p.astype(jnp.bfloat16)
            pv = jnp.dot(vT_ref[0, j], ps_ref[...], preferred_element_type=jnp.float32)
            acc_ref[...] = acc_ref[...] * alpha + pv
            m_ref[...] = m_new
            return carry

        lax.fori_loop(0, nchunk, chunk, 0)

    o = acc_ref[:dv, :] / acc_ref[dv:dv + 1, :]
    if diff:
        lp = lam_ref[...]
        lam = (jnp.exp(jnp.sum(lp[0:1] * lp[1:2], axis=-1, keepdims=True))
               - jnp.exp(jnp.sum(lp[2:3] * lp[3:4], axis=-1, keepdims=True))
               + lambda_init)
        o = o[:, :tq] - lam * o[:, tq:]
        ms = jnp.mean(o * o, axis=0, keepdims=True)
        o = o * lax.rsqrt(ms + EPS) * sub_ref[...] * (1.0 - lambda_init)
    o_ref[...] = o.T.astype(o_ref.dtype)


def _attention(qT, k, vT, score_bound, *, tq, diff=False, lam_p=None, subln=None, lambda_init=0.0, name):
    n_heads, n_qtiles, dq, q_tile = qT.shape
    s = n_qtiles * q_tile
    n_kv = k.shape[0]
    group = n_heads // n_kv
    _, nchunk, dv_ext, tk = vT.shape
    dv = dv_ext - ONES_ROWS
    nq = 2 * tq if diff else tq
    assert tq % q_tile == 0 and ATTN_CHUNKS_PER_TRIP % 2 == 0 and nchunk % ATTN_CHUNKS_PER_TRIP == 0
    in_specs = [pl.BlockSpec((1, n_qtiles, dq, q_tile), lambda h, i: (h, 0, 0, 0)),
                pl.BlockSpec((1, s, dq), lambda h, i: (h // group, 0, 0)),
                pl.BlockSpec((1, nchunk, dv_ext, tk), lambda h, i: (h // group, 0, 0, 0)),
                pl.BlockSpec((1, nq), lambda h, i: (0, 0))]
    args = [qT, k, vT, jnp.full((1, nq), score_bound, jnp.float32)]
    if diff:
        in_specs += [pl.BlockSpec(lam_p.shape, lambda h, i: (0, 0)),
                     pl.BlockSpec((dv, 1), lambda h, i: (0, 0))]
        args += [lam_p, subln.reshape(dv, 1)]
    return pl.pallas_call(
        functools.partial(_attn_body, nchunk=nchunk, tk=tk, tq=tq, diff=diff, lambda_init=lambda_init),
        grid=(n_heads, s // tq),
        in_specs=in_specs,
        out_specs=pl.BlockSpec((tq, dv), lambda h, i: (i, h)),
        out_shape=jax.ShapeDtypeStruct((s, n_heads * dv), jnp.bfloat16),
        scratch_shapes=[pltpu.VMEM((dv_ext, nq), jnp.float32),
                        pltpu.VMEM((2, tk, nq), jnp.bfloat16),
                        pltpu.VMEM((s // tq, dq, nq), jnp.bfloat16),
                        pltpu.VMEM((1, nq), jnp.float32),
                        pltpu.VMEM((tk, nq), jnp.float32),
                        pltpu.VMEM((tk, nq), jnp.bfloat16)],
        compiler_params=_cparams(("arbitrary", "arbitrary")),
        name=name,
    )(*args)


def _rope_cos_sin(pos, dim, theta):
    inv = theta ** (-jnp.arange(0, dim, 2, dtype=jnp.float32) / dim)
    ang = pos.astype(jnp.float32)[:, None] * inv[None, :]
    return jnp.cos(ang), jnp.sin(ang)


def _rotate_half_tables(cos, sin):
    zero = jnp.zeros_like(sin)
    return (jnp.concatenate([cos, cos], axis=-1),
            jnp.concatenate([-sin, zero], axis=-1),
            jnp.concatenate([zero, sin], axis=-1))


def _rope_tables(s):
    t = jnp.arange(s, dtype=jnp.int32)
    row = t // GRID_W
    col = t % GRID_W
    half = A_HEAD_DIM // 2
    a_row = _rotate_half_tables(*_rope_cos_sin(row, half, A_ROPE_THETA))
    a_col = _rotate_half_tables(*_rope_cos_sin(col, half, A_ROPE_THETA))
    a_tabs = tuple(jnp.concatenate([r, c], axis=-1) for r, c in zip(a_row, a_col))

    b_rot = _rotate_half_tables(*_rope_cos_sin(t, B_ROPE_DIM, B_ROPE_THETA))
    rest = B_QK_DIM - B_ROPE_DIM
    fill = (jnp.ones((s, rest), jnp.float32), jnp.zeros((s, rest), jnp.float32),
            jnp.zeros((s, rest), jnp.float32))
    b_tabs = tuple(jnp.tile(jnp.concatenate([r, f], axis=-1), (1, 2)) for r, f in zip(b_rot, fill))

    c_rot = _rotate_half_tables(*_rope_cos_sin(t, C_ROPE, C_ROPE_THETA))
    ones = jnp.ones((s, C_NOPE), jnp.float32)
    zeros = jnp.zeros((s, C_NOPE), jnp.float32)
    pad = jnp.zeros((s, C_QK_PAD - C_QK_DIM), jnp.float32)
    c_tabs = (jnp.concatenate([ones, c_rot[0], pad], axis=-1),
              jnp.concatenate([zeros, c_rot[1], pad], axis=-1),
              jnp.concatenate([zeros, c_rot[2], pad], axis=-1))
    return a_tabs, b_tabs, c_tabs


def _score_bound(dim, q_gain, k_gain, scale):
    return BOUND_SLACK * dim * scale * jnp.max(jnp.abs(q_gain)) * jnp.max(jnp.abs(k_gain))


def _pad_vec(g, n):
    return jnp.pad(g, (0, n - g.shape[0]))


def kernel(x, attn_norm, w_in, a_q_norm, a_k_norm, b_q_norm, b_k_norm, b_lambda, b_subln,
           c_q_a_norm, c_kv_a_norm, c_w_q_up, c_w_kv_up, c_q_norm, c_k_norm, w_out,
           ffn_norm, w_gate, w_up, w_down):
    bsz, s, d = x.shape
    assert (bsz, s, d) == (1, SEQ, D_MODEL)
    bf = jnp.bfloat16
    a_tabs, b_tabs, c_tabs = _rope_tables(s)
    tp = 512
    h = x.reshape(s, d)
    w_in_t = jnp.swapaxes(w_in, 1, 2)

    for l in range(DEPTH):
        lambda_init = 0.8 - 0.6 * math.exp(-0.3 * l)
        wq_up = c_w_q_up[l].astype(bf).reshape(C_Q_LORA, C_HEADS, C_QK_DIM)
        wq_up = jnp.pad(wq_up, ((0, 0), (0, 0), (0, C_QK_PAD - C_QK_DIM))).reshape(C_Q_LORA, C_HEADS * C_QK_PAD)
        wkv_up = c_w_kv_up[l].astype(bf)

        xn = _rmsnorm(h, attn_norm[l])
        proj = _matmul([xn], [w_in_t], out_dtype=jnp.float32, tm=1024, tn=PROJ_BLOCK, layer=l,
                       w_transposed=True, n_cols=N_IN_MAIN, name="in_proj")
        c_kr = _tail_proj(xn, w_in_t, layer=l, row0=N_IN_MAIN, tm=1024, name="in_proj_kr")

        def proj_blocks(first, count):
            return [(proj, PROJ_BLOCK, first + b) for b in range(count)]

        a_scale = LOG2E / math.sqrt(A_HEAD_DIM)
        a_q = _prep(proj_blocks(0, 3), _head_pieces(A_HEADS), out_kind="cols", tm=tp,
                    gain=a_q_norm[l], norm="full", n_norm=A_HEAD_DIM, tabs=a_tabs, rope_blocks=(0,),
                    shifts=(96, 32), scale=a_scale, name="prep_a_q")
        a_k = _prep(proj_blocks(3, 1), _head_pieces(A_KV_HEADS), out_kind="rows", tm=tp,
                    gain=a_k_norm[l], norm="full", n_norm=A_HEAD_DIM, tabs=a_tabs, rope_blocks=(0,),
                    shifts=(96, 32), name="prep_a_k")
        a_v = _prep(proj_blocks(4, 1), _head_pieces(A_KV_HEADS), out_kind="chunks", tm=ATTN_KEY_CHUNK,
                    name="prep_a_v")
        ya = _attention(a_q, a_k, a_v, _score_bound(A_HEAD_DIM, a_q_norm[l], a_k_norm[l], a_scale),
                        tq=2048, name="attn_a")

        b_scale = LOG2E / math.sqrt(B_QK_DIM)
        b_q = _prep(proj_blocks(5, 2), _head_pieces(B_HEADS), out_kind="cols", tm=tp,
                    gain=jnp.tile(b_q_norm[l], 2), norm="half", n_norm=B_QK_DIM, tabs=b_tabs,
                    rope_blocks=(0,), shifts=(120, 8), scale=b_scale, name="prep_b_q")
        b_k = _prep(proj_blocks(7, 2), _head_pieces(B_HEADS), out_kind="rows", tm=tp,
                    gain=jnp.tile(b_k_norm[l], 2), norm="half", n_norm=B_QK_DIM, tabs=b_tabs,
                    rope_blocks=(0,), shifts=(120, 8), name="prep_b_k")
        b_v = _prep(proj_blocks(9, 2), _head_pieces(B_HEADS), out_kind="chunks", tm=ATTN_KEY_CHUNK,
                    name="prep_b_v")
        yb = _attention(b_q, b_k, b_v, _score_bound(B_QK_DIM, b_q_norm[l], b_k_norm[l], b_scale),
                        tq=2048, diff=True, lam_p=b_lambda[l], subln=b_subln[l],
                        lambda_init=lambda_init, name="attn_b")

        cq_lat = _prep(proj_blocks(11, 2), _head_pieces(1, blocks=C_Q_LORA // LANES), out_kind="rows", tm=tp,
                       gain=c_q_a_norm[l], norm="full", n_norm=C_Q_LORA, name="norm_c_q")[0]
        ckv_lat = _prep(proj_blocks(13, 1), _head_pieces(1, blocks=C_KV_LORA // LANES), out_kind="rows", tm=tp,
                        gain=c_kv_a_norm[l], norm="full", n_norm=C_KV_LORA, name="norm_c_kv")[0]
        c_q_raw = _matmul([cq_lat], [wq_up], out_dtype=jnp.float32, tm=1024, tn=1536, name="c_q_up")
        c_kv_raw = _matmul([ckv_lat], [wkv_up], out_dtype=jnp.float32, tm=1024, tn=1536, name="c_kv_up")
        c_scale = LOG2E / math.sqrt(C_QK_DIM)
        n_up = C_HEADS * C_QK_PAD // PROJ_BLOCK
        c_q = _prep([(c_q_raw, PROJ_BLOCK, b) for b in range(n_up)],
                    _head_pieces(C_HEADS, lanes_per_head=C_QK_PAD, blocks=2), out_kind="cols", tm=tp,
                    gain=_pad_vec(c_q_norm[l], C_QK_PAD), norm="full", n_norm=C_QK_DIM, tabs=c_tabs,
                    rope_blocks=(1,), shifts=(96, 32), scale=c_scale, name="prep_c_q")
        kv_srcs = [(c_kv_raw, PROJ_BLOCK, b) for b in range(n_up)]
        c_k_pieces = [[nope, (n_up, 0)] for (nope,) in _head_pieces(C_HEADS, lanes_per_head=2 * LANES)]
        c_k = _prep(kv_srcs + [(c_kr, LANES, 0)], c_k_pieces, out_kind="rows", tm=tp,
                    gain=_pad_vec(c_k_norm[l], C_QK_PAD), norm="full", n_norm=C_QK_DIM, tabs=c_tabs,
                    rope_blocks=(1,), shifts=(96, 32), name="prep_c_k")
        c_v = _prep(kv_srcs, _head_pieces(C_HEADS, lanes_per_head=2 * LANES, lane0=LANES), out_kind="chunks",
                    tm=ATTN_KEY_CHUNK, name="prep_c_v")
        yc = _attention(c_q, c_k, c_v, _score_bound(C_QK_DIM, c_q_norm[l], c_k_norm[l], c_scale),
                        tq=2048, name="attn_c")

        h = _matmul([ya, yb, yc], [w_out], res=h, out_dtype=jnp.float32, tm=1024, tn=512, layer=l,
                    name="out_proj")

        hn = _rmsnorm(h, ffn_norm[l])
        act = _matmul([hn], [w_gate, w_up], out_dtype=bf, tm=2048, tn=256, layer=l, name="ffn_gate_up")
        h = _matmul_kouter(act, w_down, h, layer=l, tm=1024, tn=256, tk=FFN_HIDDEN // 2, n_split=1,
                           name="ffn_down")

    return h.reshape(bsz, s, d)
```

```python
import functools
import math

import jax
import jax.numpy as jnp
from jax import lax
from jax.experimental import pallas as pl
from jax.experimental.pallas import tpu as pltpu

D_MODEL = 4096
SEQ = 8192
DEPTH = 2
GRID_W = 64
EPS = 1e-6

A_HEAD_DIM = 128
A_HEADS = 12
A_KV_HEADS = 4
A_ROPE_THETA = 10000.0

B_QK_DIM = 64
B_V_DIM = 128
B_HEADS = 8
B_ROPE_DIM = 16
B_ROPE_THETA = 500000.0

C_V_DIM = 128
C_HEADS = 12
C_Q_LORA = 1024
C_KV_LORA = 512
C_NOPE = 128
C_ROPE = 64
C_QK_DIM = C_NOPE + C_ROPE
C_QK_PAD = 256
C_ROPE_THETA = 10000.0

FFN_HIDDEN = 11008

LANES = 128
SUBLANES = 8
ATTN_ROW_BLOCK = 64
ATTN_KEY_CHUNK = 512
ATTN_CHUNKS_PER_TRIP = 8
ATTN_MIN_DENOM = 2.0 ** -60
BOUND_SLACK = 1.01
MXU_SUM_MAX_WIDTH = 256
ONES_ROWS = 16
VMEM_LIMIT_BYTES = 56 * 1024 * 1024

LOG2E = math.log2(math.e)

PROJ_BLOCK = 512
N_IN_MAIN = 14 * PROJ_BLOCK


def _head_pieces(n_heads, lanes_per_head=LANES, lane0=0, blocks=1):
    out = []
    for h in range(n_heads):
        head = []
        for b in range(blocks):
            col = lane0 + h * lanes_per_head + b * LANES
            head.append((col // PROJ_BLOCK, col % PROJ_BLOCK))
        out.append(head)
    return out


def _cparams(sem):
    return pltpu.CompilerParams(dimension_semantics=sem, vmem_limit_bytes=VMEM_LIMIT_BYTES)


def _rmsnorm_body(x_ref, g_ref, o_ref):
    x = x_ref[...]
    ms = jnp.mean(x * x, axis=-1, keepdims=True)
    o_ref[...] = (x * lax.rsqrt(ms + EPS) * g_ref[...]).astype(o_ref.dtype)


def _rmsnorm(x, g, tm=256):
    s, d = x.shape
    return pl.pallas_call(
        _rmsnorm_body,
        grid=(s // tm,),
        in_specs=[pl.BlockSpec((tm, d), lambda i: (i, 0)),
                  pl.BlockSpec((1, d), lambda i: (0, 0))],
        out_specs=pl.BlockSpec((tm, d), lambda i: (i, 0)),
        out_shape=jax.ShapeDtypeStruct((s, d), jnp.bfloat16),
        compiler_params=_cparams(("parallel",)),
        name="rmsnorm",
    )(x, g.reshape(1, d))


def _matmul_body(*refs, n_x, n_w, has_res, w_transposed):
    x_refs = refs[:n_x]
    w_refs = refs[n_x:n_x + n_w]
    pos = n_x + n_w
    res_ref = refs[pos] if has_res else None
    o_ref = refs[pos + int(has_res)]

    x = x_refs[0][...] if n_x == 1 else jnp.concatenate([r[...] for r in x_refs], axis=1)
    tiles = [w[...] if w.dtype == x.dtype else w[...].astype(x.dtype) for w in w_refs]
    contract = (((1,), (1,)), ((), ())) if w_transposed else (((1,), (0,)), ((), ()))
    prods = [lax.dot_general(x, w, contract, preferred_element_type=jnp.float32) for w in tiles]
    if n_w == 2:
        g, u = prods
        y = g * jax.nn.sigmoid(g) * u
    else:
        y = prods[0]
    if has_res:
        y = y + res_ref[...]
    o_ref[...] = y.astype(o_ref.dtype)


def _matmul(xs, ws, res=None, *, out_dtype, tm, tn, name, layer=None, w_transposed=False, n_cols=None):
    m = xs[0].shape[0]
    kdim = sum(x.shape[1] for x in xs)
    n = n_cols if n_cols is not None else ws[0].shape[-2 if w_transposed else -1]
    assert m % tm == 0 and n % tn == 0
    in_specs = [pl.BlockSpec((tm, x.shape[1]), lambda i, j: (i, 0)) for x in xs]
    w_block = (tn, kdim) if w_transposed else (kdim, tn)
    lead = () if layer is None else (None,)

    def w_index(i, j):
        pos = (j, 0) if w_transposed else (0, j)
        return pos if layer is None else (layer,) + pos

    in_specs += [pl.BlockSpec(lead + w_block, w_index) for _ in ws]
    args = [*xs, *ws]
    if res is not None:
        in_specs.append(pl.BlockSpec((tm, tn), lambda i, j: (i, j)))
        args.append(res)
    return pl.pallas_call(
        functools.partial(_matmul_body, n_x=len(xs), n_w=len(ws), has_res=res is not None,
                          w_transposed=w_transposed),
        grid=(m // tm, n // tn),
        in_specs=in_specs,
        out_specs=pl.BlockSpec((tm, tn), lambda i, j: (i, j)),
        out_shape=jax.ShapeDtypeStruct((m, n), out_dtype),
        compiler_params=_cparams(("parallel", "parallel")),
        name=name,
    )(*args)


def _tail_proj_body(x_ref, w_ref, o_ref):
    w = w_ref[...].astype(x_ref.dtype)
    w = jnp.concatenate([w, jnp.zeros_like(w)], axis=0)
    o_ref[...] = lax.dot_general(x_ref[...], w, (((1,), (1,)), ((), ())),
                                 preferred_element_type=jnp.float32)


def _tail_proj(x, w_t, *, layer, row0, tm, name):
    m, kdim = x.shape
    rows = w_t.shape[1] - row0
    assert rows * 2 == LANES and row0 % rows == 0
    return pl.pallas_call(
        _tail_proj_body,
        grid=(m // tm,),
        in_specs=[pl.BlockSpec((tm, kdim), lambda i: (i, 0)),
                  pl.BlockSpec((None, rows, kdim), lambda i: (layer, row0 // rows, 0))],
        out_specs=pl.BlockSpec((tm, LANES), lambda i: (i, 0)),
        out_shape=jax.ShapeDtypeStruct((m, LANES), jnp.float32),
        compiler_params=_cparams(("parallel",)),
        name=name,
    )(x, w_t)


def _matmul_kouter_body(x_ref, w_ref, res_ref, o_ref, acc_ref, *, nk):
    k = pl.program_id(2)
    j = pl.program_id(3)
    prod = jnp.dot(x_ref[...], w_ref[...].astype(x_ref.dtype), preferred_element_type=jnp.float32)

    @pl.when(k == 0)
    def _():
        acc_ref[j] = prod

    if nk > 2:
        @pl.when(jnp.logical_and(k > 0, k < nk - 1))
        def _():
            acc_ref[j] += prod

    @pl.when(k == nk - 1)
    def _():
        o_ref[...] = (acc_ref[j] + prod + res_ref[...]).astype(o_ref.dtype)


def _matmul_kouter(x, w, res, *, layer, tm, tn, tk, n_split, name):
    m, kdim = x.shape
    n = w.shape[-1]
    nk = kdim // tk
    nj = n // (tn * n_split)
    assert m % tm == 0 and n % (tn * n_split) == 0 and kdim % tk == 0 and nk >= 2

    def out_index(hf, i, k, j):
        return i, hf * nj + jnp.where(k == nk - 1, j, 0)

    return pl.pallas_call(
        functools.partial(_matmul_kouter_body, nk=nk),
        grid=(n_split, m // tm, nk, nj),
        in_specs=[pl.BlockSpec((tm, tk), lambda hf, i, k, j: (i, k)),
                  pl.BlockSpec((None, tk, tn), lambda hf, i, k, j: (layer, k, hf * nj + j)),
                  pl.BlockSpec((tm, tn), out_index)],
        out_specs=pl.BlockSpec((tm, tn), out_index),
        out_shape=jax.ShapeDtypeStruct((m, n), res.dtype),
        scratch_shapes=[pltpu.VMEM((nj, tm, tn), jnp.float32)],
        compiler_params=_cparams(("parallel", "parallel", "arbitrary", "arbitrary")),
        name=name,
    )(x, w, res)


def _prep_head(x, g_ref, tabs, *, norm, n_norm, rope_blocks, shifts, scale, transpose, ones_rows):
    width = x.shape[-1]
    if norm is not None:
        if width <= MXU_SUM_MAX_WIDTH:
            r = lax.broadcasted_iota(jnp.int32, (width, width), 0)
            c = lax.broadcasted_iota(jnp.int32, (width, width), 1)
            group = LANES // 2 if norm == "half" else width
            same = (r // group == c // group).astype(jnp.bfloat16)
            ss = jnp.dot((x * x).astype(jnp.bfloat16), same, preferred_element_type=jnp.float32)
        else:
            ss = jnp.sum(x * x, axis=-1, keepdims=True)
        x = x * lax.rsqrt(ss * (1.0 / n_norm) + EPS) * g_ref[...]

    if rope_blocks:
        c_ref, s1_ref, s2_ref = tabs
        blocks = []
        for b in range(width // LANES):
            xb = x[:, b * LANES:(b + 1) * LANES]
            if b in rope_blocks:
                sl = slice(b * LANES, (b + 1) * LANES)
                xb = (xb * c_ref[:, sl]
                      + pltpu.roll(xb, shifts[0], 1) * s1_ref[:, sl]
                      + pltpu.roll(xb, shifts[1], 1) * s2_ref[:, sl])
            blocks.append(xb)
        x = blocks[0] if len(blocks) == 1 else jnp.concatenate(blocks, axis=-1)

    if scale != 1.0:
        x = x * scale
    x = x.astype(jnp.bfloat16)
    if transpose:
        eye = (lax.broadcasted_iota(jnp.int32, (width, width), 0)
               == lax.broadcasted_iota(jnp.int32, (width, width), 1)).astype(jnp.bfloat16)
        x = lax.dot_general(eye, x, (((1,), (1,)), ((), ())),
                            preferred_element_type=jnp.float32).astype(jnp.bfloat16)
    if ones_rows:
        x = jnp.concatenate([x, jnp.ones((ones_rows, x.shape[1]), x.dtype)], axis=0)
    return x


def _prep_body(*refs, nx, pieces, has_gain, has_tabs, **head_kw):
    x_refs = refs[:nx]
    pos = nx
    g_ref = None
    if has_gain:
        g_ref = refs[pos]
        pos += 1
    tabs = None
    if has_tabs:
        tabs = refs[pos:pos + 3]
        pos += 3
    o_ref = refs[pos]
    for h, head_pieces in enumerate(pieces):
        cols = [x_refs[src][:, off:off + LANES] for src, off in head_pieces]
        x = cols[0] if len(cols) == 1 else jnp.concatenate(cols, axis=-1)
        y = _prep_head(x, g_ref, tabs, **head_kw)
        o_ref[h] = y.reshape(o_ref.shape[1:])


def _prep(srcs, pieces, *, out_kind, tm, gain=None, norm=None, n_norm=None, tabs=None,
          rope_blocks=(), shifts=(0, 0), scale=1.0, name="prep"):
    s = srcs[0][0].shape[0]
    n_heads = len(pieces)
    width = LANES * len(pieces[0])
    in_specs, args = [], []
    for arr, bw, idx in srcs:
        in_specs.append(pl.BlockSpec((tm, bw), functools.partial(lambda i, idx: (i, idx), idx=idx)))
        args.append(arr)
    if norm is not None:
        in_specs.append(pl.BlockSpec((1, width), lambda i: (0, 0)))
        args.append(gain.reshape(1, width))
    if rope_blocks:
        for t in tabs:
            in_specs.append(pl.BlockSpec((tm, width), lambda i: (i, 0)))
            args.append(t)
    if out_kind == "rows":
        out_shape = (n_heads, s, width)
        out_spec = pl.BlockSpec((n_heads, tm, width), lambda i: (0, i, 0))
    elif out_kind == "cols":
        out_shape = (n_heads, s // tm, width, tm)
        out_spec = pl.BlockSpec((n_heads, 1, width, tm), lambda i: (0, i, 0, 0))
    else:
        out_shape = (n_heads, s // tm, width + ONES_ROWS, tm)
        out_spec = pl.BlockSpec((n_heads, 1, width + ONES_ROWS, tm), lambda i: (0, i, 0, 0))
    return pl.pallas_call(
        functools.partial(_prep_body, nx=len(srcs), pieces=tuple(tuple(p) for p in pieces),
                          has_gain=norm is not None, has_tabs=bool(rope_blocks),
                          norm=norm, n_norm=n_norm, rope_blocks=tuple(rope_blocks), shifts=shifts,
                          scale=scale, transpose=out_kind != "rows",
                          ones_rows=ONES_ROWS if out_kind == "chunks" else 0),
        grid=(s // tm,),
        in_specs=in_specs,
        out_specs=out_spec,
        out_shape=jax.ShapeDtypeStruct(out_shape, jnp.bfloat16),
        compiler_params=_cparams(("parallel",)),
        name=name,
    )(*args)


def _attn_body(*refs, nchunk, tk, tq, diff, lambda_init):
    if diff:
        (qT_ref, k_ref, vT_ref, shift_ref, lam_ref, sub_ref, o_ref,
         acc_ref, p_ref, qq_ref, m_ref, s_ref, ps_ref) = refs
    else:
        qT_ref, k_ref, vT_ref, shift_ref, o_ref, acc_ref, p_ref, qq_ref, m_ref, s_ref, ps_ref = refs

    i = pl.program_id(1)
    nqb, _, nq = qq_ref.shape
    dv = o_ref.shape[1]
    tiles_per_block = tq // qT_ref.shape[3]
    chunk_bits = nchunk.bit_length() - 1
    assert nchunk == 1 << chunk_bits

    def scores(step):
        chunk = step & (nchunk - 1)
        block = jnp.minimum(i + (step >> chunk_bits), nqb - 1)
        start = pl.multiple_of(chunk * tk, tk)
        return jnp.dot(k_ref[0, pl.ds(start, tk), :], qq_ref[block],
                       preferred_element_type=jnp.float32)

    def probs(step):
        return jnp.exp2(scores(step) - shift_ref[...]).astype(jnp.bfloat16)

    @pl.when(i == 0)
    def _():
        for b in range(nqb):
            tiles = [qT_ref[0, b * tiles_per_block + c] for c in range(tiles_per_block)]
            q = tiles[0] if len(tiles) == 1 else jnp.concatenate(tiles, axis=1)
            if diff:
                first = lax.broadcasted_iota(jnp.int32, q.shape, 0) < B_QK_DIM
                zero = jnp.zeros_like(q)
                q = jnp.concatenate([jnp.where(first, q, zero), jnp.where(first, zero, q)], axis=1)
            qq_ref[b] = q
        p_ref[0] = probs(0)

    acc_ref[...] = jnp.zeros(acc_ref.shape, jnp.float32)

    def trip(t, carry):
        for u in range(ATTN_CHUNKS_PER_TRIP):
            j = t * ATTN_CHUNKS_PER_TRIP + u
            p_ref[(u + 1) % 2] = probs(j + 1)
            acc_ref[...] += jnp.dot(vT_ref[0, j], p_ref[u % 2], preferred_element_type=jnp.float32)
        return carry

    lax.fori_loop(0, nchunk // ATTN_CHUNKS_PER_TRIP, trip, 0)

    healthy = jnp.min(acc_ref[dv:dv + 1, :]) >= ATTN_MIN_DENOM

    @pl.when(jnp.logical_not(healthy))
    def _():
        rb = ATTN_ROW_BLOCK
        groups = rb // SUBLANES
        m_ref[...] = jnp.full(m_ref.shape, -jnp.inf, jnp.float32)
        acc_ref[...] = jnp.zeros(acc_ref.shape, jnp.float32)

        def block(r):
            return s_ref[r * rb:(r + 1) * rb, :].reshape(groups, SUBLANES, nq)

        def chunk(j, carry):
            s_ref[...] = scores(j)
            mx = jnp.max(block(0), axis=0)
            for r in range(1, tk // rb):
                mx = jnp.maximum(mx, jnp.max(block(r), axis=0))
            m_old = m_ref[...]
            m_new = jnp.maximum(m_old, jnp.max(mx, axis=0, keepdims=True))
            alpha = jnp.exp2(m_old - m_new)
            m_rows = jnp.broadcast_to(m_new, (SUBLANES, nq))[None]
            for r in range(tk // rb):
                p = jnp.exp2(block(r) - m_rows).reshape(rb, nq)
                ps_ref[r * rb:(r + 1) * rb, :] = p.astype(jnp.bfloat16)
            pv = jnp.dot(vT_ref[0, j], ps_ref[...], preferred_element_type=jnp.float32)
            acc_ref[...] = acc_ref[...] * alpha + pv
            m_ref[...] = m_new
            return carry

        lax.fori_loop(0, nchunk, chunk, 0)

    o = acc_ref[:dv, :] / acc_ref[dv:dv + 1, :]
    if diff:
        lp = lam_ref[...]
        lam = (jnp.exp(jnp.sum(lp[0:1] * lp[1:2], axis=-1, keepdims=True))
               - jnp.exp(jnp.sum(lp[2:3] * lp[3:4], axis=-1, keepdims=True))
               + lambda_init)
        o = o[:, :tq] - lam * o[:, tq:]
        ms = jnp.mean(o * o, axis=0, keepdims=True)
        o = o * lax.rsqrt(ms + EPS) * sub_ref[...] * (1.0 - lambda_init)
    o_ref[...] = o.T.astype(o_ref.dtype)


def _attention(qT, k, vT, score_bound, *, tq, diff=False, lam_p=None, subln=None, lambda_init=0.0, name):
    n_heads, n_qtiles, dq, q_tile = qT.shape
    s = n_qtiles * q_tile
    n_kv = k.shape[0]
    group = n_heads // n_kv
    _, nchunk, dv_ext, tk = vT.shape
    dv = dv_ext - ONES_ROWS
    nq = 2 * tq if diff else tq
    assert tq % q_tile == 0 and ATTN_CHUNKS_PER_TRIP % 2 == 0 and nchunk % ATTN_CHUNKS_PER_TRIP == 0
    in_specs = [pl.BlockSpec((1, n_qtiles, dq, q_tile), lambda h, i: (h, 0, 0, 0)),
                pl.BlockSpec((1, s, dq), lambda h, i: (h // group, 0, 0)),
                pl.BlockSpec((1, nchunk, dv_ext, tk), lambda h, i: (h // group, 0, 0, 0)),
                pl.BlockSpec((1, nq), lambda h, i: (0, 0))]
    args = [qT, k, vT, jnp.full((1, nq), score_bound, jnp.float32)]
    if diff:
        in_specs += [pl.BlockSpec(lam_p.shape, lambda h, i: (0, 0)),
                     pl.BlockSpec((dv, 1), lambda h, i: (0, 0))]
        args += [lam_p, subln.reshape(dv, 1)]
    return pl.pallas_call(
        functools.partial(_attn_body, nchunk=nchunk, tk=tk, tq=tq, diff=diff, lambda_init=lambda_init),
        grid=(n_heads, s // tq),
        in_specs=in_specs,
        out_specs=pl.BlockSpec((tq, dv), lambda h, i: (i, h)),
        out_shape=jax.ShapeDtypeStruct((s, n_heads * dv), jnp.bfloat16),
        scratch_shapes=[pltpu.VMEM((dv_ext, nq), jnp.float32),
                        pltpu.VMEM((2, tk, nq), jnp.bfloat16),
                        pltpu.VMEM((s // tq, dq, nq), jnp.bfloat16),
                        pltpu.VMEM((1, nq), jnp.float32),
                        pltpu.VMEM((tk, nq), jnp.float32),
                        pltpu.VMEM((tk, nq), jnp.bfloat16)],
        compiler_params=_cparams(("arbitrary", "arbitrary")),
        name=name,
    )(*args)


def _rope_cos_sin(pos, dim, theta):
    inv = theta ** (-jnp.arange(0, dim, 2, dtype=jnp.float32) / dim)
    ang = pos.astype(jnp.float32)[:, None] * inv[None, :]
    return jnp.cos(ang), jnp.sin(ang)


def _rotate_half_tables(cos, sin):
    zero = jnp.zeros_like(sin)
    return (jnp.concatenate([cos, cos], axis=-1),
            jnp.concatenate([-sin, zero], axis=-1),
            jnp.concatenate([zero, sin], axis=-1))


def _rope_tables(s):
    t = jnp.arange(s, dtype=jnp.int32)
    row = t // GRID_W
    col = t % GRID_W
    half = A_HEAD_DIM // 2
    a_row = _rotate_half_tables(*_rope_cos_sin(row, half, A_ROPE_THETA))
    a_col = _rotate_half_tables(*_rope_cos_sin(col, half, A_ROPE_THETA))
    a_tabs = tuple(jnp.concatenate([r, c], axis=-1) for r, c in zip(a_row, a_col))

    b_rot = _rotate_half_tables(*_rope_cos_sin(t, B_ROPE_DIM, B_ROPE_THETA))
    rest = B_QK_DIM - B_ROPE_DIM
    fill = (jnp.ones((s, rest), jnp.float32), jnp.zeros((s, rest), jnp.float32),
            jnp.zeros((s, rest), jnp.float32))
    b_tabs = tuple(jnp.tile(jnp.concatenate([r, f], axis=-1), (1, 2)) for r, f in zip(b_rot, fill))

    c_rot = _rotate_half_tables(*_rope_cos_sin(t, C_ROPE, C_ROPE_THETA))
    ones = jnp.ones((s, C_NOPE), jnp.float32)
    zeros = jnp.zeros((s, C_NOPE), jnp.float32)
    pad = jnp.zeros((s, C_QK_PAD - C_QK_DIM), jnp.float32)
    c_tabs = (jnp.concatenate([ones, c_rot[0], pad], axis=-1),
              jnp.concatenate([zeros, c_rot[1], pad], axis=-1),
              jnp.concatenate([zeros, c_rot[2], pad], axis=-1))
    return a_tabs, b_tabs, c_tabs


def _score_bound(dim, q_gain, k_gain, scale):
    return BOUND_SLACK * dim * scale * jnp.max(jnp.abs(q_gain)) * jnp.max(jnp.abs(k_gain))


def _pad_vec(g, n):
    return jnp.pad(g, (0, n - g.shape[0]))


def kernel(x, attn_norm, w_in, a_q_norm, a_k_norm, b_q_norm, b_k_norm, b_lambda, b_subln,
           c_q_a_norm, c_kv_a_norm, c_w_q_up, c_w_kv_up, c_q_norm, c_k_norm, w_out,
           ffn_norm, w_gate, w_up, w_down):
    bsz, s, d = x.shape
    assert (bsz, s, d) == (1, SEQ, D_MODEL)
    bf = jnp.bfloat16
    a_tabs, b_tabs, c_tabs = _rope_tables(s)
    tp = 512
    h = x.reshape(s, d)
    w_in_t = jnp.swapaxes(w_in, 1, 2)

    for l in range(DEPTH):
        lambda_init = 0.8 - 0.6 * math.exp(-0.3 * l)
        wq_up = c_w_q_up[l].astype(bf).reshape(C_Q_LORA, C_HEADS, C_QK_DIM)
        wq_up = jnp.pad(wq_up, ((0, 0), (0, 0), (0, C_QK_PAD - C_QK_DIM))).reshape(C_Q_LORA, C_HEADS * C_QK_PAD)
        wkv_up = c_w_kv_up[l].astype(bf)

        xn = _rmsnorm(h, attn_norm[l])
        proj = _matmul([xn], [w_in_t], out_dtype=jnp.float32, tm=1024, tn=PROJ_BLOCK, layer=l,
                       w_transposed=True, n_cols=N_IN_MAIN, name="in_proj")
        c_kr = _tail_proj(xn, w_in_t, layer=l, row0=N_IN_MAIN, tm=1024, name="in_proj_kr")

        def proj_blocks(first, count):
            return [(proj, PROJ_BLOCK, first + b) for b in range(count)]

        a_scale = LOG2E / math.sqrt(A_HEAD_DIM)
        a_q = _prep(proj_blocks(0, 3), _head_pieces(A_HEADS), out_kind="cols", tm=tp,
                    gain=a_q_norm[l], norm="full", n_norm=A_HEAD_DIM, tabs=a_tabs, rope_blocks=(0,),
                    shifts=(96, 32), scale=a_scale, name="prep_a_q")
        a_k = _prep(proj_blocks(3, 1), _head_pieces(A_KV_HEADS), out_kind="rows", tm=tp,
                    gain=a_k_norm[l], norm="full", n_norm=A_HEAD_DIM, tabs=a_tabs, rope_blocks=(0,),
                    shifts=(96, 32), name="prep_a_k")
        a_v = _prep(proj_blocks(4, 1), _head_pieces(A_KV_HEADS), out_kind="chunks", tm=ATTN_KEY_CHUNK,
                    name="prep_a_v")
        ya = _attention(a_q, a_k, a_v, _score_bound(A_HEAD_DIM, a_q_norm[l], a_k_norm[l], a_scale),
                        tq=4096, name="attn_a")

        b_scale = LOG2E / math.sqrt(B_QK_DIM)
        b_q = _prep(proj_blocks(5, 2), _head_pieces(B_HEADS), out_kind="cols", tm=tp,
                    gain=jnp.tile(b_q_norm[l], 2), norm="half", n_norm=B_QK_DIM, tabs=b_tabs,
                    rope_blocks=(0,), shifts=(120, 8), scale=b_scale, name="prep_b_q")
        b_k = _prep(proj_blocks(7, 2), _head_pieces(B_HEADS), out_kind="rows", tm=tp,
                    gain=jnp.tile(b_k_norm[l], 2), norm="half", n_norm=B_QK_DIM, tabs=b_tabs,
                    rope_blocks=(0,), shifts=(120, 8), name="prep_b_k")
        b_v = _prep(proj_blocks(9, 2), _head_pieces(B_HEADS), out_kind="chunks", tm=ATTN_KEY_CHUNK,
                    name="prep_b_v")
        yb = _attention(b_q, b_k, b_v, _score_bound(B_QK_DIM, b_q_norm[l], b_k_norm[l], b_scale),
                        tq=2048, diff=True, lam_p=b_lambda[l], subln=b_subln[l],
                        lambda_init=lambda_init, name="attn_b")

        cq_lat = _prep(proj_blocks(11, 2), _head_pieces(1, blocks=C_Q_LORA // LANES), out_kind="rows", tm=tp,
                       gain=c_q_a_norm[l], norm="full", n_norm=C_Q_LORA, name="norm_c_q")[0]
        ckv_lat = _prep(proj_blocks(13, 1), _head_pieces(1, blocks=C_KV_LORA // LANES), out_kind="rows", tm=tp,
                        gain=c_kv_a_norm[l], norm="full", n_norm=C_KV_LORA, name="norm_c_kv")[0]
        c_q_raw = _matmul([cq_lat], [wq_up], out_dtype=jnp.float32, tm=1024, tn=1536, name="c_q_up")
        c_kv_raw = _matmul([ckv_lat], [wkv_up], out_dtype=jnp.float32, tm=1024, tn=1536, name="c_kv_up")
        c_scale = LOG2E / math.sqrt(C_QK_DIM)
        n_up = C_HEADS * C_QK_PAD // PROJ_BLOCK
        c_q = _prep([(c_q_raw, PROJ_BLOCK, b) for b in range(n_up)],
                    _head_pieces(C_HEADS, lanes_per_head=C_QK_PAD, blocks=2), out_kind="cols", tm=tp,
                    gain=_pad_vec(c_q_norm[l], C_QK_PAD), norm="full", n_norm=C_QK_DIM, tabs=c_tabs,
                    rope_blocks=(1,), shifts=(96, 32), scale=c_scale, name="prep_c_q")
        kv_srcs = [(c_kv_raw, PROJ_BLOCK, b) for b in range(n_up)]
        c_k_pieces = [[nope, (n_up, 0)] for (nope,) in _head_pieces(C_HEADS, lanes_per_head=2 * LANES)]
        c_k = _prep(kv_srcs + [(c_kr, LANES, 0)], c_k_pieces, out_kind="rows", tm=tp,
                    gain=_pad_vec(c_k_norm[l], C_QK_PAD), norm="full", n_norm=C_QK_DIM, tabs=c_tabs,
                    rope_blocks=(1,), shifts=(96, 32), name="prep_c_k")
        c_v = _prep(kv_srcs, _head_pieces(C_HEADS, lanes_per_head=2 * LANES, lane0=LANES), out_kind="chunks",
                    tm=ATTN_KEY_CHUNK, name="prep_c_v")
        yc = _attention(c_q, c_k, c_v, _score_bound(C_QK_DIM, c_q_norm[l], c_k_norm[l], c_scale),
                        tq=4096, name="attn_c")

        h = _matmul([ya, yb, yc], [w_out], res=h, out_dtype=jnp.float32, tm=1024, tn=512, layer=l,
                    name="out_proj")

        hn = _rmsnorm(h, ffn_norm[l])
        act = _matmul([hn], [w_gate, w_up], out_dtype=bf, tm=2048, tn=256, layer=l, name="ffn_gate_up")
        h = _matmul_kouter(act, w_down, h, layer=l, tm=1024, tn=256, tk=FFN_HIDDEN // 2, n_split=1,
                           name="ffn_down")

    return h.reshape(bsz, s, d)
```

```python
import functools
import math

import jax
import jax.numpy as jnp
from jax import lax
from jax.experimental import pallas as pl
from jax.experimental.pallas import tpu as pltpu

D_MODEL = 4096
SEQ = 8192
DEPTH = 2
GRID_W = 64
EPS = 1e-6

A_HEAD_DIM = 128
A_HEADS = 12
A_KV_HEADS = 4
A_ROPE_THETA = 10000.0

B_QK_DIM = 64
B_V_DIM = 128
B_HEADS = 8
B_ROPE_DIM = 16
B_ROPE_THETA = 500000.0

C_V_DIM = 128
C_HEADS = 12
C_Q_LORA = 1024
C_KV_LORA = 512
C_NOPE = 128
C_ROPE = 64
C_QK_DIM = C_NOPE + C_ROPE
C_QK_PAD = 256
C_ROPE_THETA = 10000.0

FFN_HIDDEN = 11008

LANES = 128
SUBLANES = 8
ATTN_ROW_BLOCK = 64
ATTN_KEY_CHUNK = 512
ATTN_CHUNKS_PER_TRIP = 8
ATTN_MIN_DENOM = 2.0 ** -60
BOUND_SLACK = 1.01
MXU_SUM_MAX_WIDTH = 256
ONES_ROWS = 16
VMEM_LIMIT_BYTES = 56 * 1024 * 1024

ROW_TILE = 1024
GATE_UP_ROW_TILE = 2048
FFN_COL_TILE = 256
OUT_COL_TILE = 512
C_UP_COL_TILE = 1536
PREP_ROW_TILE = 512
ATTN_Q_BLOCK_A = 4096
ATTN_Q_BLOCK_B = 2048
ATTN_Q_BLOCK_C = 4096

LOG2E = math.log2(math.e)

PROJ_BLOCK = 512
N_IN_MAIN = 14 * PROJ_BLOCK


def _head_pieces(n_heads, lanes_per_head=LANES, lane0=0, blocks=1):
    out = []
    for h in range(n_heads):
        head = []
        for b in range(blocks):
            col = lane0 + h * lanes_per_head + b * LANES
            head.append((col // PROJ_BLOCK, col % PROJ_BLOCK))
        out.append(head)
    return out


def _cparams(sem):
    return pltpu.CompilerParams(dimension_semantics=sem, vmem_limit_bytes=VMEM_LIMIT_BYTES)


def _rmsnorm_body(x_ref, g_ref, o_ref):
    x = x_ref[...]
    ms = jnp.mean(x * x, axis=-1, keepdims=True)
    o_ref[...] = (x * lax.rsqrt(ms + EPS) * g_ref[...]).astype(o_ref.dtype)


def _rmsnorm(x, g, tm=512):
    s, d = x.shape
    return pl.pallas_call(
        _rmsnorm_body,
        grid=(s // tm,),
        in_specs=[pl.BlockSpec((tm, d), lambda i: (i, 0)),
                  pl.BlockSpec((1, d), lambda i: (0, 0))],
        out_specs=pl.BlockSpec((tm, d), lambda i: (i, 0)),
        out_shape=jax.ShapeDtypeStruct((s, d), jnp.bfloat16),
        compiler_params=_cparams(("parallel",)),
        name="rmsnorm",
    )(x, g.reshape(1, d))


def _matmul_body(*refs, n_x, n_w, has_res, w_transposed):
    x_refs = refs[:n_x]
    w_refs = refs[n_x:n_x + n_w]
    pos = n_x + n_w
    res_ref = refs[pos] if has_res else None
    o_ref = refs[pos + int(has_res)]

    xs = [r[...] for r in x_refs]
    tiles = [w[...] if w.dtype == xs[0].dtype else w[...].astype(xs[0].dtype) for w in w_refs]
    contract = (((1,), (1,)), ((), ())) if w_transposed else (((1,), (0,)), ((), ()))
    if n_x == 1:
        prods = [lax.dot_general(xs[0], w, contract, preferred_element_type=jnp.float32) for w in tiles]
    else:
        assert not w_transposed
        prods = []
        for w in tiles:
            row, total = 0, None
            for x in xs:
                part = jnp.dot(x, w[row:row + x.shape[1], :], preferred_element_type=jnp.float32)
                total = part if total is None else total + part
                row += x.shape[1]
            prods.append(total)
    if n_w == 2:
        g, u = prods
        y = g * jax.nn.sigmoid(g) * u
    else:
        y = prods[0]
    if has_res:
        y = y + res_ref[...]
    o_ref[...] = y.astype(o_ref.dtype)


def _matmul(xs, ws, res=None, *, out_dtype, tm, tn, name, layer=None, w_transposed=False, n_cols=None):
    m = xs[0].shape[0]
    kdim = sum(x.shape[1] for x in xs)
    n = n_cols if n_cols is not None else ws[0].shape[-2 if w_transposed else -1]
    assert m % tm == 0 and n % tn == 0
    in_specs = [pl.BlockSpec((tm, x.shape[1]), lambda i, j: (i, 0)) for x in xs]
    w_block = (tn, kdim) if w_transposed else (kdim, tn)
    lead = () if layer is None else (None,)

    def w_index(i, j):
        pos = (j, 0) if w_transposed else (0, j)
        return pos if layer is None else (layer,) + pos

    in_specs += [pl.BlockSpec(lead + w_block, w_index) for _ in ws]
    args = [*xs, *ws]
    if res is not None:
        in_specs.append(pl.BlockSpec((tm, tn), lambda i, j: (i, j)))
        args.append(res)
    return pl.pallas_call(
        functools.partial(_matmul_body, n_x=len(xs), n_w=len(ws), has_res=res is not None,
                          w_transposed=w_transposed),
        grid=(m // tm, n // tn),
        in_specs=in_specs,
        out_specs=pl.BlockSpec((tm, tn), lambda i, j: (i, j)),
        out_shape=jax.ShapeDtypeStruct((m, n), out_dtype),
        compiler_params=_cparams(("parallel", "parallel")),
        name=name,
    )(*args)


def _tail_proj_body(x_ref, w_ref, o_ref):
    w = w_ref[...].astype(x_ref.dtype)
    w = jnp.concatenate([w, jnp.zeros_like(w)], axis=0)
    o_ref[...] = lax.dot_general(x_ref[...], w, (((1,), (1,)), ((), ())),
                                 preferred_element_type=jnp.float32)


def _tail_proj(x, w_t, *, layer, row0, tm, name):
    m, kdim = x.shape
    rows = w_t.shape[1] - row0
    assert rows * 2 == LANES and row0 % rows == 0
    return pl.pallas_call(
        _tail_proj_body,
        grid=(m // tm,),
        in_specs=[pl.BlockSpec((tm, kdim), lambda i: (i, 0)),
                  pl.BlockSpec((None, rows, kdim), lambda i: (layer, row0 // rows, 0))],
        out_specs=pl.BlockSpec((tm, LANES), lambda i: (i, 0)),
        out_shape=jax.ShapeDtypeStruct((m, LANES), jnp.float32),
        compiler_params=_cparams(("parallel",)),
        name=name,
    )(x, w_t)


def _matmul_kouter_body(x_ref, w_ref, res_ref, o_ref, acc_ref, *, nk):
    k = pl.program_id(2)
    j = pl.program_id(3)
    prod = jnp.dot(x_ref[...], w_ref[...].astype(x_ref.dtype), preferred_element_type=jnp.float32)

    @pl.when(k == 0)
    def _():
        acc_ref[j] = prod

    if nk > 2:
        @pl.when(jnp.logical_and(k > 0, k < nk - 1))
        def _():
            acc_ref[j] += prod

    @pl.when(k == nk - 1)
    def _():
        o_ref[...] = (acc_ref[j] + prod + res_ref[...]).astype(o_ref.dtype)


def _matmul_kouter(x, w, res, *, layer, tm, tn, tk, n_split, name):
    m, kdim = x.shape
    n = w.shape[-1]
    nk = kdim // tk
    nj = n // (tn * n_split)
    assert m % tm == 0 and n % (tn * n_split) == 0 and kdim % tk == 0 and nk >= 2

    def out_index(hf, i, k, j):
        return i, hf * nj + jnp.where(k == nk - 1, j, 0)

    return pl.pallas_call(
        functools.partial(_matmul_kouter_body, nk=nk),
        grid=(n_split, m // tm, nk, nj),
        in_specs=[pl.BlockSpec((tm, tk), lambda hf, i, k, j: (i, k)),
                  pl.BlockSpec((None, tk, tn), lambda hf, i, k, j: (layer, k, hf * nj + j)),
                  pl.BlockSpec((tm, tn), out_index)],
        out_specs=pl.BlockSpec((tm, tn), out_index),
        out_shape=jax.ShapeDtypeStruct((m, n), res.dtype),
        scratch_shapes=[pltpu.VMEM((nj, tm, tn), jnp.float32)],
        compiler_params=_cparams(("parallel", "parallel", "arbitrary", "arbitrary")),
        name=name,
    )(x, w, res)


def _prep_head(x, g_ref, tabs, *, norm, n_norm, rope_blocks, shifts, scale, transpose, ones_rows):
    width = x.shape[-1]
    if norm is not None:
        if width <= MXU_SUM_MAX_WIDTH:
            r = lax.broadcasted_iota(jnp.int32, (width, width), 0)
            c = lax.broadcasted_iota(jnp.int32, (width, width), 1)
            group = LANES // 2 if norm == "half" else width
            same = (r // group == c // group).astype(jnp.bfloat16)
            ss = jnp.dot((x * x).astype(jnp.bfloat16), same, preferred_element_type=jnp.float32)
        else:
            ss = jnp.sum(x * x, axis=-1, keepdims=True)
        x = x * lax.rsqrt(ss * (1.0 / n_norm) + EPS) * g_ref[...]

    if rope_blocks:
        c_ref, s1_ref, s2_ref = tabs
        blocks = []
        for b in range(width // LANES):
            xb = x[:, b * LANES:(b + 1) * LANES]
            if b in rope_blocks:
                sl = slice(b * LANES, (b + 1) * LANES)
                xb = (xb * c_ref[:, sl]
                      + pltpu.roll(xb, shifts[0], 1) * s1_ref[:, sl]
                      + pltpu.roll(xb, shifts[1], 1) * s2_ref[:, sl])
            blocks.append(xb)
        x = blocks[0] if len(blocks) == 1 else jnp.concatenate(blocks, axis=-1)

    if scale != 1.0:
        x = x * scale
    x = x.astype(jnp.bfloat16)
    if transpose:
        eye = (lax.broadcasted_iota(jnp.int32, (width, width), 0)
               == lax.broadcasted_iota(jnp.int32, (width, width), 1)).astype(jnp.bfloat16)
        x = lax.dot_general(eye, x, (((1,), (1,)), ((), ())),
                            preferred_element_type=jnp.float32).astype(jnp.bfloat16)
    if ones_rows:
        x = jnp.concatenate([x, jnp.ones((ones_rows, x.shape[1]), x.dtype)], axis=0)
    return x


def _prep_body(*refs, nx, pieces, has_gain, has_tabs, **head_kw):
    x_refs = refs[:nx]
    pos = nx
    g_ref = None
    if has_gain:
        g_ref = refs[pos]
        pos += 1
    tabs = None
    if has_tabs:
        tabs = refs[pos:pos + 3]
        pos += 3
    o_ref = refs[pos]
    for h, head_pieces in enumerate(pieces):
        cols = [x_refs[src][:, off:off + LANES] for src, off in head_pieces]
        x = cols[0] if len(cols) == 1 else jnp.concatenate(cols, axis=-1)
        y = _prep_head(x, g_ref, tabs, **head_kw)
        o_ref[h] = y.reshape(o_ref.shape[1:])


def _prep(srcs, pieces, *, out_kind, tm, gain=None, norm=None, n_norm=None, tabs=None,
          rope_blocks=(), shifts=(0, 0), scale=1.0, name="prep"):
    s = srcs[0][0].shape[0]
    n_heads = len(pieces)
    width = LANES * len(pieces[0])
    in_specs, args = [], []
    for arr, bw, idx in srcs:
        in_specs.append(pl.BlockSpec((tm, bw), functools.partial(lambda i, idx: (i, idx), idx=idx)))
        args.append(arr)
    if norm is not None:
        in_specs.append(pl.BlockSpec((1, width), lambda i: (0, 0)))
        args.append(gain.reshape(1, width))
    if rope_blocks:
        for t in tabs:
            in_specs.append(pl.BlockSpec((tm, width), lambda i: (i, 0)))
            args.append(t)
    if out_kind == "rows":
        out_shape = (n_heads, s, width)
        out_spec = pl.BlockSpec((n_heads, tm, width), lambda i: (0, i, 0))
    elif out_kind == "cols":
        out_shape = (n_heads, s // tm, width, tm)
        out_spec = pl.BlockSpec((n_heads, 1, width, tm), lambda i: (0, i, 0, 0))
    else:
        out_shape = (n_heads, s // tm, width + ONES_ROWS, tm)
        out_spec = pl.BlockSpec((n_heads, 1, width + ONES_ROWS, tm), lambda i: (0, i, 0, 0))
    return pl.pallas_call(
        functools.partial(_prep_body, nx=len(srcs), pieces=tuple(tuple(p) for p in pieces),
                          has_gain=norm is not None, has_tabs=bool(rope_blocks),
                          norm=norm, n_norm=n_norm, rope_blocks=tuple(rope_blocks), shifts=shifts,
                          scale=scale, transpose=out_kind != "rows",
                          ones_rows=ONES_ROWS if out_kind == "chunks" else 0),
        grid=(s // tm,),
        in_specs=in_specs,
        out_specs=out_spec,
        out_shape=jax.ShapeDtypeStruct(out_shape, jnp.bfloat16),
        compiler_params=_cparams(("parallel",)),
        name=name,
    )(*args)


def _attn_body(*refs, nchunk, tk, tq, diff, lambda_init):
    if diff:
        (qT_ref, k_ref, vT_ref, shift_ref, lam_ref, sub_ref, o_ref,
         acc_ref, p_ref, qq_ref, m_ref, s_ref, ps_ref) = refs
    else:
        qT_ref, k_ref, vT_ref, shift_ref, o_ref, acc_ref, p_ref, qq_ref, m_ref, s_ref, ps_ref = refs

    i = pl.program_id(1)
    nqb, _, nq = qq_ref.shape
    dv = o_ref.shape[1]
    tiles_per_block = tq // qT_ref.shape[3]
    chunk_bits = nchunk.bit_length() - 1
    assert nchunk == 1 << chunk_bits

    def scores(step):
        chunk = step & (nchunk - 1)
        block = jnp.minimum(i + (step >> chunk_bits), nqb - 1)
        start = pl.multiple_of(chunk * tk, tk)
        return jnp.dot(k_ref[0, pl.ds(start, tk), :], qq_ref[block],
                       preferred_element_type=jnp.float32)

    def probs(step):
        return jnp.exp2(scores(step) - shift_ref[...]).astype(jnp.bfloat16)

    @pl.when(i == 0)
    def _():
        for b in range(nqb):
            tiles = [qT_ref[0, b * tiles_per_block + c] for c in range(tiles_per_block)]
            q = tiles[0] if len(tiles) == 1 else jnp.concatenate(tiles, axis=1)
            if diff:
                first = lax.broadcasted_iota(jnp.int32, q.shape, 0) < B_QK_DIM
                zero = jnp.zeros_like(q)
                q = jnp.concatenate([jnp.where(first, q, zero), jnp.where(first, zero, q)], axis=1)
            qq_ref[b] = q
        p_ref[0] = probs(0)

    acc_ref[...] = jnp.zeros(acc_ref.shape, jnp.float32)

    def trip(t, carry):
        for u in range(ATTN_CHUNKS_PER_TRIP):
            j = t * ATTN_CHUNKS_PER_TRIP + u
            p_ref[(u + 1) % 2] = probs(j + 1)
            acc_ref[...] += jnp.dot(vT_ref[0, j], p_ref[u % 2], preferred_element_type=jnp.float32)
        return carry

    lax.fori_loop(0, nchunk // ATTN_CHUNKS_PER_TRIP, trip, 0)

    healthy = jnp.min(acc_ref[dv:dv + 1, :]) >= ATTN_MIN_DENOM

    @pl.when(jnp.logical_not(healthy))
    def _():
        rb = ATTN_ROW_BLOCK
        groups = rb // SUBLANES
        m_ref[...] = jnp.full(m_ref.shape, -jnp.inf, jnp.float32)
        acc_ref[...] = jnp.zeros(acc_ref.shape, jnp.float32)

        def block(r):
            return s_ref[r * rb:(r + 1) * rb, :].reshape(groups, SUBLANES, nq)

        def chunk(j, carry):
            s_ref[...] = scores(j)
            mx = jnp.max(block(0), axis=0)
            for r in range(1, tk // rb):
                mx = jnp.maximum(mx, jnp.max(block(r), axis=0))
            m_old = m_ref[...]
            m_new = jnp.maximum(m_old, jnp.max(mx, axis=0, keepdims=True))
            alpha = jnp.exp2(m_old - m_new)
            m_rows = jnp.broadcast_to(m_new, (SUBLANES, nq))[None]
            for r in range(tk // rb):
                p = jnp.exp2(block(r) - m_rows).reshape(rb, nq)
                ps_ref[r * rb:(r + 1) * rb, :] = p.astype(jnp.bfloat16)
            pv = jnp.dot(vT_ref[0, j], ps_ref[...], preferred_element_type=jnp.float32)
            acc_ref[...] = acc_ref[...] * alpha + pv
            m_ref[...] = m_new
            return carry

        lax.fori_loop(0, nchunk, chunk, 0)

    o = acc_ref[:dv, :] / acc_ref[dv:dv + 1, :]
    if diff:
        lp = lam_ref[...]
        lam = (jnp.exp(jnp.sum(lp[0:1] * lp[1:2], axis=-1, keepdims=True))
               - jnp.exp(jnp.sum(lp[2:3] * lp[3:4], axis=-1, keepdims=True))
               + lambda_init)
        o = o[:, :tq] - lam * o[:, tq:]
        ms = jnp.mean(o * o, axis=0, keepdims=True)
        o = o * lax.rsqrt(ms + EPS) * sub_ref[...] * (1.0 - lambda_init)
    o_ref[...] = o.T.astype(o_ref.dtype)


def _attention(qT, k, vT, score_bound, *, tq, diff=False, lam_p=None, subln=None, lambda_init=0.0, name):
    n_heads, n_qtiles, dq, q_tile = qT.shape
    s = n_qtiles * q_tile
    n_kv = k.shape[0]
    group = n_heads // n_kv
    _, nchunk, dv_ext, tk = vT.shape
    dv = dv_ext - ONES_ROWS
    nq = 2 * tq if diff else tq
    assert tq % q_tile == 0 and ATTN_CHUNKS_PER_TRIP % 2 == 0 and nchunk % ATTN_CHUNKS_PER_TRIP == 0
    in_specs = [pl.BlockSpec((1, n_qtiles, dq, q_tile), lambda h, i: (h, 0, 0, 0)),
                pl.BlockSpec((1, s, dq), lambda h, i: (h // group, 0, 0)),
                pl.BlockSpec((1, nchunk, dv_ext, tk), lambda h, i: (h // group, 0, 0, 0)),
                pl.BlockSpec((1, nq), lambda h, i: (0, 0))]
    args = [qT, k, vT, jnp.full((1, nq), score_bound, jnp.float32)]
    if diff:
        in_specs += [pl.BlockSpec(lam_p.shape, lambda h, i: (0, 0)),
                     pl.BlockSpec((dv, 1), lambda h, i: (0, 0))]
        args += [lam_p, subln.reshape(dv, 1)]
    return pl.pallas_call(
        functools.partial(_attn_body, nchunk=nchunk, tk=tk, tq=tq, diff=diff, lambda_init=lambda_init),
        grid=(n_heads, s // tq),
        in_specs=in_specs,
        out_specs=pl.BlockSpec((tq, dv), lambda h, i: (i, h)),
        out_shape=jax.ShapeDtypeStruct((s, n_heads * dv), jnp.bfloat16),
        scratch_shapes=[pltpu.VMEM((dv_ext, nq), jnp.float32),
                        pltpu.VMEM((2, tk, nq), jnp.bfloat16),
                        pltpu.VMEM((s // tq, dq, nq), jnp.bfloat16),
                        pltpu.VMEM((1, nq), jnp.float32),
                        pltpu.VMEM((tk, nq), jnp.float32),
                        pltpu.VMEM((tk, nq), jnp.bfloat16)],
        compiler_params=_cparams(("arbitrary", "arbitrary")),
        name=name,
    )(*args)


def _rope_cos_sin(pos, dim, theta):
    inv = theta ** (-jnp.arange(0, dim, 2, dtype=jnp.float32) / dim)
    ang = pos.astype(jnp.float32)[:, None] * inv[None, :]
    return jnp.cos(ang), jnp.sin(ang)


def _rotate_half_tables(cos, sin):
    zero = jnp.zeros_like(sin)
    return (jnp.concatenate([cos, cos], axis=-1),
            jnp.concatenate([-sin, zero], axis=-1),
            jnp.concatenate([zero, sin], axis=-1))


def _rope_tables(s):
    t = jnp.arange(s, dtype=jnp.int32)
    row = t // GRID_W
    col = t % GRID_W
    half = A_HEAD_DIM // 2
    a_row = _rotate_half_tables(*_rope_cos_sin(row, half, A_ROPE_THETA))
    a_col = _rotate_half_tables(*_rope_cos_sin(col, half, A_ROPE_THETA))
    a_tabs = tuple(jnp.concatenate([r, c], axis=-1) for r, c in zip(a_row, a_col))

    b_rot = _rotate_half_tables(*_rope_cos_sin(t, B_ROPE_DIM, B_ROPE_THETA))
    rest = B_QK_DIM - B_ROPE_DIM
    fill = (jnp.ones((s, rest), jnp.float32), jnp.zeros((s, rest), jnp.float32),
            jnp.zeros((s, rest), jnp.float32))
    b_tabs = tuple(jnp.tile(jnp.concatenate([r, f], axis=-1), (1, 2)) for r, f in zip(b_rot, fill))

    c_rot = _rotate_half_tables(*_rope_cos_sin(t, C_ROPE, C_ROPE_THETA))
    ones = jnp.ones((s, C_NOPE), jnp.float32)
    zeros = jnp.zeros((s, C_NOPE), jnp.float32)
    pad = jnp.zeros((s, C_QK_PAD - C_QK_DIM), jnp.float32)
    c_tabs = (jnp.concatenate([ones, c_rot[0], pad], axis=-1),
              jnp.concatenate([zeros, c_rot[1], pad], axis=-1),
              jnp.concatenate([zeros, c_rot[2], pad], axis=-1))
    return a_tabs, b_tabs, c_tabs


def _score_bound(dim, q_gain, k_gain, scale):
    return BOUND_SLACK * dim * scale * jnp.max(jnp.abs(q_gain)) * jnp.max(jnp.abs(k_gain))


def _pad_vec(g, n):
    return jnp.pad(g, (0, n - g.shape[0]))


def kernel(x, attn_norm, w_in, a_q_norm, a_k_norm, b_q_norm, b_k_norm, b_lambda, b_subln,
           c_q_a_norm, c_kv_a_norm, c_w_q_up, c_w_kv_up, c_q_norm, c_k_norm, w_out,
           ffn_norm, w_gate, w_up, w_down):
    bsz, s, d = x.shape
    assert (bsz, s, d) == (1, SEQ, D_MODEL)
    bf = jnp.bfloat16
    a_tabs, b_tabs, c_tabs = _rope_tables(s)
    tp = PREP_ROW_TILE
    h = x.reshape(s, d)
    w_in_t = jnp.swapaxes(w_in, 1, 2)

    for l in range(DEPTH):
        lambda_init = 0.8 - 0.6 * math.exp(-0.3 * l)
        wq_up = c_w_q_up[l].astype(bf).reshape(C_Q_LORA, C_HEADS, C_QK_DIM)
        wq_up = jnp.pad(wq_up, ((0, 0), (0, 0), (0, C_QK_PAD - C_QK_DIM))).reshape(C_Q_LORA, C_HEADS * C_QK_PAD)
        wkv_up = c_w_kv_up[l].astype(bf)

        xn = _rmsnorm(h, attn_norm[l])
        proj = _matmul([xn], [w_in_t], out_dtype=jnp.float32, tm=ROW_TILE, tn=PROJ_BLOCK, layer=l,
                       w_transposed=True, n_cols=N_IN_MAIN, name="in_proj")
        c_kr = _tail_proj(xn, w_in_t, layer=l, row0=N_IN_MAIN, tm=ROW_TILE, name="in_proj_kr")

        def proj_blocks(first, count):
            return [(proj, PROJ_BLOCK, first + b) for b in range(count)]

        a_scale = LOG2E / math.sqrt(A_HEAD_DIM)
        a_q = _prep(proj_blocks(0, 3), _head_pieces(A_HEADS), out_kind="cols", tm=tp,
                    gain=a_q_norm[l], norm="full", n_norm=A_HEAD_DIM, tabs=a_tabs, rope_blocks=(0,),
                    shifts=(96, 32), scale=a_scale, name="prep_a_q")
        a_k = _prep(proj_blocks(3, 1), _head_pieces(A_KV_HEADS), out_kind="rows", tm=tp,
                    gain=a_k_norm[l], norm="full", n_norm=A_HEAD_DIM, tabs=a_tabs, rope_blocks=(0,),
                    shifts=(96, 32), name="prep_a_k")
        a_v = _prep(proj_blocks(4, 1), _head_pieces(A_KV_HEADS), out_kind="chunks", tm=ATTN_KEY_CHUNK,
                    name="prep_a_v")
        ya = _attention(a_q, a_k, a_v, _score_bound(A_HEAD_DIM, a_q_norm[l], a_k_norm[l], a_scale),
                        tq=ATTN_Q_BLOCK_A, name="attn_a")

        b_scale = LOG2E / math.sqrt(B_QK_DIM)
        b_q = _prep(proj_blocks(5, 2), _head_pieces(B_HEADS), out_kind="cols", tm=tp,
                    gain=jnp.tile(b_q_norm[l], 2), norm="half", n_norm=B_QK_DIM, tabs=b_tabs,
                    rope_blocks=(0,), shifts=(120, 8), scale=b_scale, name="prep_b_q")
        b_k = _prep(proj_blocks(7, 2), _head_pieces(B_HEADS), out_kind="rows", tm=tp,
                    gain=jnp.tile(b_k_norm[l], 2), norm="half", n_norm=B_QK_DIM, tabs=b_tabs,
                    rope_blocks=(0,), shifts=(120, 8), name="prep_b_k")
        b_v = _prep(proj_blocks(9, 2), _head_pieces(B_HEADS), out_kind="chunks", tm=ATTN_KEY_CHUNK,
                    name="prep_b_v")
        yb = _attention(b_q, b_k, b_v, _score_bound(B_QK_DIM, b_q_norm[l], b_k_norm[l], b_scale),
                        tq=ATTN_Q_BLOCK_B, diff=True, lam_p=b_lambda[l], subln=b_subln[l],
                        lambda_init=lambda_init, name="attn_b")

        cq_lat = _prep(proj_blocks(11, 2), _head_pieces(1, blocks=C_Q_LORA // LANES), out_kind="rows", tm=tp,
                       gain=c_q_a_norm[l], norm="full", n_norm=C_Q_LORA, name="norm_c_q")[0]
        ckv_lat = _prep(proj_blocks(13, 1), _head_pieces(1, blocks=C_KV_LORA // LANES), out_kind="rows", tm=tp,
                        gain=c_kv_a_norm[l], norm="full", n_norm=C_KV_LORA, name="norm_c_kv")[0]
        c_q_raw = _matmul([cq_lat], [wq_up], out_dtype=jnp.float32, tm=ROW_TILE, tn=C_UP_COL_TILE,
                          name="c_q_up")
        c_kv_raw = _matmul([ckv_lat], [wkv_up], out_dtype=jnp.float32, tm=ROW_TILE, tn=C_UP_COL_TILE,
                           name="c_kv_up")
        c_scale = LOG2E / math.sqrt(C_QK_DIM)
        n_up = C_HEADS * C_QK_PAD // PROJ_BLOCK
        c_q = _prep([(c_q_raw, PROJ_BLOCK, b) for b in range(n_up)],
                    _head_pieces(C_HEADS, lanes_per_head=C_QK_PAD, blocks=2), out_kind="cols", tm=tp,
                    gain=_pad_vec(c_q_norm[l], C_QK_PAD), norm="full", n_norm=C_QK_DIM, tabs=c_tabs,
                    rope_blocks=(1,), shifts=(96, 32), scale=c_scale, name="prep_c_q")
        kv_srcs = [(c_kv_raw, PROJ_BLOCK, b) for b in range(n_up)]
        c_k_pieces = [[nope, (n_up, 0)] for (nope,) in _head_pieces(C_HEADS, lanes_per_head=2 * LANES)]
        c_k = _prep(kv_srcs + [(c_kr, LANES, 0)], c_k_pieces, out_kind="rows", tm=tp,
                    gain=_pad_vec(c_k_norm[l], C_QK_PAD), norm="full", n_norm=C_QK_DIM, tabs=c_tabs,
                    rope_blocks=(1,), shifts=(96, 32), name="prep_c_k")
        c_v = _prep(kv_srcs, _head_pieces(C_HEADS, lanes_per_head=2 * LANES, lane0=LANES), out_kind="chunks",
                    tm=ATTN_KEY_CHUNK, name="prep_c_v")
        yc = _attention(c_q, c_k, c_v, _score_bound(C_QK_DIM, c_q_norm[l], c_k_norm[l], c_scale),
                        tq=ATTN_Q_BLOCK_C, name="attn_c")

        h = _matmul([ya, yb, yc], [w_out], res=h, out_dtype=jnp.float32, tm=ROW_TILE, tn=OUT_COL_TILE, layer=l,
                    name="out_proj")

        hn = _rmsnorm(h, ffn_norm[l])
        act = _matmul([hn], [w_gate, w_up], out_dtype=bf, tm=GATE_UP_ROW_TILE, tn=FFN_COL_TILE, layer=l,
                      name="ffn_gate_up")
        h = _matmul_kouter(act, w_down, h, layer=l, tm=ROW_TILE, tn=FFN_COL_TILE, tk=FFN_HIDDEN // 2, n_split=1,
                           name="ffn_down")

    return h.reshape(bsz, s, d)
```

```python
import functools
import math

import jax
import jax.numpy as jnp
from jax import lax
from jax.experimental import pallas as pl
from jax.experimental.pallas import tpu as pltpu

D_MODEL = 4096
SEQ = 8192
DEPTH = 2
GRID_W = 64
EPS = 1e-6

A_HEAD_DIM = 128
A_HEADS = 12
A_KV_HEADS = 4
A_ROPE_THETA = 10000.0

B_QK_DIM = 64
B_V_DIM = 128
B_HEADS = 8
B_ROPE_DIM = 16
B_ROPE_THETA = 500000.0

C_V_DIM = 128
C_HEADS = 12
C_Q_LORA = 1024
C_KV_LORA = 512
C_NOPE = 128
C_ROPE = 64
C_QK_DIM = C_NOPE + C_ROPE
C_QK_PAD = 256
C_ROPE_THETA = 10000.0

FFN_HIDDEN = 11008

LANES = 128
SUBLANES = 8
ATTN_ROW_BLOCK = 64
ATTN_KEY_CHUNK = 512
ATTN_CHUNKS_PER_TRIP = 8
ATTN_MIN_DENOM = 2.0 ** -60
BOUND_SLACK = 1.01
MXU_SUM_MAX_WIDTH = 256
ONES_ROWS = 16
VMEM_LIMIT_BYTES = 56 * 1024 * 1024

ROW_TILE = 1024
GATE_UP_ROW_TILE = 2048
FFN_COL_TILE = 256
OUT_COL_TILE = 512
C_UP_COL_TILE = 3072
PREP_ROW_TILE = 512
ATTN_Q_BLOCK_A = 4096
ATTN_Q_BLOCK_B = 2048
ATTN_Q_BLOCK_C = 4096

LOG2E = math.log2(math.e)

PROJ_BLOCK = 512
N_IN_MAIN = 14 * PROJ_BLOCK


def _head_pieces(n_heads, lanes_per_head=LANES, lane0=0, blocks=1):
    out = []
    for h in range(n_heads):
        head = []
        for b in range(blocks):
            col = lane0 + h * lanes_per_head + b * LANES
            head.append((col // PROJ_BLOCK, col % PROJ_BLOCK))
        out.append(head)
    return out


def _cparams(sem):
    return pltpu.CompilerParams(dimension_semantics=sem, vmem_limit_bytes=VMEM_LIMIT_BYTES)


def _rmsnorm_body(x_ref, g_ref, o_ref):
    x = x_ref[...]
    ms = jnp.mean(x * x, axis=-1, keepdims=True)
    o_ref[...] = (x * lax.rsqrt(ms + EPS) * g_ref[...]).astype(o_ref.dtype)


def _rmsnorm(x, g, tm=512):
    s, d = x.shape
    return pl.pallas_call(
        _rmsnorm_body,
        grid=(s // tm,),
        in_specs=[pl.BlockSpec((tm, d), lambda i: (i, 0)),
                  pl.BlockSpec((1, d), lambda i: (0, 0))],
        out_specs=pl.BlockSpec((tm, d), lambda i: (i, 0)),
        out_shape=jax.ShapeDtypeStruct((s, d), jnp.bfloat16),
        compiler_params=_cparams(("parallel",)),
        name="rmsnorm",
    )(x, g.reshape(1, d))


def _matmul_body(*refs, n_x, n_w, has_res, w_transposed):
    x_refs = refs[:n_x]
    w_refs = refs[n_x:n_x + n_w]
    pos = n_x + n_w
    res_ref = refs[pos] if has_res else None
    o_ref = refs[pos + int(has_res)]

    xs = [r[...] for r in x_refs]
    tiles = [w[...] if w.dtype == xs[0].dtype else w[...].astype(xs[0].dtype) for w in w_refs]
    contract = (((1,), (1,)), ((), ())) if w_transposed else (((1,), (0,)), ((), ()))
    if n_x == 1:
        prods = [lax.dot_general(xs[0], w, contract, preferred_element_type=jnp.float32) for w in tiles]
    else:
        assert not w_transposed
        prods = []
        for w in tiles:
            row, total = 0, None
            for x in xs:
                part = jnp.dot(x, w[row:row + x.shape[1], :], preferred_element_type=jnp.float32)
                total = part if total is None else total + part
                row += x.shape[1]
            prods.append(total)
    if n_w == 2:
        g, u = prods
        y = g * jax.nn.sigmoid(g) * u
    else:
        y = prods[0]
    if has_res:
        y = y + res_ref[...]
    o_ref[...] = y.astype(o_ref.dtype)


def _matmul(xs, ws, res=None, *, out_dtype, tm, tn, name, layer=None, w_transposed=False, n_cols=None):
    m = xs[0].shape[0]
    kdim = sum(x.shape[1] for x in xs)
    n = n_cols if n_cols is not None else ws[0].shape[-2 if w_transposed else -1]
    assert m % tm == 0 and n % tn == 0
    in_specs = [pl.BlockSpec((tm, x.shape[1]), lambda i, j: (i, 0)) for x in xs]
    w_block = (tn, kdim) if w_transposed else (kdim, tn)
    lead = () if layer is None else (None,)

    def w_index(i, j):
        pos = (j, 0) if w_transposed else (0, j)
        return pos if layer is None else (layer,) + pos

    in_specs += [pl.BlockSpec(lead + w_block, w_index) for _ in ws]
    args = [*xs, *ws]
    if res is not None:
        in_specs.append(pl.BlockSpec((tm, tn), lambda i, j: (i, j)))
        args.append(res)
    return pl.pallas_call(
        functools.partial(_matmul_body, n_x=len(xs), n_w=len(ws), has_res=res is not None,
                          w_transposed=w_transposed),
        grid=(m // tm, n // tn),
        in_specs=in_specs,
        out_specs=pl.BlockSpec((tm, tn), lambda i, j: (i, j)),
        out_shape=jax.ShapeDtypeStruct((m, n), out_dtype),
        compiler_params=_cparams(("parallel", "parallel")),
        name=name,
    )(*args)


def _tail_proj_body(x_ref, w_ref, o_ref):
    w = w_ref[...].astype(x_ref.dtype)
    w = jnp.concatenate([w, jnp.zeros_like(w)], axis=0)
    o_ref[...] = lax.dot_general(x_ref[...], w, (((1,), (1,)), ((), ())),
                                 preferred_element_type=jnp.float32)


def _tail_proj(x, w_t, *, layer, row0, tm, name):
    m, kdim = x.shape
    rows = w_t.shape[1] - row0
    assert rows * 2 == LANES and row0 % rows == 0
    return pl.pallas_call(
        _tail_proj_body,
        grid=(m // tm,),
        in_specs=[pl.BlockSpec((tm, kdim), lambda i: (i, 0)),
                  pl.BlockSpec((None, rows, kdim), lambda i: (layer, row0 // rows, 0))],
        out_specs=pl.BlockSpec((tm, LANES), lambda i: (i, 0)),
        out_shape=jax.ShapeDtypeStruct((m, LANES), jnp.float32),
        compiler_params=_cparams(("parallel",)),
        name=name,
    )(x, w_t)


def _matmul_kouter_body(x_ref, w_ref, res_ref, o_ref, acc_ref, *, nk):
    k = pl.program_id(2)
    j = pl.program_id(3)
    prod = jnp.dot(x_ref[...], w_ref[...].astype(x_ref.dtype), preferred_element_type=jnp.float32)

    @pl.when(k == 0)
    def _():
        acc_ref[j] = prod

    if nk > 2:
        @pl.when(jnp.logical_and(k > 0, k < nk - 1))
        def _():
            acc_ref[j] += prod

    @pl.when(k == nk - 1)
    def _():
        o_ref[...] = (acc_ref[j] + prod + res_ref[...]).astype(o_ref.dtype)


def _matmul_kouter(x, w, res, *, layer, tm, tn, tk, n_split, name):
    m, kdim = x.shape
    n = w.shape[-1]
    nk = kdim // tk
    nj = n // (tn * n_split)
    assert m % tm == 0 and n % (tn * n_split) == 0 and kdim % tk == 0 and nk >= 2

    def out_index(hf, i, k, j):
        return i, hf * nj + jnp.where(k == nk - 1, j, 0)

    return pl.pallas_call(
        functools.partial(_matmul_kouter_body, nk=nk),
        grid=(n_split, m // tm, nk, nj),
        in_specs=[pl.BlockSpec((tm, tk), lambda hf, i, k, j: (i, k)),
                  pl.BlockSpec((None, tk, tn), lambda hf, i, k, j: (layer, k, hf * nj + j)),
                  pl.BlockSpec((tm, tn), out_index)],
        out_specs=pl.BlockSpec((tm, tn), out_index),
        out_shape=jax.ShapeDtypeStruct((m, n), res.dtype),
        scratch_shapes=[pltpu.VMEM((nj, tm, tn), jnp.float32)],
        compiler_params=_cparams(("parallel", "parallel", "arbitrary", "arbitrary")),
        name=name,
    )(x, w, res)


def _prep_head(x, g_ref, tabs, *, norm, n_norm, rope_blocks, shifts, scale, transpose, ones_rows):
    width = x.shape[-1]
    if norm is not None:
        if width <= MXU_SUM_MAX_WIDTH:
            r = lax.broadcasted_iota(jnp.int32, (width, width), 0)
            c = lax.broadcasted_iota(jnp.int32, (width, width), 1)
            group = LANES // 2 if norm == "half" else width
            same = (r // group == c // group).astype(jnp.bfloat16)
            ss = jnp.dot((x * x).astype(jnp.bfloat16), same, preferred_element_type=jnp.float32)
        else:
            ss = jnp.sum(x * x, axis=-1, keepdims=True)
        x = x * lax.rsqrt(ss * (1.0 / n_norm) + EPS) * g_ref[...]

    if rope_blocks:
        c_ref, s1_ref, s2_ref = tabs
        blocks = []
        for b in range(width // LANES):
            xb = x[:, b * LANES:(b + 1) * LANES]
            if b in rope_blocks:
                sl = slice(b * LANES, (b + 1) * LANES)
                xb = (xb * c_ref[:, sl]
                      + pltpu.roll(xb, shifts[0], 1) * s1_ref[:, sl]
                      + pltpu.roll(xb, shifts[1], 1) * s2_ref[:, sl])
            blocks.append(xb)
        x = blocks[0] if len(blocks) == 1 else jnp.concatenate(blocks, axis=-1)

    if scale != 1.0:
        x = x * scale
    x = x.astype(jnp.bfloat16)
    if transpose:
        eye = (lax.broadcasted_iota(jnp.int32, (width, width), 0)
               == lax.broadcasted_iota(jnp.int32, (width, width), 1)).astype(jnp.bfloat16)
        x = lax.dot_general(eye, x, (((1,), (1,)), ((), ())),
                            preferred_element_type=jnp.float32).astype(jnp.bfloat16)
    if ones_rows:
        x = jnp.concatenate([x, jnp.ones((ones_rows, x.shape[1]), x.dtype)], axis=0)
    return x


def _prep_body(*refs, nx, pieces, has_gain, has_tabs, **head_kw):
    x_refs = refs[:nx]
    pos = nx
    g_ref = None
    if has_gain:
        g_ref = refs[pos]
        pos += 1
    tabs = None
    if has_tabs:
        tabs = refs[pos:pos + 3]
        pos += 3
    o_ref = refs[pos]
    for h, head_pieces in enumerate(pieces):
        cols = [x_refs[src][:, off:off + LANES] for src, off in head_pieces]
        x = cols[0] if len(cols) == 1 else jnp.concatenate(cols, axis=-1)
        y = _prep_head(x, g_ref, tabs, **head_kw)
        o_ref[h] = y.reshape(o_ref.shape[1:])


def _prep(srcs, pieces, *, out_kind, tm, gain=None, norm=None, n_norm=None, tabs=None,
          rope_blocks=(), shifts=(0, 0), scale=1.0, name="prep"):
    s = srcs[0][0].shape[0]
    n_heads = len(pieces)
    width = LANES * len(pieces[0])
    in_specs, args = [], []
    for arr, bw, idx in srcs:
        in_specs.append(pl.BlockSpec((tm, bw), functools.partial(lambda i, idx: (i, idx), idx=idx)))
        args.append(arr)
    if norm is not None:
        in_specs.append(pl.BlockSpec((1, width), lambda i: (0, 0)))
        args.append(gain.reshape(1, width))
    if rope_blocks:
        for t in tabs:
            in_specs.append(pl.BlockSpec((tm, width), lambda i: (i, 0)))
            args.append(t)
    if out_kind == "rows":
        out_shape = (n_heads, s, width)
        out_spec = pl.BlockSpec((n_heads, tm, width), lambda i: (0, i, 0))
    elif out_kind == "cols":
        out_shape = (n_heads, s // tm, width, tm)
        out_spec = pl.BlockSpec((n_heads, 1, width, tm), lambda i: (0, i, 0, 0))
    else:
        out_shape = (n_heads, s // tm, width + ONES_ROWS, tm)
        out_spec = pl.BlockSpec((n_heads, 1, width + ONES_ROWS, tm), lambda i: (0, i, 0, 0))
    return pl.pallas_call(
        functools.partial(_prep_body, nx=len(srcs), pieces=tuple(tuple(p) for p in pieces),
                          has_gain=norm is not None, has_tabs=bool(rope_blocks),
                          norm=norm, n_norm=n_norm, rope_blocks=tuple(rope_blocks), shifts=shifts,
                          scale=scale, transpose=out_kind != "rows",
                          ones_rows=ONES_ROWS if out_kind == "chunks" else 0),
        grid=(s // tm,),
        in_specs=in_specs,
        out_specs=out_spec,
        out_shape=jax.ShapeDtypeStruct(out_shape, jnp.bfloat16),
        compiler_params=_cparams(("parallel",)),
        name=name,
    )(*args)


def _attn_body(*refs, nchunk, tk, tq, diff, lambda_init):
    if diff:
        (qT_ref, k_ref, vT_ref, shift_ref, lam_ref, sub_ref, o_ref,
         acc_ref, p_ref, qq_ref, m_ref, s_ref, ps_ref) = refs
    else:
        qT_ref, k_ref, vT_ref, shift_ref, o_ref, acc_ref, p_ref, qq_ref, m_ref, s_ref, ps_ref = refs

    i = pl.program_id(1)
    nqb, _, nq = qq_ref.shape
    dv = o_ref.shape[1]
    tiles_per_block = tq // qT_ref.shape[3]
    chunk_bits = nchunk.bit_length() - 1
    assert nchunk == 1 << chunk_bits

    def scores(step):
        chunk = step & (nchunk - 1)
        block = jnp.minimum(i + (step >> chunk_bits), nqb - 1)
        start = pl.multiple_of(chunk * tk, tk)
        return jnp.dot(k_ref[0, pl.ds(start, tk), :], qq_ref[block],
                       preferred_element_type=jnp.float32)

    def probs(step):
        return jnp.exp2(scores(step) - shift_ref[...]).astype(jnp.bfloat16)

    @pl.when(i == 0)
    def _():
        for b in range(nqb):
            tiles = [qT_ref[0, b * tiles_per_block + c] for c in range(tiles_per_block)]
            q = tiles[0] if len(tiles) == 1 else jnp.concatenate(tiles, axis=1)
            if diff:
                first = lax.broadcasted_iota(jnp.int32, q.shape, 0) < B_QK_DIM
                zero = jnp.zeros_like(q)
                q = jnp.concatenate([jnp.where(first, q, zero), jnp.where(first, zero, q)], axis=1)
            qq_ref[b] = q
        p_ref[0] = probs(0)

    acc_ref[...] = jnp.zeros(acc_ref.shape, jnp.float32)

    def trip(t, carry):
        for u in range(ATTN_CHUNKS_PER_TRIP):
            j = t * ATTN_CHUNKS_PER_TRIP + u
            p_ref[(u + 1) % 2] = probs(j + 1)
            acc_ref[...] += jnp.dot(vT_ref[0, j], p_ref[u % 2], preferred_element_type=jnp.float32)
        return carry

    lax.fori_loop(0, nchunk // ATTN_CHUNKS_PER_TRIP, trip, 0)

    healthy = jnp.min(acc_ref[dv:dv + 1, :]) >= ATTN_MIN_DENOM

    @pl.when(jnp.logical_not(healthy))
    def _():
        rb = ATTN_ROW_BLOCK
        groups = rb // SUBLANES
        m_ref[...] = jnp.full(m_ref.shape, -jnp.inf, jnp.float32)
        acc_ref[...] = jnp.zeros(acc_ref.shape, jnp.float32)

        def block(r):
            return s_ref[r * rb:(r + 1) * rb, :].reshape(groups, SUBLANES, nq)

        def chunk(j, carry):
            s_ref[...] = scores(j)
            mx = jnp.max(block(0), axis=0)
            for r in range(1, tk // rb):
                mx = jnp.maximum(mx, jnp.max(block(r), axis=0))
            m_old = m_ref[...]
            m_new = jnp.maximum(m_old, jnp.max(mx, axis=0, keepdims=True))
            alpha = jnp.exp2(m_old - m_new)
            m_rows = jnp.broadcast_to(m_new, (SUBLANES, nq))[None]
            for r in range(tk // rb):
                p = jnp.exp2(block(r) - m_rows).reshape(rb, nq)
                ps_ref[r * rb:(r + 1) * rb, :] = p.astype(jnp.bfloat16)
            pv = jnp.dot(vT_ref[0, j], ps_ref[...], preferred_element_type=jnp.float32)
            acc_ref[...] = acc_ref[...] * alpha + pv
            m_ref[...] = m_new
            return carry

        lax.fori_loop(0, nchunk, chunk, 0)

    o = acc_ref[:dv, :] / acc_ref[dv:dv + 1, :]
    if diff:
        lp = lam_ref[...]
        lam = (jnp.exp(jnp.sum(lp[0:1] * lp[1:2], axis=-1, keepdims=True))
               - jnp.exp(jnp.sum(lp[2:3] * lp[3:4], axis=-1, keepdims=True))
               + lambda_init)
        o = o[:, :tq] - lam * o[:, tq:]
        ms = jnp.mean(o * o, axis=0, keepdims=True)
        o = o * lax.rsqrt(ms + EPS) * sub_ref[...] * (1.0 - lambda_init)
    o_ref[...] = o.T.astype(o_ref.dtype)


def _attention(qT, k, vT, score_bound, *, tq, diff=False, lam_p=None, subln=None, lambda_init=0.0, name):
    n_heads, n_qtiles, dq, q_tile = qT.shape
    s = n_qtiles * q_tile
    n_kv = k.shape[0]
    group = n_heads // n_kv
    _, nchunk, dv_ext, tk = vT.shape
    dv = dv_ext - ONES_ROWS
    nq = 2 * tq if diff else tq
    assert tq % q_tile == 0 and ATTN_CHUNKS_PER_TRIP % 2 == 0 and nchunk % ATTN_CHUNKS_PER_TRIP == 0
    in_specs = [pl.BlockSpec((1, n_qtiles, dq, q_tile), lambda h, i: (h, 0, 0, 0)),
                pl.BlockSpec((1, s, dq), lambda h, i: (h // group, 0, 0)),
                pl.BlockSpec((1, nchunk, dv_ext, tk), lambda h, i: (h // group, 0, 0, 0)),
                pl.BlockSpec((1, nq), lambda h, i: (0, 0))]
    args = [qT, k, vT, jnp.full((1, nq), score_bound, jnp.float32)]
    if diff:
        in_specs += [pl.BlockSpec(lam_p.shape, lambda h, i: (0, 0)),
                     pl.BlockSpec((dv, 1), lambda h, i: (0, 0))]
        args += [lam_p, subln.reshape(dv, 1)]
    return pl.pallas_call(
        functools.partial(_attn_body, nchunk=nchunk, tk=tk, tq=tq, diff=diff, lambda_init=lambda_init),
        grid=(n_heads, s // tq),
        in_specs=in_specs,
        out_specs=pl.BlockSpec((tq, dv), lambda h, i: (i, h)),
        out_shape=jax.ShapeDtypeStruct((s, n_heads * dv), jnp.bfloat16),
        scratch_shapes=[pltpu.VMEM((dv_ext, nq), jnp.float32),
                        pltpu.VMEM((2, tk, nq), jnp.bfloat16),
                        pltpu.VMEM((s // tq, dq, nq), jnp.bfloat16),
                        pltpu.VMEM((1, nq), jnp.float32),
                        pltpu.VMEM((tk, nq), jnp.float32),
                        pltpu.VMEM((tk, nq), jnp.bfloat16)],
        compiler_params=_cparams(("arbitrary", "arbitrary")),
        name=name,
    )(*args)


def _rope_cos_sin(pos, dim, theta):
    inv = theta ** (-jnp.arange(0, dim, 2, dtype=jnp.float32) / dim)
    ang = pos.astype(jnp.float32)[:, None] * inv[None, :]
    return jnp.cos(ang), jnp.sin(ang)


def _rotate_half_tables(cos, sin):
    zero = jnp.zeros_like(sin)
    return (jnp.concatenate([cos, cos], axis=-1),
            jnp.concatenate([-sin, zero], axis=-1),
            jnp.concatenate([zero, sin], axis=-1))


def _rope_tables(s):
    t = jnp.arange(s, dtype=jnp.int32)
    row = t // GRID_W
    col = t % GRID_W
    half = A_HEAD_DIM // 2
    a_row = _rotate_half_tables(*_rope_cos_sin(row, half, A_ROPE_THETA))
    a_col = _rotate_half_tables(*_rope_cos_sin(col, half, A_ROPE_THETA))
    a_tabs = tuple(jnp.concatenate([r, c], axis=-1) for r, c in zip(a_row, a_col))

    b_rot = _rotate_half_tables(*_rope_cos_sin(t, B_ROPE_DIM, B_ROPE_THETA))
    rest = B_QK_DIM - B_ROPE_DIM
    fill = (jnp.ones((s, rest), jnp.float32), jnp.zeros((s, rest), jnp.float32),
            jnp.zeros((s, rest), jnp.float32))
    b_tabs = tuple(jnp.tile(jnp.concatenate([r, f], axis=-1), (1, 2)) for r, f in zip(b_rot, fill))

    c_rot = _rotate_half_tables(*_rope_cos_sin(t, C_ROPE, C_ROPE_THETA))
    ones = jnp.ones((s, C_NOPE), jnp.float32)
    zeros = jnp.zeros((s, C_NOPE), jnp.float32)
    pad = jnp.zeros((s, C_QK_PAD - C_QK_DIM), jnp.float32)
    c_tabs = (jnp.concatenate([ones, c_rot[0], pad], axis=-1),
              jnp.concatenate([zeros, c_rot[1], pad], axis=-1),
              jnp.concatenate([zeros, c_rot[2], pad], axis=-1))
    return a_tabs, b_tabs, c_tabs


def _score_bound(dim, q_gain, k_gain, scale):
    return BOUND_SLACK * dim * scale * jnp.max(jnp.abs(q_gain)) * jnp.max(jnp.abs(k_gain))


def _pad_vec(g, n):
    return jnp.pad(g, (0, n - g.shape[0]))


def kernel(x, attn_norm, w_in, a_q_norm, a_k_norm, b_q_norm, b_k_norm, b_lambda, b_subln,
           c_q_a_norm, c_kv_a_norm, c_w_q_up, c_w_kv_up, c_q_norm, c_k_norm, w_out,
           ffn_norm, w_gate, w_up, w_down):
    bsz, s, d = x.shape
    assert (bsz, s, d) == (1, SEQ, D_MODEL)
    bf = jnp.bfloat16
    a_tabs, b_tabs, c_tabs = _rope_tables(s)
    tp = PREP_ROW_TILE
    h = x.reshape(s, d)
    w_in_t = jnp.swapaxes(w_in, 1, 2)

    for l in range(DEPTH):
        lambda_init = 0.8 - 0.6 * math.exp(-0.3 * l)
        wq_up = c_w_q_up[l].astype(bf).reshape(C_Q_LORA, C_HEADS, C_QK_DIM)
        wq_up = jnp.pad(wq_up, ((0, 0), (0, 0), (0, C_QK_PAD - C_QK_DIM))).reshape(C_Q_LORA, C_HEADS * C_QK_PAD)
        wkv_up = c_w_kv_up[l].astype(bf)

        xn = _rmsnorm(h, attn_norm[l])
        proj = _matmul([xn], [w_in_t], out_dtype=jnp.float32, tm=ROW_TILE, tn=PROJ_BLOCK, layer=l,
                       w_transposed=True, n_cols=N_IN_MAIN, name="in_proj")
        c_kr = _tail_proj(xn, w_in_t, layer=l, row0=N_IN_MAIN, tm=ROW_TILE, name="in_proj_kr")

        def proj_blocks(first, count):
            return [(proj, PROJ_BLOCK, first + b) for b in range(count)]

        a_scale = LOG2E / math.sqrt(A_HEAD_DIM)
        a_q = _prep(proj_blocks(0, 3), _head_pieces(A_HEADS), out_kind="cols", tm=tp,
                    gain=a_q_norm[l], norm="full", n_norm=A_HEAD_DIM, tabs=a_tabs, rope_blocks=(0,),
                    shifts=(96, 32), scale=a_scale, name="prep_a_q")
        a_k = _prep(proj_blocks(3, 1), _head_pieces(A_KV_HEADS), out_kind="rows", tm=tp,
                    gain=a_k_norm[l], norm="full", n_norm=A_HEAD_DIM, tabs=a_tabs, rope_blocks=(0,),
                    shifts=(96, 32), name="prep_a_k")
        a_v = _prep(proj_blocks(4, 1), _head_pieces(A_KV_HEADS), out_kind="chunks", tm=ATTN_KEY_CHUNK,
                    name="prep_a_v")
        ya = _attention(a_q, a_k, a_v, _score_bound(A_HEAD_DIM, a_q_norm[l], a_k_norm[l], a_scale),
                        tq=ATTN_Q_BLOCK_A, name="attn_a")

        b_scale = LOG2E / math.sqrt(B_QK_DIM)
        b_q = _prep(proj_blocks(5, 2), _head_pieces(B_HEADS), out_kind="cols", tm=tp,
                    gain=jnp.tile(b_q_norm[l], 2), norm="half", n_norm=B_QK_DIM, tabs=b_tabs,
                    rope_blocks=(0,), shifts=(120, 8), scale=b_scale, name="prep_b_q")
        b_k = _prep(proj_blocks(7, 2), _head_pieces(B_HEADS), out_kind="rows", tm=tp,
                    gain=jnp.tile(b_k_norm[l], 2), norm="half", n_norm=B_QK_DIM, tabs=b_tabs,
                    rope_blocks=(0,), shifts=(120, 8), name="prep_b_k")
        b_v = _prep(proj_blocks(9, 2), _head_pieces(B_HEADS), out_kind="chunks", tm=ATTN_KEY_CHUNK,
                    name="prep_b_v")
        yb = _attention(b_q, b_k, b_v, _score_bound(B_QK_DIM, b_q_norm[l], b_k_norm[l], b_scale),
                        tq=ATTN_Q_BLOCK_B, diff=True, lam_p=b_lambda[l], subln=b_subln[l],
                        lambda_init=lambda_init, name="attn_b")

        cq_lat = _prep(proj_blocks(11, 2), _head_pieces(1, blocks=C_Q_LORA // LANES), out_kind="rows", tm=ROW_TILE,
                       gain=c_q_a_norm[l], norm="full", n_norm=C_Q_LORA, name="norm_c_q")[0]
        ckv_lat = _prep(proj_blocks(13, 1), _head_pieces(1, blocks=C_KV_LORA // LANES), out_kind="rows", tm=ROW_TILE,
                        gain=c_kv_a_norm[l], norm="full", n_norm=C_KV_LORA, name="norm_c_kv")[0]
        c_q_raw = _matmul([cq_lat], [wq_up], out_dtype=jnp.float32, tm=ROW_TILE, tn=C_UP_COL_TILE,
                          name="c_q_up")
        c_kv_raw = _matmul([ckv_lat], [wkv_up], out_dtype=jnp.float32, tm=ROW_TILE, tn=C_UP_COL_TILE,
                           name="c_kv_up")
        c_scale = LOG2E / math.sqrt(C_QK_DIM)
        n_up = C_HEADS * C_QK_PAD // PROJ_BLOCK
        c_q = _prep([(c_q_raw, PROJ_BLOCK, b) for b in range(n_up)],
                    _head_pieces(C_HEADS, lanes_per_head=C_QK_PAD, blocks=2), out_kind="cols", tm=tp,
                    gain=_pad_vec(c_q_norm[l], C_QK_PAD), norm="full", n_norm=C_QK_DIM, tabs=c_tabs,
                    rope_blocks=(1,), shifts=(96, 32), scale=c_scale, name="prep_c_q")
        kv_srcs = [(c_kv_raw, PROJ_BLOCK, b) for b in range(n_up)]
        c_k_pieces = [[nope, (n_up, 0)] for (nope,) in _head_pieces(C_HEADS, lanes_per_head=2 * LANES)]
        c_k = _prep(kv_srcs + [(c_kr, LANES, 0)], c_k_pieces, out_kind="rows", tm=tp,
                    gain=_pad_vec(c_k_norm[l], C_QK_PAD), norm="full", n_norm=C_QK_DIM, tabs=c_tabs,
                    rope_blocks=(1,), shifts=(96, 32), name="prep_c_k")
        c_v = _prep(kv_srcs, _head_pieces(C_HEADS, lanes_per_head=2 * LANES, lane0=LANES), out_kind="chunks",
                    tm=ATTN_KEY_CHUNK, name="prep_c_v")
        yc = _attention(c_q, c_k, c_v, _score_bound(C_QK_DIM, c_q_norm[l], c_k_norm[l], c_scale),
                        tq=ATTN_Q_BLOCK_C, name="attn_c")

        h = _matmul([ya, yb, yc], [w_out], res=h, out_dtype=jnp.float32, tm=ROW_TILE, tn=OUT_COL_TILE, layer=l,
                    name="out_proj")

        hn = _rmsnorm(h, ffn_norm[l])
        act = _matmul([hn], [w_gate, w_up], out_dtype=bf, tm=GATE_UP_ROW_TILE, tn=FFN_COL_TILE, layer=l,
                      name="ffn_gate_up")
        h = _matmul_kouter(act, w_down, h, layer=l, tm=ROW_TILE, tn=FFN_COL_TILE, tk=FFN_HIDDEN // 2, n_split=1,
                           name="ffn_down")

    return h.reshape(bsz, s, d)
```

```python
import functools
import math

import jax
import jax.numpy as jnp
from jax import lax
from jax.experimental import pallas as pl
from jax.experimental.pallas import tpu as pltpu

D_MODEL = 4096
SEQ = 8192
DEPTH = 2
GRID_W = 64
EPS = 1e-6

A_HEAD_DIM = 128
A_HEADS = 12
A_KV_HEADS = 4
A_ROPE_THETA = 10000.0

B_QK_DIM = 64
B_V_DIM = 128
B_HEADS = 8
B_ROPE_DIM = 16
B_ROPE_THETA = 500000.0

C_V_DIM = 128
C_HEADS = 12
C_Q_LORA = 1024
C_KV_LORA = 512
C_NOPE = 128
C_ROPE = 64
C_QK_DIM = C_NOPE + C_ROPE
C_QK_PAD = 256
C_ROPE_THETA = 10000.0

FFN_HIDDEN = 11008

LANES = 128
SUBLANES = 8
ATTN_ROW_BLOCK = 64
ATTN_KEY_CHUNK = 512
ATTN_CHUNKS_PER_TRIP = 8
ATTN_MIN_DENOM = 2.0 ** -60
BOUND_SLACK = 1.01
MXU_SUM_MAX_WIDTH = 256
ONES_ROWS = 16
VMEM_LIMIT_BYTES = 56 * 1024 * 1024

ROW_TILE = 1024
GATE_UP_ROW_TILE = 2048
FFN_COL_TILE = 256
OUT_COL_TILE = 512
C_UP_COL_TILE = 3072
PREP_ROW_TILE = 512
ATTN_Q_BLOCK_A = 4096
ATTN_Q_BLOCK_B = 2048
ATTN_Q_BLOCK_C = 4096

LOG2E = math.log2(math.e)

PROJ_BLOCK = 512
N_IN_MAIN = 14 * PROJ_BLOCK


def _head_pieces(n_heads, lanes_per_head=LANES, lane0=0, blocks=1):
    out = []
    for h in range(n_heads):
        head = []
        for b in range(blocks):
            col = lane0 + h * lanes_per_head + b * LANES
            head.append((col // PROJ_BLOCK, col % PROJ_BLOCK))
        out.append(head)
    return out


def _cparams(sem):
    return pltpu.CompilerParams(dimension_semantics=sem, vmem_limit_bytes=VMEM_LIMIT_BYTES)


def _rmsnorm_body(x_ref, g_ref, o_ref):
    x = x_ref[...]
    ms = jnp.mean(x * x, axis=-1, keepdims=True)
    o_ref[...] = (x * lax.rsqrt(ms + EPS) * g_ref[...]).astype(o_ref.dtype)


def _rmsnorm(x, g, tm=512):
    s, d = x.shape
    return pl.pallas_call(
        _rmsnorm_body,
        grid=(s // tm,),
        in_specs=[pl.BlockSpec((tm, d), lambda i: (i, 0)),
                  pl.BlockSpec((1, d), lambda i: (0, 0))],
        out_specs=pl.BlockSpec((tm, d), lambda i: (i, 0)),
        out_shape=jax.ShapeDtypeStruct((s, d), jnp.bfloat16),
        compiler_params=_cparams(("parallel",)),
        name="rmsnorm",
    )(x, g.reshape(1, d))


def _matmul_body(*refs, n_x, n_w, has_res, w_transposed):
    x_refs = refs[:n_x]
    w_refs = refs[n_x:n_x + n_w]
    pos = n_x + n_w
    res_ref = refs[pos] if has_res else None
    o_ref = refs[pos + int(has_res)]

    xs = [r[...] for r in x_refs]
    tiles = [w[...] if w.dtype == xs[0].dtype else w[...].astype(xs[0].dtype) for w in w_refs]
    contract = (((1,), (1,)), ((), ())) if w_transposed else (((1,), (0,)), ((), ()))
    if n_x == 1:
        prods = [lax.dot_general(xs[0], w, contract, preferred_element_type=jnp.float32) for w in tiles]
    else:
        assert not w_transposed
        prods = []
        for w in tiles:
            row, total = 0, None
            for x in xs:
                part = jnp.dot(x, w[row:row + x.shape[1], :], preferred_element_type=jnp.float32)
                total = part if total is None else total + part
                row += x.shape[1]
            prods.append(total)
    if n_w == 2:
        g, u = prods
        y = g * jax.nn.sigmoid(g) * u
    else:
        y = prods[0]
    if has_res:
        y = y + res_ref[...]
    o_ref[...] = y.astype(o_ref.dtype)


def _matmul(xs, ws, res=None, *, out_dtype, tm, tn, name, layer=None, w_transposed=False, n_cols=None,
            single_buffer_x=False):
    m = xs[0].shape[0]
    kdim = sum(x.shape[1] for x in xs)
    n = n_cols if n_cols is not None else ws[0].shape[-2 if w_transposed else -1]
    assert m % tm == 0 and n % tn == 0
    in_specs = [pl.BlockSpec((tm, x.shape[1]), lambda i, j: (i, 0)) for x in xs]
    w_block = (tn, kdim) if w_transposed else (kdim, tn)
    lead = () if layer is None else (None,)

    def w_index(i, j):
        pos = (j, 0) if w_transposed else (0, j)
        return pos if layer is None else (layer,) + pos

    if single_buffer_x:
        in_specs = [pl.BlockSpec((tm, x.shape[1]), lambda i, j: (i, 0), pipeline_mode=pl.Buffered(1))
                    for x in xs]
    in_specs += [pl.BlockSpec(lead + w_block, w_index) for _ in ws]
    args = [*xs, *ws]
    if res is not None:
        in_specs.append(pl.BlockSpec((tm, tn), lambda i, j: (i, j)))
        args.append(res)
    return pl.pallas_call(
        functools.partial(_matmul_body, n_x=len(xs), n_w=len(ws), has_res=res is not None,
                          w_transposed=w_transposed),
        grid=(m // tm, n // tn),
        in_specs=in_specs,
        out_specs=pl.BlockSpec((tm, tn), lambda i, j: (i, j)),
        out_shape=jax.ShapeDtypeStruct((m, n), out_dtype),
        compiler_params=_cparams(("parallel", "parallel")),
        name=name,
    )(*args)


def _tail_proj_body(x_ref, w_ref, o_ref):
    w = w_ref[...].astype(x_ref.dtype)
    w = jnp.concatenate([w, jnp.zeros_like(w)], axis=0)
    o_ref[...] = lax.dot_general(x_ref[...], w, (((1,), (1,)), ((), ())),
                                 preferred_element_type=jnp.float32)


def _tail_proj(x, w_t, *, layer, row0, tm, name):
    m, kdim = x.shape
    rows = w_t.shape[1] - row0
    assert rows * 2 == LANES and row0 % rows == 0
    return pl.pallas_call(
        _tail_proj_body,
        grid=(m // tm,),
        in_specs=[pl.BlockSpec((tm, kdim), lambda i: (i, 0)),
                  pl.BlockSpec((None, rows, kdim), lambda i: (layer, row0 // rows, 0))],
        out_specs=pl.BlockSpec((tm, LANES), lambda i: (i, 0)),
        out_shape=jax.ShapeDtypeStruct((m, LANES), jnp.float32),
        compiler_params=_cparams(("parallel",)),
        name=name,
    )(x, w_t)


def _matmul_kouter_body(x_ref, w_ref, res_ref, o_ref, acc_ref, *, nk):
    k = pl.program_id(2)
    j = pl.program_id(3)
    prod = jnp.dot(x_ref[...], w_ref[...].astype(x_ref.dtype), preferred_element_type=jnp.float32)

    @pl.when(k == 0)
    def _():
        acc_ref[j] = prod

    if nk > 2:
        @pl.when(jnp.logical_and(k > 0, k < nk - 1))
        def _():
            acc_ref[j] += prod

    @pl.when(k == nk - 1)
    def _():
        o_ref[...] = (acc_ref[j] + prod + res_ref[...]).astype(o_ref.dtype)


def _matmul_kouter(x, w, res, *, layer, tm, tn, tk, n_split, name):
    m, kdim = x.shape
    n = w.shape[-1]
    nk = kdim // tk
    nj = n // (tn * n_split)
    assert m % tm == 0 and n % (tn * n_split) == 0 and kdim % tk == 0 and nk >= 2

    def out_index(hf, i, k, j):
        return i, hf * nj + jnp.where(k == nk - 1, j, 0)

    return pl.pallas_call(
        functools.partial(_matmul_kouter_body, nk=nk),
        grid=(n_split, m // tm, nk, nj),
        in_specs=[pl.BlockSpec((tm, tk), lambda hf, i, k, j: (i, k)),
                  pl.BlockSpec((None, tk, tn), lambda hf, i, k, j: (layer, k, hf * nj + j)),
                  pl.BlockSpec((tm, tn), out_index)],
        out_specs=pl.BlockSpec((tm, tn), out_index),
        out_shape=jax.ShapeDtypeStruct((m, n), res.dtype),
        scratch_shapes=[pltpu.VMEM((nj, tm, tn), jnp.float32)],
        compiler_params=_cparams(("parallel", "parallel", "arbitrary", "arbitrary")),
        name=name,
    )(x, w, res)


def _prep_head(x, g_ref, tabs, *, norm, n_norm, rope_blocks, shifts, scale, transpose, ones_rows):
    width = x.shape[-1]
    if norm is not None:
        if width <= MXU_SUM_MAX_WIDTH:
            r = lax.broadcasted_iota(jnp.int32, (width, width), 0)
            c = lax.broadcasted_iota(jnp.int32, (width, width), 1)
            group = LANES // 2 if norm == "half" else width
            same = (r // group == c // group).astype(jnp.bfloat16)
            ss = jnp.dot((x * x).astype(jnp.bfloat16), same, preferred_element_type=jnp.float32)
        else:
            ss = jnp.sum(x * x, axis=-1, keepdims=True)
        x = x * lax.rsqrt(ss * (1.0 / n_norm) + EPS) * g_ref[...]

    if rope_blocks:
        c_ref, s1_ref, s2_ref = tabs
        blocks = []
        for b in range(width // LANES):
            xb = x[:, b * LANES:(b + 1) * LANES]
            if b in rope_blocks:
                sl = slice(b * LANES, (b + 1) * LANES)
                xb = (xb * c_ref[:, sl]
                      + pltpu.roll(xb, shifts[0], 1) * s1_ref[:, sl]
                      + pltpu.roll(xb, shifts[1], 1) * s2_ref[:, sl])
            blocks.append(xb)
        x = blocks[0] if len(blocks) == 1 else jnp.concatenate(blocks, axis=-1)

    if scale != 1.0:
        x = x * scale
    x = x.astype(jnp.bfloat16)
    if transpose:
        eye = (lax.broadcasted_iota(jnp.int32, (width, width), 0)
               == lax.broadcasted_iota(jnp.int32, (width, width), 1)).astype(jnp.bfloat16)
        x = lax.dot_general(eye, x, (((1,), (1,)), ((), ())),
                            preferred_element_type=jnp.float32).astype(jnp.bfloat16)
    if ones_rows:
        x = jnp.concatenate([x, jnp.ones((ones_rows, x.shape[1]), x.dtype)], axis=0)
    return x


def _prep_body(*refs, nx, pieces, has_gain, has_tabs, **head_kw):
    x_refs = refs[:nx]
    pos = nx
    g_ref = None
    if has_gain:
        g_ref = refs[pos]
        pos += 1
    tabs = None
    if has_tabs:
        tabs = refs[pos:pos + 3]
        pos += 3
    o_ref = refs[pos]
    for h, head_pieces in enumerate(pieces):
        cols = [x_refs[src][:, off:off + LANES] for src, off in head_pieces]
        x = cols[0] if len(cols) == 1 else jnp.concatenate(cols, axis=-1)
        y = _prep_head(x, g_ref, tabs, **head_kw)
        o_ref[h] = y.reshape(o_ref.shape[1:])


def _prep(srcs, pieces, *, out_kind, tm, gain=None, norm=None, n_norm=None, tabs=None,
          rope_blocks=(), shifts=(0, 0), scale=1.0, name="prep"):
    s = srcs[0][0].shape[0]
    n_heads = len(pieces)
    width = LANES * len(pieces[0])
    in_specs, args = [], []
    for arr, bw, idx in srcs:
        in_specs.append(pl.BlockSpec((tm, bw), functools.partial(lambda i, idx: (i, idx), idx=idx)))
        args.append(arr)
    if norm is not None:
        in_specs.append(pl.BlockSpec((1, width), lambda i: (0, 0)))
        args.append(gain.reshape(1, width))
    if rope_blocks:
        for t in tabs:
            in_specs.append(pl.BlockSpec((tm, width), lambda i: (i, 0)))
            args.append(t)
    if out_kind == "rows":
        out_shape = (n_heads, s, width)
        out_spec = pl.BlockSpec((n_heads, tm, width), lambda i: (0, i, 0))
    elif out_kind == "cols":
        out_shape = (n_heads, s // tm, width, tm)
        out_spec = pl.BlockSpec((n_heads, 1, width, tm), lambda i: (0, i, 0, 0))
    else:
        out_shape = (n_heads, s // tm, width + ONES_ROWS, tm)
        out_spec = pl.BlockSpec((n_heads, 1, width + ONES_ROWS, tm), lambda i: (0, i, 0, 0))
    return pl.pallas_call(
        functools.partial(_prep_body, nx=len(srcs), pieces=tuple(tuple(p) for p in pieces),
                          has_gain=norm is not None, has_tabs=bool(rope_blocks),
                          norm=norm, n_norm=n_norm, rope_blocks=tuple(rope_blocks), shifts=shifts,
                          scale=scale, transpose=out_kind != "rows",
                          ones_rows=ONES_ROWS if out_kind == "chunks" else 0),
        grid=(s // tm,),
        in_specs=in_specs,
        out_specs=out_spec,
        out_shape=jax.ShapeDtypeStruct(out_shape, jnp.bfloat16),
        compiler_params=_cparams(("parallel",)),
        name=name,
    )(*args)


def _attn_body(*refs, nchunk, tk, tq, diff, lambda_init):
    if diff:
        (qT_ref, k_ref, vT_ref, shift_ref, lam_ref, sub_ref, o_ref,
         acc_ref, p_ref, qq_ref, m_ref, s_ref, ps_ref) = refs
    else:
        qT_ref, k_ref, vT_ref, shift_ref, o_ref, acc_ref, p_ref, qq_ref, m_ref, s_ref, ps_ref = refs

    i = pl.program_id(1)
    nqb, _, nq = qq_ref.shape
    dv = o_ref.shape[1]
    tiles_per_block = tq // qT_ref.shape[3]
    chunk_bits = nchunk.bit_length() - 1
    assert nchunk == 1 << chunk_bits

    def scores(step):
        chunk = step & (nchunk - 1)
        block = jnp.minimum(i + (step >> chunk_bits), nqb - 1)
        start = pl.multiple_of(chunk * tk, tk)
        return jnp.dot(k_ref[0, pl.ds(start, tk), :], qq_ref[block],
                       preferred_element_type=jnp.float32)

    def probs(step):
        return jnp.exp2(scores(step) - shift_ref[...]).astype(jnp.bfloat16)

    @pl.when(i == 0)
    def _():
        for b in range(nqb):
            tiles = [qT_ref[0, b * tiles_per_block + c] for c in range(tiles_per_block)]
            q = tiles[0] if len(tiles) == 1 else jnp.concatenate(tiles, axis=1)
            if diff:
                first = lax.broadcasted_iota(jnp.int32, q.shape, 0) < B_QK_DIM
                zero = jnp.zeros_like(q)
                q = jnp.concatenate([jnp.where(first, q, zero), jnp.where(first, zero, q)], axis=1)
            qq_ref[b] = q
        p_ref[0] = probs(0)

    acc_ref[...] = jnp.zeros(acc_ref.shape, jnp.float32)

    def trip(t, carry):
        for u in range(ATTN_CHUNKS_PER_TRIP):
            j = t * ATTN_CHUNKS_PER_TRIP + u
            p_ref[(u + 1) % 2] = probs(j + 1)
            acc_ref[...] += jnp.dot(vT_ref[0, j], p_ref[u % 2], preferred_element_type=jnp.float32)
        return carry

    lax.fori_loop(0, nchunk // ATTN_CHUNKS_PER_TRIP, trip, 0)

    healthy = jnp.min(acc_ref[dv:dv + 1, :]) >= ATTN_MIN_DENOM

    @pl.when(jnp.logical_not(healthy))
    def _():
        rb = ATTN_ROW_BLOCK
        groups = rb // SUBLANES
        m_ref[...] = jnp.full(m_ref.shape, -jnp.inf, jnp.float32)
        acc_ref[...] = jnp.zeros(acc_ref.shape, jnp.float32)

        def block(r):
            return s_ref[r * rb:(r + 1) * rb, :].reshape(groups, SUBLANES, nq)

        def chunk(j, carry):
            s_ref[...] = scores(j)
            mx = jnp.max(block(0), axis=0)
            for r in range(1, tk // rb):
                mx = jnp.maximum(mx, jnp.max(block(r), axis=0))
            m_old = m_ref[...]
            m_new = jnp.maximum(m_old, jnp.max(mx, axis=0, keepdims=True))
            alpha = jnp.exp2(m_old - m_new)
            m_rows = jnp.broadcast_to(m_new, (SUBLANES, nq))[None]
            for r in range(tk // rb):
                p = jnp.exp2(block(r) - m_rows).reshape(rb, nq)
                ps_ref[r * rb:(r + 1) * rb, :] = p.astype(jnp.bfloat16)
            pv = jnp.dot(vT_ref[0, j], ps_ref[...], preferred_element_type=jnp.float32)
            acc_ref[...] = acc_ref[...] * alpha + pv
            m_ref[...] = m_new
            return carry

        lax.fori_loop(0, nchunk, chunk, 0)

    o = acc_ref[:dv, :] / acc_ref[dv:dv + 1, :]
    if diff:
        lp = lam_ref[...]
        lam = (jnp.exp(jnp.sum(lp[0:1] * lp[1:2], axis=-1, keepdims=True))
               - jnp.exp(jnp.sum(lp[2:3] * lp[3:4], axis=-1, keepdims=True))
               + lambda_init)
        o = o[:, :tq] - lam * o[:, tq:]
        ms = jnp.mean(o * o, axis=0, keepdims=True)
        o = o * lax.rsqrt(ms + EPS) * sub_ref[...] * (1.0 - lambda_init)
    o_ref[...] = o.T.astype(o_ref.dtype)


def _attention(qT, k, vT, score_bound, *, tq, diff=False, lam_p=None, subln=None, lambda_init=0.0, name):
    n_heads, n_qtiles, dq, q_tile = qT.shape
    s = n_qtiles * q_tile
    n_kv = k.shape[0]
    group = n_heads // n_kv
    _, nchunk, dv_ext, tk = vT.shape
    dv = dv_ext - ONES_ROWS
    nq = 2 * tq if diff else tq
    assert tq % q_tile == 0 and ATTN_CHUNKS_PER_TRIP % 2 == 0 and nchunk % ATTN_CHUNKS_PER_TRIP == 0
    in_specs = [pl.BlockSpec((1, n_qtiles, dq, q_tile), lambda h, i: (h, 0, 0, 0)),
                pl.BlockSpec((1, s, dq), lambda h, i: (h // group, 0, 0)),
                pl.BlockSpec((1, nchunk, dv_ext, tk), lambda h, i: (h // group, 0, 0, 0)),
                pl.BlockSpec((1, nq), lambda h, i: (0, 0))]
    args = [qT, k, vT, jnp.full((1, nq), score_bound, jnp.float32)]
    if diff:
        in_specs += [pl.BlockSpec(lam_p.shape, lambda h, i: (0, 0)),
                     pl.BlockSpec((dv, 1), lambda h, i: (0, 0))]
        args += [lam_p, subln.reshape(dv, 1)]
    return pl.pallas_call(
        functools.partial(_attn_body, nchunk=nchunk, tk=tk, tq=tq, diff=diff, lambda_init=lambda_init),
        grid=(n_heads, s // tq),
        in_specs=in_specs,
        out_specs=pl.BlockSpec((tq, dv), lambda h, i: (i, h)),
        out_shape=jax.ShapeDtypeStruct((s, n_heads * dv), jnp.bfloat16),
        scratch_shapes=[pltpu.VMEM((dv_ext, nq), jnp.float32),
                        pltpu.VMEM((2, tk, nq), jnp.bfloat16),
                        pltpu.VMEM((s // tq, dq, nq), jnp.bfloat16),
                        pltpu.VMEM((1, nq), jnp.float32),
                        pltpu.VMEM((tk, nq), jnp.float32),
                        pltpu.VMEM((tk, nq), jnp.bfloat16)],
        compiler_params=_cparams(("arbitrary", "arbitrary")),
        name=name,
    )(*args)


def _rope_cos_sin(pos, dim, theta):
    inv = theta ** (-jnp.arange(0, dim, 2, dtype=jnp.float32) / dim)
    ang = pos.astype(jnp.float32)[:, None] * inv[None, :]
    return jnp.cos(ang), jnp.sin(ang)


def _rotate_half_tables(cos, sin):
    zero = jnp.zeros_like(sin)
    return (jnp.concatenate([cos, cos], axis=-1),
            jnp.concatenate([-sin, zero], axis=-1),
            jnp.concatenate([zero, sin], axis=-1))


def _rope_tables(s):
    t = jnp.arange(s, dtype=jnp.int32)
    row = t // GRID_W
    col = t % GRID_W
    half = A_HEAD_DIM // 2
    a_row = _rotate_half_tables(*_rope_cos_sin(row, half, A_ROPE_THETA))
    a_col = _rotate_half_tables(*_rope_cos_sin(col, half, A_ROPE_THETA))
    a_tabs = tuple(jnp.concatenate([r, c], axis=-1) for r, c in zip(a_row, a_col))

    b_rot = _rotate_half_tables(*_rope_cos_sin(t, B_ROPE_DIM, B_ROPE_THETA))
    rest = B_QK_DIM - B_ROPE_DIM
    fill = (jnp.ones((s, rest), jnp.float32), jnp.zeros((s, rest), jnp.float32),
            jnp.zeros((s, rest), jnp.float32))
    b_tabs = tuple(jnp.tile(jnp.concatenate([r, f], axis=-1), (1, 2)) for r, f in zip(b_rot, fill))

    c_rot = _rotate_half_tables(*_rope_cos_sin(t, C_ROPE, C_ROPE_THETA))
    ones = jnp.ones((s, C_NOPE), jnp.float32)
    zeros = jnp.zeros((s, C_NOPE), jnp.float32)
    pad = jnp.zeros((s, C_QK_PAD - C_QK_DIM), jnp.float32)
    c_tabs = (jnp.concatenate([ones, c_rot[0], pad], axis=-1),
              jnp.concatenate([zeros, c_rot[1], pad], axis=-1),
              jnp.concatenate([zeros, c_rot[2], pad], axis=-1))
    return a_tabs, b_tabs, c_tabs


def _score_bound(dim, q_gain, k_gain, scale):
    return BOUND_SLACK * dim * scale * jnp.max(jnp.abs(q_gain)) * jnp.max(jnp.abs(k_gain))


def _pad_vec(g, n):
    return jnp.pad(g, (0, n - g.shape[0]))


def kernel(x, attn_norm, w_in, a_q_norm, a_k_norm, b_q_norm, b_k_norm, b_lambda, b_subln,
           c_q_a_norm, c_kv_a_norm, c_w_q_up, c_w_kv_up, c_q_norm, c_k_norm, w_out,
           ffn_norm, w_gate, w_up, w_down):
    bsz, s, d = x.shape
    assert (bsz, s, d) == (1, SEQ, D_MODEL)
    bf = jnp.bfloat16
    a_tabs, b_tabs, c_tabs = _rope_tables(s)
    tp = PREP_ROW_TILE
    h = x.reshape(s, d)
    w_in_t = jnp.swapaxes(w_in, 1, 2)

    for l in range(DEPTH):
        lambda_init = 0.8 - 0.6 * math.exp(-0.3 * l)
        wq_up = c_w_q_up[l].astype(bf).reshape(C_Q_LORA, C_HEADS, C_QK_DIM)
        wq_up = jnp.pad(wq_up, ((0, 0), (0, 0), (0, C_QK_PAD - C_QK_DIM))).reshape(C_Q_LORA, C_HEADS * C_QK_PAD)
        wkv_up = c_w_kv_up[l].astype(bf)

        xn = _rmsnorm(h, attn_norm[l])
        proj = _matmul([xn], [w_in_t], out_dtype=jnp.float32, tm=ROW_TILE, tn=PROJ_BLOCK, layer=l,
                       w_transposed=True, n_cols=N_IN_MAIN, name="in_proj")
        c_kr = _tail_proj(xn, w_in_t, layer=l, row0=N_IN_MAIN, tm=ROW_TILE, name="in_proj_kr")

        def proj_blocks(first, count):
            return [(proj, PROJ_BLOCK, first + b) for b in range(count)]

        a_scale = LOG2E / math.sqrt(A_HEAD_DIM)
        a_q = _prep(proj_blocks(0, 3), _head_pieces(A_HEADS), out_kind="cols", tm=tp,
                    gain=a_q_norm[l], norm="full", n_norm=A_HEAD_DIM, tabs=a_tabs, rope_blocks=(0,),
                    shifts=(96, 32), scale=a_scale, name="prep_a_q")
        a_k = _prep(proj_blocks(3, 1), _head_pieces(A_KV_HEADS), out_kind="rows", tm=tp,
                    gain=a_k_norm[l], norm="full", n_norm=A_HEAD_DIM, tabs=a_tabs, rope_blocks=(0,),
                    shifts=(96, 32), name="prep_a_k")
        a_v = _prep(proj_blocks(4, 1), _head_pieces(A_KV_HEADS), out_kind="chunks", tm=ATTN_KEY_CHUNK,
                    name="prep_a_v")
        ya = _attention(a_q, a_k, a_v, _score_bound(A_HEAD_DIM, a_q_norm[l], a_k_norm[l], a_scale),
                        tq=ATTN_Q_BLOCK_A, name="attn_a")

        b_scale = LOG2E / math.sqrt(B_QK_DIM)
        b_q = _prep(proj_blocks(5, 2), _head_pieces(B_HEADS), out_kind="cols", tm=tp,
                    gain=jnp.tile(b_q_norm[l], 2), norm="half", n_norm=B_QK_DIM, tabs=b_tabs,
                    rope_blocks=(0,), shifts=(120, 8), scale=b_scale, name="prep_b_q")
        b_k = _prep(proj_blocks(7, 2), _head_pieces(B_HEADS), out_kind="rows", tm=tp,
                    gain=jnp.tile(b_k_norm[l], 2), norm="half", n_norm=B_QK_DIM, tabs=b_tabs,
                    rope_blocks=(0,), shifts=(120, 8), name="prep_b_k")
        b_v = _prep(proj_blocks(9, 2), _head_pieces(B_HEADS), out_kind="chunks", tm=ATTN_KEY_CHUNK,
                    name="prep_b_v")
        yb = _attention(b_q, b_k, b_v, _score_bound(B_QK_DIM, b_q_norm[l], b_k_norm[l], b_scale),
                        tq=ATTN_Q_BLOCK_B, diff=True, lam_p=b_lambda[l], subln=b_subln[l],
                        lambda_init=lambda_init, name="attn_b")

        cq_lat = _prep(proj_blocks(11, 2), _head_pieces(1, blocks=C_Q_LORA // LANES), out_kind="rows", tm=ROW_TILE,
                       gain=c_q_a_norm[l], norm="full", n_norm=C_Q_LORA, name="norm_c_q")[0]
        ckv_lat = _prep(proj_blocks(13, 1), _head_pieces(1, blocks=C_KV_LORA // LANES), out_kind="rows", tm=ROW_TILE,
                        gain=c_kv_a_norm[l], norm="full", n_norm=C_KV_LORA, name="norm_c_kv")[0]
        c_q_raw = _matmul([cq_lat], [wq_up], out_dtype=jnp.float32, tm=ROW_TILE, tn=C_UP_COL_TILE,
                          name="c_q_up")
        c_kv_raw = _matmul([ckv_lat], [wkv_up], out_dtype=jnp.float32, tm=ROW_TILE, tn=C_UP_COL_TILE,
                           name="c_kv_up")
        c_scale = LOG2E / math.sqrt(C_QK_DIM)
        n_up = C_HEADS * C_QK_PAD // PROJ_BLOCK
        c_q = _prep([(c_q_raw, PROJ_BLOCK, b) for b in range(n_up)],
                    _head_pieces(C_HEADS, lanes_per_head=C_QK_PAD, blocks=2), out_kind="cols", tm=tp,
                    gain=_pad_vec(c_q_norm[l], C_QK_PAD), norm="full", n_norm=C_QK_DIM, tabs=c_tabs,
                    rope_blocks=(1,), shifts=(96, 32), scale=c_scale, name="prep_c_q")
        kv_srcs = [(c_kv_raw, PROJ_BLOCK, b) for b in range(n_up)]
        c_k_pieces = [[nope, (n_up, 0)] for (nope,) in _head_pieces(C_HEADS, lanes_per_head=2 * LANES)]
        c_k = _prep(kv_srcs + [(c_kr, LANES, 0)], c_k_pieces, out_kind="rows", tm=tp,
                    gain=_pad_vec(c_k_norm[l], C_QK_PAD), norm="full", n_norm=C_QK_DIM, tabs=c_tabs,
                    rope_blocks=(1,), shifts=(96, 32), name="prep_c_k")
        c_v = _prep(kv_srcs, _head_pieces(C_HEADS, lanes_per_head=2 * LANES, lane0=LANES), out_kind="chunks",
                    tm=ATTN_KEY_CHUNK, name="prep_c_v")
        yc = _attention(c_q, c_k, c_v, _score_bound(C_QK_DIM, c_q_norm[l], c_k_norm[l], c_scale),
                        tq=ATTN_Q_BLOCK_C, name="attn_c")

        h = _matmul([ya, yb, yc], [w_out], res=h, out_dtype=jnp.float32, tm=GATE_UP_ROW_TILE, tn=FFN_COL_TILE,
                    layer=l, single_buffer_x=True, name="out_proj")

        hn = _rmsnorm(h, ffn_norm[l])
        act = _matmul([hn], [w_gate, w_up], out_dtype=bf, tm=GATE_UP_ROW_TILE, tn=FFN_COL_TILE, layer=l,
                      name="ffn_gate_up")
        h = _matmul_kouter(act, w_down, h, layer=l, tm=ROW_TILE, tn=FFN_COL_TILE, tk=FFN_HIDDEN // 2, n_split=1,
                           name="ffn_down")

    return h.reshape(bsz, s, d)
```
